```python
import math
import jax, jax.numpy as jnp
from jax import lax
import numpy as np

D_MODEL = 1024
BATCH = 8
SEQ = 2048
DEPTH = 1
DEC_BATCH = 128
DEC_SEQ = 8
PAST_LEN = 16384
PAGE_SIZE = 128

D_MIX = 2 * D_MODEL
HG_WIDTH = D_MIX // 2
HG_DK = 128
HG_DV = 128
HG_HEADS = HG_WIDTH // HG_DV
HG_CHUNK = 16
M_WIDTH = D_MIX - HG_WIDTH
M_HEADDIM = 64
M_HEADS = M_WIDTH // M_HEADDIM
M_DSTATE = 128
M_GROUPS = 2
M_CONV = 4
M_CHUNK = 128
M_CONV_DIM = M_WIDTH + 2 * M_GROUPS * M_DSTATE
D_FF = 4 * D_MODEL
NORM_EPS = 1e-5
SPLITS = (HG_HEADS * HG_DK, HG_HEADS * HG_DK, HG_WIDTH, HG_WIDTH, M_WIDTH, M_CONV_DIM, M_HEADS)
D_IN_PROJ = sum(SPLITS)
SPLIT_POINTS = tuple(sum(SPLITS[:i + 1]) for i in range(len(SPLITS) - 1))

kernel_name = 'hymba_hgrn2_ssd_decoder_step'


def rmsnorm(x, gain):
    xf = x.astype(jnp.float32)
    xf = xf * lax.rsqrt(jnp.mean(xf * xf, axis=-1, keepdims=True) + NORM_EPS)
    return (xf * gain.astype(jnp.float32)).astype(x.dtype)


def chunk_len(L, target):
    return L if L <= target else math.gcd(L, target)


def hgrn2_recurrence(q, k, v, log_f, s0):
    bsz, L, H, K = q.shape
    C = chunk_len(L, HG_CHUNK)
    n = L // C

    def blk(t):
        return t.reshape(bsz, n, C, H, t.shape[-1]).transpose(1, 0, 2, 3, 4)

    q, k, v, a = blk(q), blk(k), blk(v), blk(log_f)
    cum = jnp.cumsum(a, axis=2)
    cum_last = cum[:, :, -1:]
    qd = q * jnp.exp(cum)
    kd = k * jnp.exp(-cum)
    k_end = k * jnp.exp(cum_last - cum)
    chunk_decay = jnp.exp(cum_last[:, :, 0])
    causal = jnp.tril(jnp.ones((C, C), bool))
    att = jnp.einsum('cbthk,cbshk->cbhts', qd, kd)
    att = jnp.where(causal, att, 0.0)
    o_intra = jnp.einsum('cbhts,cbshv->cbthv', att, v)

    def step(s, inp):
        qd_c, ke_c, v_c, dec_c = inp
        o = jnp.einsum('bthk,bhkv->bthv', qd_c, s)
        s = dec_c[..., None] * s + jnp.einsum('bshk,bshv->bhkv', ke_c, v_c)
        return s, o

    s_fin, o_inter = lax.scan(step, s0, (qd, k_end, v, chunk_decay))
    o = (o_intra + o_inter).transpose(1, 0, 2, 3, 4).reshape(bsz, L, H, v.shape[-1])
    return o, s_fin


def ssd_scan(xs, dt, a, bm, cm, s0):
    bsz, L, H, P = xs.shape
    G, N = bm.shape[2], bm.shape[3]
    R = H // G
    C = chunk_len(L, M_CHUNK)
    n = L // C
    log_a = (dt * a).reshape(bsz, n, C, G, R).transpose(1, 0, 2, 3, 4)
    xdt = (xs * dt[..., None]).reshape(bsz, n, C, G, R, P).transpose(1, 0, 2, 3, 4, 5)
    bm = bm.reshape(bsz, n, C, G, N).transpose(1, 0, 2, 3, 4)
    cm = cm.reshape(bsz, n, C, G, N).transpose(1, 0, 2, 3, 4)
    cum = jnp.cumsum(log_a, axis=2)
    seg = cum[:, :, :, None] - cum[:, :, None, :]
    causal = jnp.tril(jnp.ones((C, C), bool))[:, :, None, None]
    decay = jnp.exp(jnp.where(causal, seg, -jnp.inf))
    cb = jnp.einsum('cbtgn,cbsgn->cbtsg', cm, bm)
    y_intra = jnp.einsum('cbtsg,cbtsgr,cbsgrp->cbtgrp', cb, decay, xdt)
    cum_last = cum[:, :, -1]
    to_end = jnp.exp(cum_last[:, :, None] - cum)
    from_start = jnp.exp(cum)
    chunk_decay = jnp.exp(cum_last)

    def step(s, inp):
        c_c, b_c, xdt_c, fs_c, te_c, cd_c = inp
        y = jnp.einsum('btgn,bgrpn->btgrp', c_c, s) * fs_c[..., None]
        s = cd_c[..., None, None] * s + jnp.einsum('bsgn,bsgr,bsgrp->bgrpn', b_c, te_c, xdt_c)
        return s, y

    s_fin, y_inter = lax.scan(step, s0.reshape(bsz, G, R, P, N),
                              (cm, bm, xdt, from_start, to_end, chunk_decay))
    y = (y_intra + y_inter).transpose(1, 0, 2, 3, 4, 5).reshape(bsz, L, H, P)
    return y, s_fin.reshape(bsz, H, P, N)


def decoder_layer(x, hg_s0, ssm_s0, conv0, lb, ln1, w_in, hg_norm, conv_w, conv_b,
                  dt_bias, a_log, d_skip, m_norm, w_out, ln2, w_up, w_down):
    f32 = jnp.float32
    bsz, L, _ = x.shape
    h = rmsnorm(x, ln1)
    proj = jnp.einsum('bld,de->ble', h, w_in)
    q, f_raw, i_in, g, z, xbc, dt_raw = jnp.split(proj, SPLIT_POINTS, axis=-1)

    f = lb + (1.0 - lb) * jax.nn.sigmoid(f_raw.astype(f32))
    log_f = jnp.log(f)
    k = 1.0 - f
    o_hg, hg_s = hgrn2_recurrence(
        q.astype(f32).reshape(bsz, L, HG_HEADS, HG_DK),
        k.reshape(bsz, L, HG_HEADS, HG_DK),
        i_in.astype(f32).reshape(bsz, L, HG_HEADS, HG_DV),
        log_f.reshape(bsz, L, HG_HEADS, HG_DK),
        hg_s0.astype(f32))
    o_hg = rmsnorm(o_hg, hg_norm).reshape(bsz, L, HG_WIDTH) * jax.nn.silu(g.astype(f32))

    xbc_full = jnp.concatenate([conv0.astype(xbc.dtype), xbc], axis=1)
    conv_new = xbc_full[:, L:]
    acc = conv_b.astype(f32)
    for j in range(M_CONV):
        acc = acc + xbc_full[:, j:j + L].astype(f32) * conv_w[j].astype(f32)
    xbc_act = jax.nn.silu(acc)
    xs, bm, cm = jnp.split(xbc_act, [M_WIDTH, M_WIDTH + M_GROUPS * M_DSTATE], axis=-1)
    xs = xs.reshape(bsz, L, M_HEADS, M_HEADDIM)
    bm = bm.reshape(bsz, L, M_GROUPS, M_DSTATE)
    cm = cm.reshape(bsz, L, M_GROUPS, M_DSTATE)
    dt = jax.nn.softplus(dt_raw.astype(f32) + dt_bias.astype(f32))
    a = -jnp.exp(a_log.astype(f32))
    y, ssm_s = ssd_scan(xs, dt, a, bm, cm, ssm_s0.astype(f32))
    y = y + d_skip.astype(f32)[:, None] * xs
    y = y.reshape(bsz, L, M_WIDTH) * jax.nn.silu(z.astype(f32))
    y = rmsnorm(y.reshape(bsz, L, M_GROUPS, M_WIDTH // M_GROUPS),
                m_norm.reshape(M_GROUPS, M_WIDTH // M_GROUPS)).reshape(bsz, L, M_WIDTH)

    mix = jnp.concatenate([o_hg, y], axis=-1).astype(x.dtype)
    x = x + jnp.einsum('ble,ed->bld', mix, w_out)

    u = jax.nn.relu(jnp.einsum('bld,df->blf', rmsnorm(x, ln2), w_up))
    x = x + jnp.einsum('blf,fd->bld', u * u, w_down)
    return x, hg_s.astype(x.dtype), ssm_s.astype(x.dtype), conv_new


def setup_inputs(seed: int = 0) -> dict:
    key = jax.random.key(seed)
    ks = jax.random.split(key, 24)
    nrm = jax.random.normal
    dt0 = jnp.exp(jax.random.uniform(ks[10], (DEPTH, M_HEADS), minval=math.log(1e-3), maxval=math.log(1e-1)))
    return {
        'x_prompt': nrm(ks[0], (BATCH, SEQ, D_MODEL), jnp.float32),
        'x_sample': nrm(ks[1], (DEC_BATCH, DEC_SEQ, D_MODEL), jnp.float32),
        'state_hgrn': 0.5 * nrm(ks[2], (DEPTH, DEC_BATCH, HG_HEADS, HG_DK, HG_DV), jnp.float32),
        'state_ssm': 0.3 * nrm(ks[3], (DEPTH, DEC_BATCH, M_HEADS, M_HEADDIM, M_DSTATE), jnp.float32),
        'state_conv': nrm(ks[4], (DEPTH, DEC_BATCH, M_CONV - 1, M_CONV_DIM), jnp.float32),
        'hg_lb_logits': 0.1 * nrm(ks[5], (DEPTH + 1, HG_HEADS * HG_DK), jnp.float32),
        'ln1': 1.0 + 0.02 * nrm(ks[6], (DEPTH, D_MODEL), jnp.float32),
        'w_in': nrm(ks[7], (DEPTH, D_MODEL, D_IN_PROJ), jnp.float32) * D_MODEL ** -0.5,
        'hg_norm': 1.0 + 0.02 * nrm(ks[8], (DEPTH, HG_HEADS, HG_DV), jnp.float32),
        'conv_w': nrm(ks[9], (DEPTH, M_CONV, M_CONV_DIM), jnp.float32) * M_CONV ** -0.5,
        'conv_b': 0.02 * nrm(ks[11], (DEPTH, M_CONV_DIM), jnp.float32),
        'dt_bias': dt0 + jnp.log(-jnp.expm1(-dt0)),
        'a_log': jnp.log(jax.random.uniform(ks[12], (DEPTH, M_HEADS), minval=1.0, maxval=16.0)),
        'd_skip': 1.0 + 0.1 * nrm(ks[13], (DEPTH, M_HEADS), jnp.float32),
        'm_norm': 1.0 + 0.02 * nrm(ks[14], (DEPTH, M_WIDTH), jnp.float32),
        'w_out': nrm(ks[15], (DEPTH, D_MIX, D_MODEL), jnp.float32) * D_MIX ** -0.5,
        'ln2': 1.0 + 0.02 * nrm(ks[16], (DEPTH, D_MODEL), jnp.float32),
        'w_up': nrm(ks[17], (DEPTH, D_MODEL, D_FF), jnp.float32) * D_MODEL ** -0.5,
        'w_down': nrm(ks[18], (DEPTH, D_FF, D_MODEL), jnp.float32) * D_FF ** -0.5,
        'ln_f': 1.0 + 0.02 * nrm(ks[19], (D_MODEL,), jnp.float32),
    }


def reference(x_prompt, x_sample, state_hgrn, state_ssm, state_conv, hg_lb_logits, ln1, w_in,
              hg_norm, conv_w, conv_b, dt_bias, a_log, d_skip, m_norm, w_out, ln2, w_up, w_down, ln_f):
    lb_all = jnp.cumsum(jax.nn.softmax(hg_lb_logits.astype(jnp.float32), axis=0), axis=0)
    yp, ys = x_prompt, x_sample
    bp = x_prompt.shape[0]
    hgp, hgs, ssp, sss, cvp, cvs = [], [], [], [], [], []
    for l in range(DEPTH):
        w = (ln1[l], w_in[l], hg_norm[l], conv_w[l], conv_b[l], dt_bias[l], a_log[l],
             d_skip[l], m_norm[l], w_out[l], ln2[l], w_up[l], w_down[l])
        hg0 = jnp.zeros((bp, HG_HEADS, HG_DK, HG_DV), yp.dtype)
        ssm0 = jnp.zeros((bp, M_HEADS, M_HEADDIM, M_DSTATE), yp.dtype)
        conv0 = jnp.zeros((bp, M_CONV - 1, M_CONV_DIM), yp.dtype)
        yp, h_p, s_p, c_p = decoder_layer(yp, hg0, ssm0, conv0, lb_all[l], *w)
        ys, h_s, s_s, c_s = decoder_layer(ys, state_hgrn[l], state_ssm[l], state_conv[l], lb_all[l], *w)
        hgp.append(h_p); hgs.append(h_s)
        ssp.append(s_p); sss.append(s_s)
        cvp.append(c_p); cvs.append(c_s)
    y_prompt = rmsnorm(yp, ln_f)
    y_sample = rmsnorm(ys, ln_f)
    return (y_prompt, y_sample, jnp.stack(hgp), jnp.stack(hgs), jnp.stack(ssp), jnp.stack(sss),
            jnp.stack(cvp), jnp.stack(cvs))
```

```python
import functools

import jax
import jax.numpy as jnp
from jax import lax
from jax.experimental import pallas as pl
from jax.experimental.pallas import tpu as pltpu

F32 = jnp.float32
BF16 = jnp.bfloat16

D_MODEL = 1024
DEPTH = 1
HG_HEADS = 8
HG_DK = 128
HG_DV = 128
HG_WIDTH = HG_HEADS * HG_DV
M_WIDTH = 1024
M_HEADDIM = 64
M_HEADS = M_WIDTH // M_HEADDIM
M_DSTATE = 128
M_GROUPS = 2
M_GROUP_WIDTH = M_WIDTH // M_GROUPS
M_CONV = 4
M_CONV_DIM = M_WIDTH + 2 * M_GROUPS * M_DSTATE
D_FF = 4 * D_MODEL
NORM_EPS = 1e-5

LANE = 128
SUBLANE = 8

OFF_Q, OFF_F, OFF_I, OFF_G = 0, 1024, 2048, 3072
OFF_XBC = 4096
OFF_DT = OFF_XBC + M_CONV_DIM
XD_WIDTH = 2048
OFF_Z = OFF_XBC + XD_WIDTH
W_IN_COLS = OFF_Z + M_WIDTH

VMEM_LIMIT = 56 * 1024 * 1024

NEG_BIG = -1e30


def _dot(a, b):
    return jnp.dot(a, b, preferred_element_type=F32)


def _dot_nt(a, b):
    return lax.dot_general(a, b, (((1,), (1,)), ((), ())), preferred_element_type=F32)


def _dot_tn(a, b):
    return lax.dot_general(a, b, (((0,), (0,)), ((), ())), preferred_element_type=F32)


def _sigmoid(x):
    return 1.0 / (1.0 + jnp.exp(-x))


def _silu(x):
    return x * _sigmoid(x)


def _softplus(x):
    return jnp.maximum(x, 0.0) + jnp.log1p(jnp.exp(-jnp.abs(x)))


def _rms(x, gain):
    ms = jnp.mean(x * x, axis=-1, keepdims=True)
    return x * lax.rsqrt(ms + NORM_EPS) * gain


def _lower_bound(lbl, layer):
    rows = [lbl[i:i + 1, :] for i in range(DEPTH + 1)]
    m = functools.reduce(jnp.maximum, rows)
    es = [jnp.exp(r - m) for r in rows]
    return sum(es[:layer + 1]) / sum(es)


def _cumsum_rows(x, seg):
    t = lax.broadcasted_iota(jnp.int32, x.shape, 0) & (seg - 1)
    s = 1
    while s < seg:
        x = x + jnp.where(t >= s, pltpu.roll(x, s, 0), 0.0)
        s *= 2
    return x


def _split3(x):
    h = x.astype(BF16)
    r = x - h.astype(F32)
    m = r.astype(BF16)
    l = (r - m.astype(F32)).astype(BF16)
    return h, m, l


def _expand_heads(x, e):
    h, m, l = _split3(x)
    return _dot(h, e) + _dot(m, e) + _dot(l, e)


def _prompt_mixer_kernel(x_ref, lbl_ref, ln1_ref, w_ref, hgn_ref, cw_ref, cb_ref, dtb_ref, alog_ref,
                         dsk_ref, mn_ref, e_ref,
                         mixh_ref, mixm_ref, hgs_ref, ssm_ref, conv_ref,
                         h_scr, hg_scr, z_scr, xbc_scr, dt_scr, xs_scr, bc_scr, y_scr, sth_scr, stm_scr,
                         *, tb, ch, cs):
    c = pl.program_id(1)
    last_c = pl.num_programs(1) - 1

    @pl.when(c == 0)
    def _():
        sth_scr[...] = jnp.zeros(sth_scr.shape, F32)
        stm_scr[...] = jnp.zeros(stm_scr.shape, F32)
        xbc_scr[0:SUBLANE, :] = jnp.zeros((SUBLANE, M_CONV_DIM), F32)

    @pl.when(c > 0)
    def _():
        xbc_scr[0:SUBLANE, :] = xbc_scr[tb:tb + SUBLANE, :]

    h_scr[...] = _rms(x_ref[0], ln1_ref[...]).astype(BF16)
    for n0 in range(0, OFF_XBC, 1024):
        hg_scr[:, n0:n0 + 1024] = _dot(h_scr[...], w_ref[:, n0:n0 + 1024])
    xbc_scr[SUBLANE:SUBLANE + tb, :] = _dot(h_scr[...], w_ref[:, OFF_XBC:OFF_DT])
    dt_scr[...] = _dot(h_scr[...], w_ref[:, OFF_DT:OFF_DT + LANE])
    z_scr[...] = _dot(h_scr[...], w_ref[:, OFF_Z:OFF_Z + M_WIDTH])

    lb = _lower_bound(lbl_ref[...], 0)
    tril_h = (lax.broadcasted_iota(jnp.int32, (ch, ch), 0) >= lax.broadcasted_iota(jnp.int32, (ch, ch), 1))
    for j in range(tb // ch):
        r = slice(j * ch, (j + 1) * ch)
        for hd in range(HG_HEADS):
            cl = slice(hd * LANE, (hd + 1) * LANE)
            q = hg_scr[r, OFF_Q + hd * LANE:OFF_Q + (hd + 1) * LANE]
            fr = hg_scr[r, OFF_F + hd * LANE:OFF_F + (hd + 1) * LANE]
            v = hg_scr[r, OFF_I + hd * LANE:OFF_I + (hd + 1) * LANE]
            g = hg_scr[r, OFF_G + hd * LANE:OFF_G + (hd + 1) * LANE]
            lbh = lb[:, cl]
            f = lbh + (1.0 - lbh) * _sigmoid(fr)
            a = jnp.log(f)
            k = 1.0 - f
            cum = _cumsum_rows(a, ch)
            mid = cum[ch // 2 - 1:ch // 2, :]
            last = cum[ch - 1:ch, :]
            qd_mid = q * jnp.exp(cum - mid)
            kd_mid = k * jnp.exp(mid - cum)
            qd = (qd_mid * jnp.exp(mid)).astype(BF16)
            k_end = (kd_mid * jnp.exp(last - mid)).astype(BF16)
            dec = jnp.exp(last)
            vb = v.astype(BF16)
            att = _dot_nt(qd_mid.astype(BF16), kd_mid.astype(BF16))
            att = jnp.where(tril_h, att, 0.0).astype(BF16)
            st = sth_scr[hd]
            o = _dot(att, vb) + _dot_nt(qd, st.astype(BF16))
            sth_scr[hd] = dec * st + _dot_tn(vb, k_end)
            on = _rms(o, hgn_ref[:, cl]) * _silu(g)
            mixh_ref[0, r, cl] = on.astype(BF16)

    for cbk in range(M_CONV_DIM // LANE):
        cl = slice(cbk * LANE, (cbk + 1) * LANE)
        acc = cb_ref[:, cl] + cw_ref[M_CONV - 1:M_CONV, cl] * xbc_scr[SUBLANE:SUBLANE + tb, cl]
        for d in range(1, M_CONV):
            acc = acc + cw_ref[M_CONV - 1 - d:M_CONV - d, cl] * xbc_scr[SUBLANE - d:SUBLANE - d + tb, cl]
        act = _silu(acc)
        if cbk < M_WIDTH // LANE:
            xs_scr[:, cl] = act
        else:
            bc_scr[:, (cbk * LANE - M_WIDTH):(cbk * LANE - M_WIDTH) + LANE] = act.astype(BF16)

    a_row = -jnp.exp(alog_ref[...])
    tril_m = (lax.broadcasted_iota(jnp.int32, (cs, cs), 0) >= lax.broadcasted_iota(jnp.int32, (cs, cs), 1))
    lane_lo = lax.broadcasted_iota(jnp.int32, (cs, LANE), 1) < M_HEADDIM
    for s in range(tb // cs):
        r = slice(s * cs, (s + 1) * cs)
        dtv = _softplus(dt_scr[r, :] + dtb_ref[...])
        cum = _cumsum_rows(dtv * a_row, cs)
        cum_t = cum.T
        for p in range(M_HEADS // 2):
            grp = p // (M_HEADS // 2 // M_GROUPS)
            cl = slice(p * LANE, (p + 1) * LANE)
            bm = bc_scr[r, grp * M_DSTATE:(grp + 1) * M_DSTATE]
            cm = bc_scr[r, (M_GROUPS + grp) * M_DSTATE:(M_GROUPS + grp + 1) * M_DSTATE]
            cb = _dot_nt(cm, bm)
            e_p = e_ref[:, cl]
            dt_x = _expand_heads(dtv, e_p)
            cum_x = _expand_heads(cum, e_p)
            last_x = cum_x[cs - 1:cs, :]
            xs = xs_scr[r, cl]
            xdt = xs * dt_x
            ms = []
            for hh in (2 * p, 2 * p + 1):
                seg = cum[:, hh:hh + 1] - cum_t[hh:hh + 1, :]
                ms.append((cb * jnp.exp(jnp.where(tril_m, seg, NEG_BIG))).astype(BF16))
            x_lo = jnp.where(lane_lo, xdt, 0.0).astype(BF16)
            x_hi = jnp.where(lane_lo, 0.0, xdt).astype(BF16)
            y = _dot(jnp.concatenate(ms, axis=1), jnp.concatenate([x_lo, x_hi], axis=0))
            st = stm_scr[:, cl]
            y = y + _dot(cm, st.astype(BF16)) * jnp.exp(cum_x)
            y = y + dsk_ref[:, cl] * xs
            to_end = (xdt * jnp.exp(last_x - cum_x)).astype(BF16)
            stm_scr[:, cl] = jnp.exp(last_x) * st + _dot_tn(bm, to_end)
            y_scr[r, cl] = y * _silu(z_scr[r, cl])
        for grp in range(M_GROUPS):
            gl = slice(grp * M_GROUP_WIDTH, (grp + 1) * M_GROUP_WIDTH)
            mixm_ref[0, r, gl] = _rms(y_scr[r, gl], mn_ref[:, gl]).astype(BF16)

    @pl.when(c == last_c)
    def _():
        for hd in range(HG_HEADS):
            hgs_ref[0, hd] = sth_scr[hd].T
        for p in range(M_WIDTH // LANE):
            ssm_ref[0, p * LANE:(p + 1) * LANE, :] = stm_scr[:, p * LANE:(p + 1) * LANE].T
        conv_ref[0] = xbc_scr[tb + SUBLANE - (M_CONV - 1):tb + SUBLANE, :]


def _const_spec(shape):
    nd = len(shape)
    return pl.BlockSpec(shape, lambda *_: (0,) * nd, pipeline_mode=pl.Buffered(1))


def _prompt_mixer(x, lbl, ln1, w_all, hgn, cw, cb, dtb, alog, dsk, mn, e_map, *, tb=256, ch=64, cs=128):
    bsz, seq, _ = x.shape
    nc = seq // tb
    kern = functools.partial(_prompt_mixer_kernel, tb=tb, ch=ch, cs=cs)
    params = (lbl, ln1, w_all, hgn, cw, cb, dtb, alog, dsk, mn, e_map)
    return pl.pallas_call(
        kern,
        grid=(bsz, nc),
        in_specs=[pl.BlockSpec((1, tb, D_MODEL), lambda b, c: (b, c, 0))] + [_const_spec(p.shape) for p in params],
        out_specs=[
            pl.BlockSpec((1, tb, HG_WIDTH), lambda b, c: (b, c, 0)),
            pl.BlockSpec((1, tb, M_WIDTH), lambda b, c: (b, c, 0)),
            pl.BlockSpec((1, HG_HEADS, HG_DK, HG_DV), lambda b, c: (b, 0, 0, 0)),
            pl.BlockSpec((1, M_WIDTH, M_DSTATE), lambda b, c: (b, 0, 0)),
            pl.BlockSpec((1, M_CONV - 1, M_CONV_DIM), lambda b, c: (b, 0, 0)),
        ],
        out_shape=[
            jax.ShapeDtypeStruct((bsz, seq, HG_WIDTH), BF16),
            jax.ShapeDtypeStruct((bsz, seq, M_WIDTH), BF16),
            jax.ShapeDtypeStruct((bsz, HG_HEADS, HG_DK, HG_DV), F32),
            jax.ShapeDtypeStruct((bsz, M_WIDTH, M_DSTATE), F32),
            jax.ShapeDtypeStruct((bsz, M_CONV - 1, M_CONV_DIM), F32),
        ],
        scratch_shapes=[
            pltpu.VMEM((tb, D_MODEL), BF16),
            pltpu.VMEM((tb, OFF_XBC), F32),
            pltpu.VMEM((tb, M_WIDTH), F32),
            pltpu.VMEM((tb + SUBLANE, M_CONV_DIM), F32),
            pltpu.VMEM((tb, LANE), F32),
            pltpu.VMEM((tb, M_WIDTH), F32),
            pltpu.VMEM((tb, 2 * M_GROUPS * M_DSTATE), BF16),
            pltpu.VMEM((tb, M_WIDTH), F32),
            pltpu.VMEM((HG_HEADS, HG_DV, HG_DK), F32),
            pltpu.VMEM((M_DSTATE, M_WIDTH), F32),
        ],
        compiler_params=pltpu.CompilerParams(
            dimension_semantics=("arbitrary", "arbitrary"), vmem_limit_bytes=VMEM_LIMIT),
        name="prompt_mixer",
    )(x, *params)


def _sample_in_proj_kernel(x_ref, ln1_ref, w_ref, o_ref):
    h = _rms(x_ref[...], ln1_ref[...]).astype(BF16)
    o_ref[...] = _dot(h, w_ref[...])


def _sample_in_proj(x, ln1, w_all, *, tn=1024):
    rows = x.shape[0]
    return pl.pallas_call(
        _sample_in_proj_kernel,
        grid=(W_IN_COLS // tn,),
        in_specs=[_const_spec((rows, D_MODEL)), _const_spec(ln1.shape),
                  pl.BlockSpec((D_MODEL, tn), lambda n: (0, n))],
        out_specs=pl.BlockSpec((rows, tn), lambda n: (0, n)),
        out_shape=jax.ShapeDtypeStruct((rows, W_IN_COLS), F32),
        compiler_params=pltpu.CompilerParams(
            dimension_semantics=("arbitrary",), vmem_limit_bytes=VMEM_LIMIT),
        name="sample_in_proj",
    )(x, ln1, w_all)


SEG = 8


def _block_causal_mask(rows):
    ri = lax.broadcasted_iota(jnp.int32, (rows, rows), 0)
    ci = lax.broadcasted_iota(jnp.int32, (rows, rows), 1)
    return ((ri & -SEG) == (ci & -SEG)) & (ci <= ri)


def _seg_last(x):
    rows, width = x.shape
    x3 = x.reshape(rows // SEG, SEG, width)
    return jnp.broadcast_to(x3[:, SEG - 1:SEG, :], x3.shape).reshape(rows, width)


def _decay_tail(dec8):
    h, m, l = _split3(dec8)
    t = lax.broadcasted_iota(jnp.int32, dec8.shape, 0)
    return jnp.where(t == 0, h.astype(F32), jnp.where(t == 1, m.astype(F32), jnp.where(t == 2, l.astype(F32), 0.0)))


def _ones_tail():
    t = lax.broadcasted_iota(jnp.int32, (2 * SEG, LANE), 0)
    return jnp.where((t >= SEG) & (t < SEG + 3), 1.0, 0.0).astype(BF16)


def _sample_hgrn_kernel(p_ref, s0_ref, lbl_ref, hgn_ref, mixh_ref, s8_ref,
                        qd_scr, ke_scr, dec_scr, o_scr, *, nb):
    rows = nb * SEG
    lb = _lower_bound(lbl_ref[...], 0)
    mask = _block_causal_mask(rows)
    for hd in range(HG_HEADS):
        cl = slice(hd * LANE, (hd + 1) * LANE)
        q = p_ref[:, OFF_Q + hd * LANE:OFF_Q + (hd + 1) * LANE]
        fr = p_ref[:, OFF_F + hd * LANE:OFF_F + (hd + 1) * LANE]
        v = p_ref[:, OFF_I + hd * LANE:OFF_I + (hd + 1) * LANE]
        lbh = lb[:, cl]
        f = lbh + (1.0 - lbh) * _sigmoid(fr)
        a = jnp.log(f)
        k = 1.0 - f
        cum = _cumsum_rows(a, SEG)
        last = _seg_last(cum)
        qd = (q * jnp.exp(cum)).astype(BF16)
        kd = (k * jnp.exp(-cum)).astype(BF16)
        att = jnp.where(mask, _dot_nt(qd, kd), 0.0).astype(BF16)
        o_scr[:, cl] = _dot(att, v.astype(BF16))
        qd_scr[:, cl] = qd
        ke_scr[:, cl] = k * jnp.exp(last - cum)
        dec_scr[:, cl] = jnp.exp(last)

    ones_tail = _ones_tail()
    zeros8 = jnp.zeros((SEG, LANE), F32)

    def pair_body(m, carry):
        r16 = pl.ds(pl.multiple_of(m * 2 * SEG, 2 * SEG), 2 * SEG)
        for par in range(2):
            i = 2 * m + par
            r8 = pl.ds(pl.multiple_of(i * SEG, SEG), SEG)
            for hd in range(HG_HEADS):
                cl = slice(hd * LANE, (hd + 1) * LANE)
                s0 = s0_ref[i, hd]
                oi = _dot(qd_scr[r16, cl], s0.astype(BF16))
                o_scr[r8, cl] = o_scr[r8, cl] + oi[par * SEG:(par + 1) * SEG, :]
                aug = jnp.concatenate([ke_scr[r8, cl], _decay_tail(dec_scr[r8, cl])], axis=0).astype(BF16)
                v8 = p_ref[r8, OFF_I + hd * LANE:OFF_I + (hd + 1) * LANE]
                rhs = jnp.concatenate([jnp.concatenate([v8, zeros8], axis=0).astype(BF16), ones_tail], axis=1)
                ud = _dot_tn(aug, rhs)
                s8_ref[i, hd] = ud[:, LANE:] * s0 + ud[:, :LANE]
        return carry

    lax.fori_loop(0, nb // 2, pair_body, 0)

    for hd in range(HG_HEADS):
        cl = slice(hd * LANE, (hd + 1) * LANE)
        g = p_ref[:, OFF_G + hd * LANE:OFF_G + (hd + 1) * LANE]
        mixh_ref[:, cl] = (_rms(o_scr[:, cl], hgn_ref[:, cl]) * _silu(g)).astype(BF16)


def _sample_hgrn(proj, s0, lbl, hgn, *, nb=16):
    nseq = s0.shape[0]
    rows = nb * SEG
    kern = functools.partial(_sample_hgrn_kernel, nb=nb)
    return pl.pallas_call(
        kern,
        grid=(nseq // nb,),
        in_specs=[pl.BlockSpec((rows, OFF_XBC), lambda i: (i, 0)),
                  pl.BlockSpec((nb, HG_HEADS, HG_DK, HG_DV), lambda i: (i, 0, 0, 0)),
                  _const_spec(lbl.shape), _const_spec(hgn.shape)],
        out_specs=[pl.BlockSpec((rows, HG_WIDTH), lambda i: (i, 0)),
                   pl.BlockSpec((nb, HG_HEADS, HG_DK, HG_DV), lambda i: (i, 0, 0, 0))],
        out_shape=[jax.ShapeDtypeStruct((nseq * SEG, HG_WIDTH), BF16),
                   jax.ShapeDtypeStruct(s0.shape, F32)],
        scratch_shapes=[pltpu.VMEM((rows, HG_WIDTH), BF16),
                        pltpu.VMEM((rows, HG_WIDTH), F32),
                        pltpu.VMEM((rows, HG_WIDTH), F32),
                        pltpu.VMEM((rows, HG_WIDTH), F32)],
        compiler_params=pltpu.CompilerParams(
            dimension_semantics=("arbitrary",), vmem_limit_bytes=VMEM_LIMIT),
        name="sample_hgrn",
    )(proj, s0, lbl, hgn)


def _sample_ssd_kernel(xd_ref, z_ref, hist_ref, s0_ref, cw_ref, cb_ref, dtb_ref, alog_ref, dsk_ref, mn_ref,
                       e_ref, mixm_ref, s8_ref,
                       xs_scr, bm_scr, cm_scr, u_scr, cd_scr, fs_scr, y_scr, *, nb):
    rows = nb * SEG
    t_in_seq = lax.broadcasted_iota(jnp.int32, (rows, LANE), 0) & (SEG - 1)

    for cbk in range(M_CONV_DIM // LANE):
        cl = slice(cbk * LANE, (cbk + 1) * LANE)
        x = xd_ref[:, cl]
        hist = hist_ref[:, cl]
        acc = cb_ref[:, cl] + cw_ref[M_CONV - 1:M_CONV, cl] * x
        for d in range(1, M_CONV):
            shifted = jnp.where(t_in_seq >= d, pltpu.roll(x, d, 0), pltpu.roll(hist, rows - SEG + d, 0))
            acc = acc + cw_ref[M_CONV - 1 - d:M_CONV - d, cl] * shifted
        act = _silu(acc)
        if cbk < M_WIDTH // LANE:
            xs_scr[:, cl] = act
        elif cbk < (M_WIDTH + M_GROUPS * M_DSTATE) // LANE:
            bm_scr[:, cbk * LANE - M_WIDTH:(cbk + 1) * LANE - M_WIDTH] = act
        else:
            o0 = cbk * LANE - M_WIDTH - M_GROUPS * M_DSTATE
            cm_scr[:, o0:o0 + LANE] = act.astype(BF16)

    a_row = -jnp.exp(alog_ref[...])
    dtv = _softplus(xd_ref[:, M_CONV_DIM:M_CONV_DIM + LANE] + dtb_ref[...])
    cum = _cumsum_rows(dtv * a_row, SEG)
    cum_t = cum.T
    mask = _block_causal_mask(rows)
    lane_lo = lax.broadcasted_iota(jnp.int32, (rows, LANE), 1) < M_HEADDIM
    for p in range(M_HEADS // 2):
        grp = p // (M_HEADS // 2 // M_GROUPS)
        cl = slice(p * LANE, (p + 1) * LANE)
        bm = bm_scr[:, grp * M_DSTATE:(grp + 1) * M_DSTATE].astype(BF16)
        cm = cm_scr[:, grp * M_DSTATE:(grp + 1) * M_DSTATE]
        cb = _dot_nt(cm, bm)
        e_p = e_ref[:, cl]
        dt_x = _expand_heads(dtv, e_p)
        cum_x = _expand_heads(cum, e_p)
        last_x = _seg_last(cum_x)
        xs = xs_scr[:, cl]
        xdt = xs * dt_x
        ms = []
        for hh in (2 * p, 2 * p + 1):
            seg = cum[:, hh:hh + 1] - cum_t[hh:hh + 1, :]
            ms.append((cb * jnp.exp(jnp.where(mask, seg, NEG_BIG))).astype(BF16))
        x_lo = jnp.where(lane_lo, xdt, 0.0).astype(BF16)
        x_hi = jnp.where(lane_lo, 0.0, xdt).astype(BF16)
        y = _dot(jnp.concatenate(ms, axis=1), jnp.concatenate([x_lo, x_hi], axis=0))
        y_scr[:, cl] = y + dsk_ref[:, cl] * xs
        u_scr[:, cl] = xdt * jnp.exp(last_x - cum_x)
        cd_scr[:, cl] = jnp.exp(last_x)
        fs_scr[:, cl] = jnp.exp(cum_x)

    ones_tail = _ones_tail()
    zeros8 = jnp.zeros((SEG, M_DSTATE), F32)

    def pair_body(m, carry):
        r16 = pl.ds(pl.multiple_of(m * 2 * SEG, 2 * SEG), 2 * SEG)
        for par in range(2):
            i = 2 * m + par
            r8 = pl.ds(pl.multiple_of(i * SEG, SEG), SEG)
            for grp in range(M_GROUPS):
                gl = slice(grp * M_GROUP_WIDTH, (grp + 1) * M_GROUP_WIDTH)
                nl = slice(grp * M_DSTATE, (grp + 1) * M_DSTATE)
                s0 = s0_ref[i, gl, :]
                yi = _dot_nt(cm_scr[r16, nl], s0.astype(BF16))
                y_scr[r8, gl] = y_scr[r8, gl] + yi[par * SEG:(par + 1) * SEG, :] * fs_scr[r8, gl]
                aug = jnp.concatenate([u_scr[r8, gl], _decay_tail(cd_scr[r8, gl])], axis=0).astype(BF16)
                rhs = jnp.concatenate(
                    [jnp.concatenate([bm_scr[r8, nl], zeros8], axis=0).astype(BF16), ones_tail], axis=1)
                ud = _dot_tn(aug, rhs)
                s8_ref[i, gl, :] = ud[:, M_DSTATE:] * s0 + ud[:, :M_DSTATE]
        return carry

    lax.fori_loop(0, nb // 2, pair_body, 0)

    for grp in range(M_GROUPS):
        gl = slice(grp * M_GROUP_WIDTH, (grp + 1) * M_GROUP_WIDTH)
        y = y_scr[:, gl] * _silu(z_ref[:, gl])
        mixm_ref[:, gl] = _rms(y, mn_ref[:, gl]).astype(BF16)


def _sample_ssd(proj, hist, s0, cw, cb, dtb, alog, dsk, mn, e_map, *, nb=16):
    nseq = s0.shape[0]
    rows = nb * SEG
    kern = functools.partial(_sample_ssd_kernel, nb=nb)
    params = (cw, cb, dtb, alog, dsk, mn, e_map)
    return pl.pallas_call(
        kern,
        grid=(nseq // nb,),
        in_specs=[pl.BlockSpec((rows, XD_WIDTH), lambda i: (i, OFF_XBC // XD_WIDTH)),
                  pl.BlockSpec((rows, M_WIDTH), lambda i: (i, OFF_Z // M_WIDTH)),
                  pl.BlockSpec((rows, M_CONV_DIM), lambda i: (i, 0)),
                  pl.BlockSpec((nb, M_WIDTH, M_DSTATE), lambda i: (i, 0, 0))]
                 + [_const_spec(p.shape) for p in params],
        out_specs=[pl.BlockSpec((rows, M_WIDTH), lambda i: (i, 0)),
                   pl.BlockSpec((nb, M_WIDTH, M_DSTATE), lambda i: (i, 0, 0))],
        out_shape=[jax.ShapeDtypeStruct((nseq * SEG, M_WIDTH), BF16),
                   jax.ShapeDtypeStruct(s0.shape, F32)],
        scratch_shapes=[pltpu.VMEM((rows, M_WIDTH), F32),
                        pltpu.VMEM((rows, M_GROUPS * M_DSTATE), F32),
                        pltpu.VMEM((rows, M_GROUPS * M_DSTATE), BF16),
                        pltpu.VMEM((rows, M_WIDTH), F32),
                        pltpu.VMEM((rows, M_WIDTH), F32),
                        pltpu.VMEM((rows, M_WIDTH), F32),
                        pltpu.VMEM((rows, M_WIDTH), F32)],
        compiler_params=pltpu.CompilerParams(
            dimension_semantics=("arbitrary",), vmem_limit_bytes=VMEM_LIMIT),
        name="sample_ssd",
    )(proj, proj, hist, s0, *params)


def _out_mlp_kernel(x_ref, mh_ref, mm_ref, wo_ref, ln2_ref, wu_ref, wd_ref, lnf_ref, o_ref, *, ff_tile):
    x1 = x_ref[...] + _dot(mh_ref[...], wo_ref[0:HG_WIDTH, :]) + _dot(mm_ref[...], wo_ref[HG_WIDTH:, :])
    hn = _rms(x1, ln2_ref[...]).astype(BF16)
    mlp = None
    for j in range(D_FF // ff_tile):
        u = jnp.maximum(_dot(hn, wu_ref[:, j * ff_tile:(j + 1) * ff_tile]), 0.0)
        d = _dot((u * u).astype(BF16), wd_ref[j * ff_tile:(j + 1) * ff_tile, :])
        mlp = d if mlp is None else mlp + d
    o_ref[...] = _rms(x1 + mlp, lnf_ref[...])


def _out_mlp(x, mix_h, mix_m, w_out, ln2, w_up, w_down, ln_f, *, tm=512, ff_tile=1024):
    rows = x.shape[0]
    kern = functools.partial(_out_mlp_kernel, ff_tile=ff_tile)
    row_spec = lambda w: pl.BlockSpec((tm, w), lambda i: (i, 0))
    return pl.pallas_call(
        kern,
        grid=(rows // tm,),
        in_specs=[row_spec(D_MODEL), row_spec(HG_WIDTH), row_spec(M_WIDTH),
                  _const_spec(w_out.shape), _const_spec(ln2.shape), _const_spec(w_up.shape),
                  _const_spec(w_down.shape), _const_spec(ln_f.shape)],
        out_specs=row_spec(D_MODEL),
        out_shape=jax.ShapeDtypeStruct((rows, D_MODEL), F32),
        compiler_params=pltpu.CompilerParams(
            dimension_semantics=("arbitrary",), vmem_limit_bytes=VMEM_LIMIT),
        name="out_mlp",
    )(x, mix_h, mix_m, w_out, ln2, w_up, w_down, ln_f)


def _head_expand_map():
    hrow = lax.broadcasted_iota(jnp.int32, (LANE, M_WIDTH), 0)
    chan = lax.broadcasted_iota(jnp.int32, (LANE, M_WIDTH), 1)
    return (hrow == chan // M_HEADDIM).astype(BF16)


def _pad_lanes(v):
    return jnp.pad(v, ((0, 0), (0, LANE - v.shape[1])))


def kernel(x_prompt, x_sample, state_hgrn, state_ssm, state_conv, hg_lb_logits, ln1, w_in, hg_norm, conv_w,
           conv_b, dt_bias, a_log, d_skip, m_norm, w_out, ln2, w_up, w_down, ln_f):
    l = 0
    bp, seq, _ = x_prompt.shape
    bs, dseq, _ = x_sample.shape
    assert DEPTH == 1 and dseq == SEG

    w = w_in[l]
    c_hg = HG_HEADS * HG_DK * 2 + 2 * HG_WIDTH
    c_z = c_hg + M_WIDTH
    c_xbc = c_z + M_CONV_DIM
    w_all = jnp.concatenate(
        [w[:, :c_hg], w[:, c_z:c_xbc], w[:, c_xbc:],
         jnp.zeros((D_MODEL, OFF_Z - OFF_DT - M_HEADS), w.dtype), w[:, c_hg:c_z]], axis=1).astype(BF16)
    lbl = hg_lb_logits.astype(F32)
    ln1_r = ln1[l][None, :]
    hgn_r = hg_norm[l].reshape(1, HG_WIDTH)
    cw = conv_w[l]
    cb_r = conv_b[l][None, :]
    dtb_r = _pad_lanes(dt_bias[l][None, :])
    alog_r = _pad_lanes(a_log[l][None, :])
    dsk_r = jnp.repeat(d_skip[l], M_HEADDIM)[None, :]
    mn_r = m_norm[l][None, :]
    e_map = _head_expand_map()
    mlp_w = (w_out[l].astype(BF16), ln2[l][None, :], w_up[l].astype(BF16), w_down[l].astype(BF16), ln_f[None, :])

    mixh_p, mixm_p, hgs_p, ssm_p, conv_p = _prompt_mixer(
        x_prompt, lbl, ln1_r, w_all, hgn_r, cw, cb_r, dtb_r, alog_r, dsk_r, mn_r, e_map)
    y_p = _out_mlp(x_prompt.reshape(bp * seq, D_MODEL), mixh_p.reshape(bp * seq, HG_WIDTH),
                   mixm_p.reshape(bp * seq, M_WIDTH), *mlp_w)

    xs2 = x_sample.reshape(bs * SEG, D_MODEL)
    proj_s = _sample_in_proj(xs2, ln1_r, w_all)
    mixh_s, hgs_s = _sample_hgrn(proj_s, state_hgrn[l], lbl, hgn_r)
    hist = jnp.pad(state_conv[l], ((0, 0), (SEG - (M_CONV - 1), 0), (0, 0))).reshape(bs * SEG, M_CONV_DIM)
    mixm_s, ssm_s = _sample_ssd(proj_s, hist, state_ssm[l].reshape(bs, M_WIDTH, M_DSTATE),
                                cw, cb_r, dtb_r, alog_r, dsk_r, mn_r, e_map)
    y_s = _out_mlp(xs2, mixh_s, mixm_s, *mlp_w)
    conv_s = proj_s.reshape(bs, SEG, W_IN_COLS)[:, SEG - (M_CONV - 1):, OFF_XBC:OFF_DT]

    return (y_p.reshape(bp, seq, D_MODEL), y_s.reshape(bs, SEG, D_MODEL),
            hgs_p[None], hgs_s[None],
            ssm_p.reshape(1, bp, M_HEADS, M_HEADDIM, M_DSTATE), ssm_s.reshape(1, bs, M_HEADS, M_HEADDIM, M_DSTATE),
            conv_p[None], conv_s[None])
```

```python
import functools

import jax
import jax.numpy as jnp
from jax import lax
from jax.experimental import pallas as pl
from jax.experimental.pallas import tpu as pltpu

F32 = jnp.float32
BF16 = jnp.bfloat16

D_MODEL = 1024
DEPTH = 1
HG_HEADS = 8
HG_DK = 128
HG_DV = 128
HG_WIDTH = HG_HEADS * HG_DV
M_WIDTH = 1024
M_HEADDIM = 64
M_HEADS = M_WIDTH // M_HEADDIM
M_DSTATE = 128
M_GROUPS = 2
M_GROUP_WIDTH = M_WIDTH // M_GROUPS
M_CONV = 4
M_CONV_DIM = M_WIDTH + 2 * M_GROUPS * M_DSTATE
D_FF = 4 * D_MODEL
NORM_EPS = 1e-5

LANE = 128
SUBLANE = 8

OFF_Q, OFF_F, OFF_I, OFF_G = 0, 1024, 2048, 3072
OFF_XBC = 4096
OFF_DT = OFF_XBC + M_CONV_DIM
XD_WIDTH = 2048
OFF_Z = OFF_XBC + XD_WIDTH
W_IN_COLS = OFF_Z + M_WIDTH

VMEM_LIMIT = 56 * 1024 * 1024

NEG_BIG = -1e30


def _dot(a, b):
    return jnp.dot(a, b, preferred_element_type=F32)


def _dot_nt(a, b):
    return lax.dot_general(a, b, (((1,), (1,)), ((), ())), preferred_element_type=F32)


def _dot_tn(a, b):
    return lax.dot_general(a, b, (((0,), (0,)), ((), ())), preferred_element_type=F32)


def _sigmoid(x):
    return 1.0 / (1.0 + jnp.exp(-x))


def _silu(x):
    return x * _sigmoid(x)


def _softplus(x):
    return jnp.maximum(x, 0.0) + jnp.log1p(jnp.exp(-jnp.abs(x)))


def _rms(x, gain):
    ms = jnp.mean(x * x, axis=-1, keepdims=True)
    return x * lax.rsqrt(ms + NORM_EPS) * gain


def _lower_bound(lbl, layer):
    rows = [lbl[i:i + 1, :] for i in range(DEPTH + 1)]
    m = functools.reduce(jnp.maximum, rows)
    es = [jnp.exp(r - m) for r in rows]
    return sum(es[:layer + 1]) / sum(es)


def _cumsum_rows(x, seg):
    t = lax.broadcasted_iota(jnp.int32, x.shape, 0) & (seg - 1)
    s = 1
    while s < seg:
        x = x + jnp.where(t >= s, pltpu.roll(x, s, 0), 0.0)
        s *= 2
    return x


def _split3(x):
    h = x.astype(BF16)
    r = x - h.astype(F32)
    m = r.astype(BF16)
    l = (r - m.astype(F32)).astype(BF16)
    return h, m, l


def _lane_bcast(x, lane):
    return jnp.broadcast_to(x[:, lane:lane + 1], x.shape)


def _prompt_mixer_kernel(x_ref, lbl_ref, ln1_ref, w_ref, hgn_ref, cw_ref, cb_ref, dtb_ref, alog_ref,
                         dsk_ref, mn_ref,
                         mixh_ref, mixm_ref, hgs_ref, ssm_ref, conv_ref,
                         h_scr, hg_scr, z_scr, xbc_scr, dt_scr, xs_scr, bc_scr, y_scr, sth_scr, stm_scr,
                         *, tb, ch, cs):
    c = pl.program_id(1)
    last_c = pl.num_programs(1) - 1

    @pl.when(c == 0)
    def _():
        sth_scr[...] = jnp.zeros(sth_scr.shape, F32)
        stm_scr[...] = jnp.zeros(stm_scr.shape, F32)
        xbc_scr[0:SUBLANE, :] = jnp.zeros((SUBLANE, M_CONV_DIM), F32)

    @pl.when(c > 0)
    def _():
        xbc_scr[0:SUBLANE, :] = xbc_scr[tb:tb + SUBLANE, :]

    h_scr[...] = _rms(x_ref[0], ln1_ref[...]).astype(BF16)
    for n0 in range(0, OFF_XBC, 1024):
        hg_scr[:, n0:n0 + 1024] = _dot(h_scr[...], w_ref[:, n0:n0 + 1024])
    xbc_scr[SUBLANE:SUBLANE + tb, :] = _dot(h_scr[...], w_ref[:, OFF_XBC:OFF_DT])
    dt_scr[...] = _dot(h_scr[...], w_ref[:, OFF_DT:OFF_DT + LANE])
    z_scr[...] = _dot(h_scr[...], w_ref[:, OFF_Z:OFF_Z + M_WIDTH])

    lb = _lower_bound(lbl_ref[...], 0)
    tril_h = (lax.broadcasted_iota(jnp.int32, (ch, ch), 0) >= lax.broadcasted_iota(jnp.int32, (ch, ch), 1))
    for j in range(tb // ch):
        r = slice(j * ch, (j + 1) * ch)
        for hd in range(HG_HEADS):
            cl = slice(hd * LANE, (hd + 1) * LANE)
            q = hg_scr[r, OFF_Q + hd * LANE:OFF_Q + (hd + 1) * LANE]
            fr = hg_scr[r, OFF_F + hd * LANE:OFF_F + (hd + 1) * LANE]
            v = hg_scr[r, OFF_I + hd * LANE:OFF_I + (hd + 1) * LANE]
            g = hg_scr[r, OFF_G + hd * LANE:OFF_G + (hd + 1) * LANE]
            lbh = lb[:, cl]
            f = lbh + (1.0 - lbh) * _sigmoid(fr)
            a = jnp.log(f)
            k = 1.0 - f
            cum = _cumsum_rows(a, ch)
            mid = cum[ch // 2 - 1:ch // 2, :]
            last = cum[ch - 1:ch, :]
            qd_mid = q * jnp.exp(cum - mid)
            kd_mid = k * jnp.exp(mid - cum)
            qd = (qd_mid * jnp.exp(mid)).astype(BF16)
            k_end = (kd_mid * jnp.exp(last - mid)).astype(BF16)
            dec = jnp.exp(last)
            vb = v.astype(BF16)
            att = _dot_nt(qd_mid.astype(BF16), kd_mid.astype(BF16))
            att = jnp.where(tril_h, att, 0.0).astype(BF16)
            st = sth_scr[hd]
            o = _dot(att, vb) + _dot_nt(qd, st.astype(BF16))
            sth_scr[hd] = dec * st + _dot_tn(vb, k_end)
            on = _rms(o, hgn_ref[:, cl]) * _silu(g)
            mixh_ref[0, r, cl] = on.astype(BF16)

    for cbk in range(M_CONV_DIM // LANE):
        cl = slice(cbk * LANE, (cbk + 1) * LANE)
        acc = cb_ref[:, cl] + cw_ref[M_CONV - 1:M_CONV, cl] * xbc_scr[SUBLANE:SUBLANE + tb, cl]
        for d in range(1, M_CONV):
            acc = acc + cw_ref[M_CONV - 1 - d:M_CONV - d, cl] * xbc_scr[SUBLANE - d:SUBLANE - d + tb, cl]
        act = _silu(acc)
        if cbk < M_WIDTH // LANE:
            xs_scr[:, cl] = act
        else:
            bc_scr[:, (cbk * LANE - M_WIDTH):(cbk * LANE - M_WIDTH) + LANE] = act.astype(BF16)

    a_row = -jnp.exp(alog_ref[...])
    tril_m = (lax.broadcasted_iota(jnp.int32, (cs, cs), 0) >= lax.broadcasted_iota(jnp.int32, (cs, cs), 1))
    lane_lo = lax.broadcasted_iota(jnp.int32, (cs, LANE), 1) < M_HEADDIM
    for s in range(tb // cs):
        r = slice(s * cs, (s + 1) * cs)
        dtv = _softplus(dt_scr[r, :] + dtb_ref[...])
        cum = _cumsum_rows(dtv * a_row, cs)
        cum_t = cum.T
        cbs = {}
        for p in range(M_HEADS // 2):
            grp = p // (M_HEADS // 2 // M_GROUPS)
            cl = slice(p * LANE, (p + 1) * LANE)
            bm = bc_scr[r, grp * M_DSTATE:(grp + 1) * M_DSTATE]
            cm = bc_scr[r, (M_GROUPS + grp) * M_DSTATE:(M_GROUPS + grp + 1) * M_DSTATE]
            if grp not in cbs:
                cbs[grp] = _dot_nt(cm, bm)
            cb = cbs[grp]
            cum_h = [_lane_bcast(cum, hh) for hh in (2 * p, 2 * p + 1)]
            cum_x = jnp.where(lane_lo, cum_h[0], cum_h[1])
            dt_x = jnp.where(lane_lo, _lane_bcast(dtv, 2 * p), _lane_bcast(dtv, 2 * p + 1))
            last_x = cum_x[cs - 1:cs, :]
            xs = xs_scr[r, cl]
            xdt = xs * dt_x
            ms = []
            for i, hh in enumerate((2 * p, 2 * p + 1)):
                seg = cum_h[i] - cum_t[hh:hh + 1, :]
                ms.append((cb * jnp.exp(jnp.where(tril_m, seg, NEG_BIG))).astype(BF16))
            x_lo = jnp.where(lane_lo, xdt, 0.0).astype(BF16)
            x_hi = jnp.where(lane_lo, 0.0, xdt).astype(BF16)
            y = _dot(jnp.concatenate(ms, axis=1), jnp.concatenate([x_lo, x_hi], axis=0))
            st = stm_scr[:, cl]
            y = y + _dot(cm, st.astype(BF16)) * jnp.exp(cum_x)
            y = y + dsk_ref[:, cl] * xs
            to_end = (xdt * jnp.exp(last_x - cum_x)).astype(BF16)
            stm_scr[:, cl] = jnp.exp(last_x) * st + _dot_tn(bm, to_end)
            y_scr[r, cl] = y * _silu(z_scr[r, cl])
        for grp in range(M_GROUPS):
            gl = slice(grp * M_GROUP_WIDTH, (grp + 1) * M_GROUP_WIDTH)
            mixm_ref[0, r, gl] = _rms(y_scr[r, gl], mn_ref[:, gl]).astype(BF16)

    @pl.when(c == last_c)
    def _():
        for hd in range(HG_HEADS):
            hgs_ref[0, hd] = sth_scr[hd].T
        for p in range(M_WIDTH // LANE):
            ssm_ref[0, p * LANE:(p + 1) * LANE, :] = stm_scr[:, p * LANE:(p + 1) * LANE].T
        conv_ref[0] = xbc_scr[tb + SUBLANE - (M_CONV - 1):tb + SUBLANE, :]


def _const_spec(shape):
    nd = len(shape)
    return pl.BlockSpec(shape, lambda *_: (0,) * nd, pipeline_mode=pl.Buffered(1))


def _prompt_mixer(x, lbl, ln1, w_all, hgn, cw, cb, dtb, alog, dsk, mn, *, tb=256, ch=64, cs=128):
    bsz, seq, _ = x.shape
    nc = seq // tb
    kern = functools.partial(_prompt_mixer_kernel, tb=tb, ch=ch, cs=cs)
    params = (lbl, ln1, w_all, hgn, cw, cb, dtb, alog, dsk, mn)
    return pl.pallas_call(
        kern,
        grid=(bsz, nc),
        in_specs=[pl.BlockSpec((1, tb, D_MODEL), lambda b, c: (b, c, 0))] + [_const_spec(p.shape) for p in params],
        out_specs=[
            pl.BlockSpec((1, tb, HG_WIDTH), lambda b, c: (b, c, 0)),
            pl.BlockSpec((1, tb, M_WIDTH), lambda b, c: (b, c, 0)),
            pl.BlockSpec((1, HG_HEADS, HG_DK, HG_DV), lambda b, c: (b, 0, 0, 0)),
            pl.BlockSpec((1, M_WIDTH, M_DSTATE), lambda b, c: (b, 0, 0)),
            pl.BlockSpec((1, M_CONV - 1, M_CONV_DIM), lambda b, c: (b, 0, 0)),
        ],
        out_shape=[
            jax.ShapeDtypeStruct((bsz, seq, HG_WIDTH), BF16),
            jax.ShapeDtypeStruct((bsz, seq, M_WIDTH), BF16),
            jax.ShapeDtypeStruct((bsz, HG_HEADS, HG_DK, HG_DV), F32),
            jax.ShapeDtypeStruct((bsz, M_WIDTH, M_DSTATE), F32),
            jax.ShapeDtypeStruct((bsz, M_CONV - 1, M_CONV_DIM), F32),
        ],
        scratch_shapes=[
            pltpu.VMEM((tb, D_MODEL), BF16),
            pltpu.VMEM((tb, OFF_XBC), F32),
            pltpu.VMEM((tb, M_WIDTH), F32),
            pltpu.VMEM((tb + SUBLANE, M_CONV_DIM), F32),
            pltpu.VMEM((tb, LANE), F32),
            pltpu.VMEM((tb, M_WIDTH), F32),
            pltpu.VMEM((tb, 2 * M_GROUPS * M_DSTATE), BF16),
            pltpu.VMEM((tb, M_WIDTH), F32),
            pltpu.VMEM((HG_HEADS, HG_DV, HG_DK), F32),
            pltpu.VMEM((M_DSTATE, M_WIDTH), F32),
        ],
        compiler_params=pltpu.CompilerParams(
            dimension_semantics=("arbitrary", "arbitrary"), vmem_limit_bytes=VMEM_LIMIT),
        name="prompt_mixer",
    )(x, *params)


def _sample_in_proj_kernel(x_ref, ln1_ref, w_ref, o_ref):
    h = _rms(x_ref[...], ln1_ref[...]).astype(BF16)
    o_ref[...] = _dot(h, w_ref[...])


def _sample_in_proj(x, ln1, w_all, *, tn=1024):
    rows = x.shape[0]
    return pl.pallas_call(
        _sample_in_proj_kernel,
        grid=(W_IN_COLS // tn,),
        in_specs=[_const_spec((rows, D_MODEL)), _const_spec(ln1.shape),
                  pl.BlockSpec((D_MODEL, tn), lambda n: (0, n))],
        out_specs=pl.BlockSpec((rows, tn), lambda n: (0, n)),
        out_shape=jax.ShapeDtypeStruct((rows, W_IN_COLS), F32),
        compiler_params=pltpu.CompilerParams(
            dimension_semantics=("arbitrary",), vmem_limit_bytes=VMEM_LIMIT),
        name="sample_in_proj",
    )(x, ln1, w_all)


SEG = 8


def _block_causal_mask(rows):
    ri = lax.broadcasted_iota(jnp.int32, (rows, rows), 0)
    ci = lax.broadcasted_iota(jnp.int32, (rows, rows), 1)
    return ((ri & -SEG) == (ci & -SEG)) & (ci <= ri)


def _seg_last(x):
    rows, width = x.shape
    x3 = x.reshape(rows // SEG, SEG, width)
    return jnp.broadcast_to(x3[:, SEG - 1:SEG, :], x3.shape).reshape(rows, width)


def _decay_tail(dec8):
    h, m, l = _split3(dec8)
    t = lax.broadcasted_iota(jnp.int32, dec8.shape, 0)
    return jnp.where(t == 0, h.astype(F32), jnp.where(t == 1, m.astype(F32), jnp.where(t == 2, l.astype(F32), 0.0)))


def _ones_tail():
    t = lax.broadcasted_iota(jnp.int32, (2 * SEG, LANE), 0)
    return jnp.where((t >= SEG) & (t < SEG + 3), 1.0, 0.0).astype(BF16)


def _sample_hgrn_kernel(p_ref, s0_ref, lbl_ref, hgn_ref, mixh_ref, s8_ref,
                        qd_scr, ke_scr, dec_scr, o_scr, *, nb):
    rows = nb * SEG
    lb = _lower_bound(lbl_ref[...], 0)
    mask = _block_causal_mask(rows)
    for hd in range(HG_HEADS):
        cl = slice(hd * LANE, (hd + 1) * LANE)
        q = p_ref[:, OFF_Q + hd * LANE:OFF_Q + (hd + 1) * LANE]
        fr = p_ref[:, OFF_F + hd * LANE:OFF_F + (hd + 1) * LANE]
        v = p_ref[:, OFF_I + hd * LANE:OFF_I + (hd + 1) * LANE]
        lbh = lb[:, cl]
        f = lbh + (1.0 - lbh) * _sigmoid(fr)
        a = jnp.log(f)
        k = 1.0 - f
        cum = _cumsum_rows(a, SEG)
        last = _seg_last(cum)
        qd = (q * jnp.exp(cum)).astype(BF16)
        kd = (k * jnp.exp(-cum)).astype(BF16)
        att = jnp.where(mask, _dot_nt(qd, kd), 0.0).astype(BF16)
        o_scr[:, cl] = _dot(att, v.astype(BF16))
        qd_scr[:, cl] = qd
        ke_scr[:, cl] = k * jnp.exp(last - cum)
        dec_scr[:, cl] = jnp.exp(last)

    ones_tail = _ones_tail()
    zeros8 = jnp.zeros((SEG, LANE), F32)

    def pair_body(m, carry):
        r16 = pl.ds(pl.multiple_of(m * 2 * SEG, 2 * SEG), 2 * SEG)
        for par in range(2):
            i = 2 * m + par
            r8 = pl.ds(pl.multiple_of(i * SEG, SEG), SEG)
            for hd in range(HG_HEADS):
                cl = slice(hd * LANE, (hd + 1) * LANE)
                s0 = s0_ref[i, hd]
                oi = _dot(qd_scr[r16, cl], s0.astype(BF16))
                o_scr[r8, cl] = o_scr[r8, cl] + oi[par * SEG:(par + 1) * SEG, :]
                aug = jnp.concatenate([ke_scr[r8, cl], _decay_tail(dec_scr[r8, cl])], axis=0).astype(BF16)
                v8 = p_ref[r8, OFF_I + hd * LANE:OFF_I + (hd + 1) * LANE]
                rhs = jnp.concatenate([jnp.concatenate([v8, zeros8], axis=0).astype(BF16), ones_tail], axis=1)
                ud = _dot_tn(aug, rhs)
                s8_ref[i, hd] = ud[:, LANE:] * s0 + ud[:, :LANE]
        return carry

    lax.fori_loop(0, nb // 2, pair_body, 0)

    for hd in range(HG_HEADS):
        cl = slice(hd * LANE, (hd + 1) * LANE)
        g = p_ref[:, OFF_G + hd * LANE:OFF_G + (hd + 1) * LANE]
        mixh_ref[:, cl] = (_rms(o_scr[:, cl], hgn_ref[:, cl]) * _silu(g)).astype(BF16)


def _sample_hgrn(proj, s0, lbl, hgn, *, nb=16):
    nseq = s0.shape[0]
    rows = nb * SEG
    kern = functools.partial(_sample_hgrn_kernel, nb=nb)
    return pl.pallas_call(
        kern,
        grid=(nseq // nb,),
        in_specs=[pl.BlockSpec((rows, OFF_XBC), lambda i: (i, 0)),
                  pl.BlockSpec((nb, HG_HEADS, HG_DK, HG_DV), lambda i: (i, 0, 0, 0)),
                  _const_spec(lbl.shape), _const_spec(hgn.shape)],
        out_specs=[pl.BlockSpec((rows, HG_WIDTH), lambda i: (i, 0)),
                   pl.BlockSpec((nb, HG_HEADS, HG_DK, HG_DV), lambda i: (i, 0, 0, 0))],
        out_shape=[jax.ShapeDtypeStruct((nseq * SEG, HG_WIDTH), BF16),
                   jax.ShapeDtypeStruct(s0.shape, F32)],
        scratch_shapes=[pltpu.VMEM((rows, HG_WIDTH), BF16),
                        pltpu.VMEM((rows, HG_WIDTH), F32),
                        pltpu.VMEM((rows, HG_WIDTH), F32),
                        pltpu.VMEM((rows, HG_WIDTH), F32)],
        compiler_params=pltpu.CompilerParams(
            dimension_semantics=("arbitrary",), vmem_limit_bytes=VMEM_LIMIT),
        name="sample_hgrn",
    )(proj, s0, lbl, hgn)


def _sample_ssd_kernel(xd_ref, z_ref, hist_ref, s0_ref, cw_ref, cb_ref, dtb_ref, alog_ref, dsk_ref, mn_ref,
                       mixm_ref, s8_ref,
                       xs_scr, bm_scr, cm_scr, u_scr, cd_scr, fs_scr, y_scr, *, nb):
    rows = nb * SEG
    t_in_seq = lax.broadcasted_iota(jnp.int32, (rows, LANE), 0) & (SEG - 1)

    for cbk in range(M_CONV_DIM // LANE):
        cl = slice(cbk * LANE, (cbk + 1) * LANE)
        x = xd_ref[:, cl]
        hist = hist_ref[:, cl]
        acc = cb_ref[:, cl] + cw_ref[M_CONV - 1:M_CONV, cl] * x
        for d in range(1, M_CONV):
            shifted = jnp.where(t_in_seq >= d, pltpu.roll(x, d, 0), pltpu.roll(hist, rows - SEG + d, 0))
            acc = acc + cw_ref[M_CONV - 1 - d:M_CONV - d, cl] * shifted
        act = _silu(acc)
        if cbk < M_WIDTH // LANE:
            xs_scr[:, cl] = act
        elif cbk < (M_WIDTH + M_GROUPS * M_DSTATE) // LANE:
            bm_scr[:, cbk * LANE - M_WIDTH:(cbk + 1) * LANE - M_WIDTH] = act
        else:
            o0 = cbk * LANE - M_WIDTH - M_GROUPS * M_DSTATE
            cm_scr[:, o0:o0 + LANE] = act.astype(BF16)

    a_row = -jnp.exp(alog_ref[...])
    dtv = _softplus(xd_ref[:, M_CONV_DIM:M_CONV_DIM + LANE] + dtb_ref[...])
    cum = _cumsum_rows(dtv * a_row, SEG)
    cum_t = cum.T
    mask = _block_causal_mask(rows)
    lane_lo = lax.broadcasted_iota(jnp.int32, (rows, LANE), 1) < M_HEADDIM
    cbs = {}
    for p in range(M_HEADS // 2):
        grp = p // (M_HEADS // 2 // M_GROUPS)
        cl = slice(p * LANE, (p + 1) * LANE)
        if grp not in cbs:
            bm = bm_scr[:, grp * M_DSTATE:(grp + 1) * M_DSTATE].astype(BF16)
            cbs[grp] = _dot_nt(cm_scr[:, grp * M_DSTATE:(grp + 1) * M_DSTATE], bm)
        cb = cbs[grp]
        cum_h = [_lane_bcast(cum, hh) for hh in (2 * p, 2 * p + 1)]
        cum_x = jnp.where(lane_lo, cum_h[0], cum_h[1])
        dt_x = jnp.where(lane_lo, _lane_bcast(dtv, 2 * p), _lane_bcast(dtv, 2 * p + 1))
        last_x = _seg_last(cum_x)
        xs = xs_scr[:, cl]
        xdt = xs * dt_x
        ms = []
        for i, hh in enumerate((2 * p, 2 * p + 1)):
            seg = cum_h[i] - cum_t[hh:hh + 1, :]
            ms.append((cb * jnp.exp(jnp.where(mask, seg, NEG_BIG))).astype(BF16))
        x_lo = jnp.where(lane_lo, xdt, 0.0).astype(BF16)
        x_hi = jnp.where(lane_lo, 0.0, xdt).astype(BF16)
        y = _dot(jnp.concatenate(ms, axis=1), jnp.concatenate([x_lo, x_hi], axis=0))
        y_scr[:, cl] = y + dsk_ref[:, cl] * xs
        u_scr[:, cl] = xdt * jnp.exp(last_x - cum_x)
        cd_scr[:, cl] = jnp.exp(last_x)
        fs_scr[:, cl] = jnp.exp(cum_x)

    ones_tail = _ones_tail()
    zeros8 = jnp.zeros((SEG, M_DSTATE), F32)

    def pair_body(m, carry):
        r16 = pl.ds(pl.multiple_of(m * 2 * SEG, 2 * SEG), 2 * SEG)
        for par in range(2):
            i = 2 * m + par
            r8 = pl.ds(pl.multiple_of(i * SEG, SEG), SEG)
            for grp in range(M_GROUPS):
                gl = slice(grp * M_GROUP_WIDTH, (grp + 1) * M_GROUP_WIDTH)
                nl = slice(grp * M_DSTATE, (grp + 1) * M_DSTATE)
                s0 = s0_ref[i, gl, :]
                yi = _dot_nt(cm_scr[r16, nl], s0.astype(BF16))
                y_scr[r8, gl] = y_scr[r8, gl] + yi[par * SEG:(par + 1) * SEG, :] * fs_scr[r8, gl]
                aug = jnp.concatenate([u_scr[r8, gl], _decay_tail(cd_scr[r8, gl])], axis=0).astype(BF16)
                rhs = jnp.concatenate(
                    [jnp.concatenate([bm_scr[r8, nl], zeros8], axis=0).astype(BF16), ones_tail], axis=1)
                ud = _dot_tn(aug, rhs)
                s8_ref[i, gl, :] = ud[:, M_DSTATE:] * s0 + ud[:, :M_DSTATE]
        return carry

    lax.fori_loop(0, nb // 2, pair_body, 0)

    for grp in range(M_GROUPS):
        gl = slice(grp * M_GROUP_WIDTH, (grp + 1) * M_GROUP_WIDTH)
        y = y_scr[:, gl] * _silu(z_ref[:, gl])
        mixm_ref[:, gl] = _rms(y, mn_ref[:, gl]).astype(BF16)


def _sample_ssd(proj, hist, s0, cw, cb, dtb, alog, dsk, mn, *, nb=16):
    nseq = s0.shape[0]
    rows = nb * SEG
    kern = functools.partial(_sample_ssd_kernel, nb=nb)
    params = (cw, cb, dtb, alog, dsk, mn)
    return pl.pallas_call(
        kern,
        grid=(nseq // nb,),
        in_specs=[pl.BlockSpec((rows, XD_WIDTH), lambda i: (i, OFF_XBC // XD_WIDTH)),
                  pl.BlockSpec((rows, M_WIDTH), lambda i: (i, OFF_Z // M_WIDTH)),
                  pl.BlockSpec((rows, M_CONV_DIM), lambda i: (i, 0)),
                  pl.BlockSpec((nb, M_WIDTH, M_DSTATE), lambda i: (i, 0, 0))]
                 + [_const_spec(p.shape) for p in params],
        out_specs=[pl.BlockSpec((rows, M_WIDTH), lambda i: (i, 0)),
                   pl.BlockSpec((nb, M_WIDTH, M_DSTATE), lambda i: (i, 0, 0))],
        out_shape=[jax.ShapeDtypeStruct((nseq * SEG, M_WIDTH), BF16),
                   jax.ShapeDtypeStruct(s0.shape, F32)],
        scratch_shapes=[pltpu.VMEM((rows, M_WIDTH), F32),
                        pltpu.VMEM((rows, M_GROUPS * M_DSTATE), F32),
                        pltpu.VMEM((rows, M_GROUPS * M_DSTATE), BF16),
                        pltpu.VMEM((rows, M_WIDTH), F32),
                        pltpu.VMEM((rows, M_WIDTH), F32),
                        pltpu.VMEM((rows, M_WIDTH), F32),
                        pltpu.VMEM((rows, M_WIDTH), F32)],
        compiler_params=pltpu.CompilerParams(
            dimension_semantics=("arbitrary",), vmem_limit_bytes=VMEM_LIMIT),
        name="sample_ssd",
    )(proj, proj, hist, s0, *params)


def _out_mlp_kernel(x_ref, mh_ref, mm_ref, wo_ref, ln2_ref, wu_ref, wd_ref, lnf_ref, o_ref, *, ff_tile):
    x1 = x_ref[...] + _dot(mh_ref[...], wo_ref[0:HG_WIDTH, :]) + _dot(mm_ref[...], wo_ref[HG_WIDTH:, :])
    hn = _rms(x1, ln2_ref[...]).astype(BF16)
    mlp = None
    for j in range(D_FF // ff_tile):
        u = jnp.maximum(_dot(hn, wu_ref[:, j * ff_tile:(j + 1) * ff_tile]), 0.0)
        d = _dot((u * u).astype(BF16), wd_ref[j * ff_tile:(j + 1) * ff_tile, :])
        mlp = d if mlp is None else mlp + d
    o_ref[...] = _rms(x1 + mlp, lnf_ref[...])


def _out_mlp(x, mix_h, mix_m, w_out, ln2, w_up, w_down, ln_f, *, tm=512, ff_tile=1024):
    rows = x.shape[0]
    kern = functools.partial(_out_mlp_kernel, ff_tile=ff_tile)
    row_spec = lambda w: pl.BlockSpec((tm, w), lambda i: (i, 0))
    return pl.pallas_call(
        kern,
        grid=(rows // tm,),
        in_specs=[row_spec(D_MODEL), row_spec(HG_WIDTH), row_spec(M_WIDTH),
                  _const_spec(w_out.shape), _const_spec(ln2.shape), _const_spec(w_up.shape),
                  _const_spec(w_down.shape), _const_spec(ln_f.shape)],
        out_specs=row_spec(D_MODEL),
        out_shape=jax.ShapeDtypeStruct((rows, D_MODEL), F32),
        compiler_params=pltpu.CompilerParams(
            dimension_semantics=("arbitrary",), vmem_limit_bytes=VMEM_LIMIT),
        name="out_mlp",
    )(x, mix_h, mix_m, w_out, ln2, w_up, w_down, ln_f)


def _pad_lanes(v):
    return jnp.pad(v, ((0, 0), (0, LANE - v.shape[1])))


def kernel(x_prompt, x_sample, state_hgrn, state_ssm, state_conv, hg_lb_logits, ln1, w_in, hg_norm, conv_w,
           conv_b, dt_bias, a_log, d_skip, m_norm, w_out, ln2, w_up, w_down, ln_f):
    l = 0
    bp, seq, _ = x_prompt.shape
    bs, dseq, _ = x_sample.shape
    assert DEPTH == 1 and dseq == SEG

    w = w_in[l]
    c_hg = HG_HEADS * HG_DK * 2 + 2 * HG_WIDTH
    c_z = c_hg + M_WIDTH
    c_xbc = c_z + M_CONV_DIM
    w_all = jnp.concatenate(
        [w[:, :c_hg], w[:, c_z:c_xbc], w[:, c_xbc:],
         jnp.zeros((D_MODEL, OFF_Z - OFF_DT - M_HEADS), w.dtype), w[:, c_hg:c_z]], axis=1).astype(BF16)
    lbl = hg_lb_logits.astype(F32)
    ln1_r = ln1[l][None, :]
    hgn_r = hg_norm[l].reshape(1, HG_WIDTH)
    cw = conv_w[l]
    cb_r = conv_b[l][None, :]
    dtb_r = _pad_lanes(dt_bias[l][None, :])
    alog_r = _pad_lanes(a_log[l][None, :])
    dsk_r = jnp.repeat(d_skip[l], M_HEADDIM)[None, :]
    mn_r = m_norm[l][None, :]
    mlp_w = (w_out[l].astype(BF16), ln2[l][None, :], w_up[l].astype(BF16), w_down[l].astype(BF16), ln_f[None, :])

    mixh_p, mixm_p, hgs_p, ssm_p, conv_p = _prompt_mixer(
        x_prompt, lbl, ln1_r, w_all, hgn_r, cw, cb_r, dtb_r, alog_r, dsk_r, mn_r)
    y_p = _out_mlp(x_prompt.reshape(bp * seq, D_MODEL), mixh_p.reshape(bp * seq, HG_WIDTH),
                   mixm_p.reshape(bp * seq, M_WIDTH), *mlp_w)

    xs2 = x_sample.reshape(bs * SEG, D_MODEL)
    proj_s = _sample_in_proj(xs2, ln1_r, w_all)
    mixh_s, hgs_s = _sample_hgrn(proj_s, state_hgrn[l], lbl, hgn_r)
    hist = jnp.pad(state_conv[l], ((0, 0), (SEG - (M_CONV - 1), 0), (0, 0))).reshape(bs * SEG, M_CONV_DIM)
    mixm_s, ssm_s = _sample_ssd(proj_s, hist, state_ssm[l].reshape(bs, M_WIDTH, M_DSTATE),
                                cw, cb_r, dtb_r, alog_r, dsk_r, mn_r)
    y_s = _out_mlp(xs2, mixh_s, mixm_s, *mlp_w)
    conv_s = proj_s.reshape(bs, SEG, W_IN_COLS)[:, SEG - (M_CONV - 1):, OFF_XBC:OFF_DT]

    return (y_p.reshape(bp, seq, D_MODEL), y_s.reshape(bs, SEG, D_MODEL),
            hgs_p[None], hgs_s[None],
            ssm_p.reshape(1, bp, M_HEADS, M_HEADDIM, M_DSTATE), ssm_s.reshape(1, bs, M_HEADS, M_HEADDIM, M_DSTATE),
            conv_p[None], conv_s[None])
```

```python
import functools

import jax
import jax.numpy as jnp
from jax import lax
from jax.experimental import pallas as pl
from jax.experimental.pallas import tpu as pltpu

F32 = jnp.float32
BF16 = jnp.bfloat16

D_MODEL = 1024
DEPTH = 1
HG_HEADS = 8
HG_DK = 128
HG_DV = 128
HG_WIDTH = HG_HEADS * HG_DV
M_WIDTH = 1024
M_HEADDIM = 64
M_HEADS = M_WIDTH // M_HEADDIM
M_DSTATE = 128
M_GROUPS = 2
M_GROUP_WIDTH = M_WIDTH // M_GROUPS
M_CONV = 4
M_CONV_DIM = M_WIDTH + 2 * M_GROUPS * M_DSTATE
D_FF = 4 * D_MODEL
NORM_EPS = 1e-5

LANE = 128
SUBLANE = 8

OFF_Q, OFF_F, OFF_I, OFF_G = 0, 1024, 2048, 3072
HG_PROJ_WIDTH = 4096
OFF_XBC = HG_PROJ_WIDTH
OFF_DT = OFF_XBC + M_CONV_DIM
XD_WIDTH = 2048
OFF_Z = OFF_XBC + XD_WIDTH
W_IN_COLS = OFF_Z + M_WIDTH

VMEM_LIMIT = 56 * 1024 * 1024

NEG_BIG = -1e30


def _dot(a, b):
    return jnp.dot(a, b, preferred_element_type=F32)


def _dot_nt(a, b):
    return lax.dot_general(a, b, (((1,), (1,)), ((), ())), preferred_element_type=F32)


def _dot_tn(a, b):
    return lax.dot_general(a, b, (((0,), (0,)), ((), ())), preferred_element_type=F32)


def _sigmoid(x):
    return 1.0 / (1.0 + jnp.exp(-x))


def _silu(x):
    return x * _sigmoid(x)


def _softplus(x):
    return jnp.maximum(x, 0.0) + jnp.log(1.0 + jnp.exp(-jnp.abs(x)))


def _rms(x, gain):
    ms = jnp.mean(x * x, axis=-1, keepdims=True)
    return x * lax.rsqrt(ms + NORM_EPS) * gain


def _lower_bound(lbl, layer):
    rows = [lbl[i:i + 1, :] for i in range(DEPTH + 1)]
    m = functools.reduce(jnp.maximum, rows)
    es = [jnp.exp(r - m) for r in rows]
    return sum(es[:layer + 1]) / sum(es)


def _cumsum_rows(x, seg):
    t = lax.broadcasted_iota(jnp.int32, x.shape, 0) & (seg - 1)
    s = 1
    while s < seg:
        x = x + jnp.where(t >= s, pltpu.roll(x, s, 0), 0.0)
        s *= 2
    return x


def _split3(x):
    h = x.astype(BF16)
    r = x - h.astype(F32)
    m = r.astype(BF16)
    l = (r - m.astype(F32)).astype(BF16)
    return h, m, l


def _lane_bcast(x, lane):
    return jnp.broadcast_to(x[:, lane:lane + 1], x.shape)


N_PROJ_BUFS = 5


def _interleave(*pieces):
    live = list(pieces)
    while live:
        for g in list(live):
            if next(g, StopIteration) is StopIteration:
                live.remove(g)


def _in_proj_tile(x, ln1_ref, w_refs, bufs, tb):
    whg_ref, wxd_ref, wz_ref = w_refs
    h_scr, hg_scr, z_scr, xbc_scr, dt_scr = bufs
    h_scr[...] = _rms(x, ln1_ref[...]).astype(BF16)
    for n0 in range(0, HG_PROJ_WIDTH, 1024):
        hg_scr[:, n0:n0 + 1024] = _dot(h_scr[...], whg_ref[:, n0:n0 + 1024])
        yield
    xbc_scr[SUBLANE:SUBLANE + tb, :] = _dot(h_scr[...], wxd_ref[:, 0:M_CONV_DIM])
    yield
    dt_scr[...] = _dot(h_scr[...], wxd_ref[:, M_CONV_DIM:M_CONV_DIM + LANE])
    z_scr[...] = _dot(h_scr[...], wz_ref[...])
    yield


def _recur_tile(bufs, prm, mixh_ref, mixm_ref, r0, work, state, tb, ch, cs):
    _, hg_scr, z_scr, xbc_scr, dt_scr = bufs
    lbl_ref, hgn_ref, cw_ref, cb_ref, dtb_ref, alog_ref, dsk_ref, mn_ref = prm
    xs_scr, bc_scr, y_scr = work
    sth_scr, stm_scr, hist_scr = state

    xbc_scr[0:SUBLANE, :] = hist_scr[...]

    lb = _lower_bound(lbl_ref[...], 0)
    tril_h = (lax.broadcasted_iota(jnp.int32, (ch, ch), 0) >= lax.broadcasted_iota(jnp.int32, (ch, ch), 1))
    for j in range(tb // ch):
        r = slice(j * ch, (j + 1) * ch)
        ro = slice(r0 + j * ch, r0 + (j + 1) * ch)
        for hd in range(HG_HEADS):
            cl = slice(hd * LANE, (hd + 1) * LANE)
            q = hg_scr[r, OFF_Q + hd * LANE:OFF_Q + (hd + 1) * LANE]
            fr = hg_scr[r, OFF_F + hd * LANE:OFF_F + (hd + 1) * LANE]
            v = hg_scr[r, OFF_I + hd * LANE:OFF_I + (hd + 1) * LANE]
            g = hg_scr[r, OFF_G + hd * LANE:OFF_G + (hd + 1) * LANE]
            lbh = lb[:, cl]
            f = lbh + (1.0 - lbh) * _sigmoid(fr)
            a = jnp.log(f)
            k = 1.0 - f
            cum = _cumsum_rows(a, ch)
            mid = cum[ch // 2 - 1:ch // 2, :]
            last = cum[ch - 1:ch, :]
            qd_mid = q * jnp.exp(cum - mid)
            kd_mid = k * jnp.exp(mid - cum)
            qd = (qd_mid * jnp.exp(mid)).astype(BF16)
            k_end = (kd_mid * jnp.exp(last - mid)).astype(BF16)
            dec = jnp.exp(last)
            vb = v.astype(BF16)
            att = _dot_nt(qd_mid.astype(BF16), kd_mid.astype(BF16))
            att = jnp.where(tril_h, att, 0.0).astype(BF16)
            st = sth_scr[hd]
            o = _dot(att, vb) + _dot_nt(qd, st.astype(BF16))
            sth_scr[hd] = dec * st + _dot_tn(vb, k_end)
            on = _rms(o, hgn_ref[:, cl]) * _silu(g)
            mixh_ref[0, ro, cl] = on.astype(BF16)
        yield

    for cbk in range(M_CONV_DIM // LANE):
        cl = slice(cbk * LANE, (cbk + 1) * LANE)
        acc = cb_ref[:, cl] + cw_ref[M_CONV - 1:M_CONV, cl] * xbc_scr[SUBLANE:SUBLANE + tb, cl]
        for d in range(1, M_CONV):
            acc = acc + cw_ref[M_CONV - 1 - d:M_CONV - d, cl] * xbc_scr[SUBLANE - d:SUBLANE - d + tb, cl]
        act = _silu(acc)
        if cbk < M_WIDTH // LANE:
            xs_scr[:, cl] = act
        else:
            bc_scr[:, (cbk * LANE - M_WIDTH):(cbk * LANE - M_WIDTH) + LANE] = act.astype(BF16)
    hist_scr[...] = xbc_scr[tb:tb + SUBLANE, :]
    yield

    a_row = -jnp.exp(alog_ref[...])
    tril_m = (lax.broadcasted_iota(jnp.int32, (cs, cs), 0) >= lax.broadcasted_iota(jnp.int32, (cs, cs), 1))
    lane_lo = lax.broadcasted_iota(jnp.int32, (cs, LANE), 1) < M_HEADDIM
    for s in range(tb // cs):
        r = slice(s * cs, (s + 1) * cs)
        ro = slice(r0 + s * cs, r0 + (s + 1) * cs)
        dtv = _softplus(dt_scr[r, :] + dtb_ref[...])
        cum = _cumsum_rows(dtv * a_row, cs)
        cum_t = cum.T
        cbs = {}
        for p in range(M_HEADS // 2):
            grp = p // (M_HEADS // 2 // M_GROUPS)
            cl = slice(p * LANE, (p + 1) * LANE)
            bm = bc_scr[r, grp * M_DSTATE:(grp + 1) * M_DSTATE]
            cm = bc_scr[r, (M_GROUPS + grp) * M_DSTATE:(M_GROUPS + grp + 1) * M_DSTATE]
            if grp not in cbs:
                cbs[grp] = _dot_nt(cm, bm)
            cb = cbs[grp]
            cum_h = [_lane_bcast(cum, hh) for hh in (2 * p, 2 * p + 1)]
            cum_x = jnp.where(lane_lo, cum_h[0], cum_h[1])
            dt_x = jnp.where(lane_lo, _lane_bcast(dtv, 2 * p), _lane_bcast(dtv, 2 * p + 1))
            last_x = cum_x[cs - 1:cs, :]
            xs = xs_scr[r, cl]
            xdt = xs * dt_x
            ms = []
            for i, hh in enumerate((2 * p, 2 * p + 1)):
                seg = cum_h[i] - cum_t[hh:hh + 1, :]
                ms.append((cb * jnp.exp(jnp.where(tril_m, seg, NEG_BIG))).astype(BF16))
            x_lo = jnp.where(lane_lo, xdt, 0.0).astype(BF16)
            x_hi = jnp.where(lane_lo, 0.0, xdt).astype(BF16)
            y = _dot(jnp.concatenate(ms, axis=1), jnp.concatenate([x_lo, x_hi], axis=0))
            st = stm_scr[:, cl]
            y = y + _dot(cm, st.astype(BF16)) * jnp.exp(cum_x)
            y = y + dsk_ref[:, cl] * xs
            to_end = (xdt * jnp.exp(last_x - cum_x)).astype(BF16)
            stm_scr[:, cl] = jnp.exp(last_x) * st + _dot_tn(bm, to_end)
            y_scr[r, cl] = y * _silu(z_scr[r, cl])
        for grp in range(M_GROUPS):
            gl = slice(grp * M_GROUP_WIDTH, (grp + 1) * M_GROUP_WIDTH)
            mixm_ref[0, ro, gl] = _rms(y_scr[r, gl], mn_ref[:, gl]).astype(BF16)
        yield


def _prompt_mixer_kernel(x_ref, lbl_ref, ln1_ref, whg_ref, wxd_ref, wz_ref, hgn_ref, cw_ref, cb_ref, dtb_ref,
                         alog_ref, dsk_ref, mn_ref,
                         mixh_ref, mixm_ref, hgs_ref, ssm_ref, conv_ref,
                         *scratch, tb, n_sub, ch, cs):
    sets = [scratch[i * N_PROJ_BUFS:(i + 1) * N_PROJ_BUFS] for i in range(n_sub)]
    xs_scr, bc_scr, y_scr, sth_scr, stm_scr, hist_scr = scratch[n_sub * N_PROJ_BUFS:]
    c = pl.program_id(1)
    last_c = pl.num_programs(1) - 1

    @pl.when(c == 0)
    def _():
        sth_scr[...] = jnp.zeros(sth_scr.shape, F32)
        stm_scr[...] = jnp.zeros(stm_scr.shape, F32)
        hist_scr[...] = jnp.zeros(hist_scr.shape, F32)

    prm = (lbl_ref, hgn_ref, cw_ref, cb_ref, dtb_ref, alog_ref, dsk_ref, mn_ref)

    def proj(s):
        return _in_proj_tile(x_ref[0, s * tb:(s + 1) * tb, :], ln1_ref, (whg_ref, wxd_ref, wz_ref), sets[s], tb)

    def recur(s):
        return _recur_tile(sets[s], prm, mixh_ref, mixm_ref, s * tb, (xs_scr, bc_scr, y_scr),
                           (sth_scr, stm_scr, hist_scr), tb, ch, cs)

    _interleave(proj(0))
    for s in range(1, n_sub):
        _interleave(recur(s - 1), proj(s))
    _interleave(recur(n_sub - 1))

    @pl.when(c == last_c)
    def _():
        for hd in range(HG_HEADS):
            hgs_ref[0, hd] = sth_scr[hd].T
        for p in range(M_WIDTH // LANE):
            ssm_ref[0, p * LANE:(p + 1) * LANE, :] = stm_scr[:, p * LANE:(p + 1) * LANE].T
        conv_ref[0] = hist_scr[SUBLANE - (M_CONV - 1):SUBLANE, :]


def _const_spec(shape):
    nd = len(shape)
    return pl.BlockSpec(shape, lambda *_: (0,) * nd, pipeline_mode=pl.Buffered(1))


def _prompt_mixer(x, lbl, ln1, w_hg, w_xd, w_z, hgn, cw, cb, dtb, alog, dsk, mn, *, tb=256, n_sub=1, ch=64, cs=128):
    bsz, seq, _ = x.shape
    ts = tb * n_sub
    nc = seq // ts
    kern = functools.partial(_prompt_mixer_kernel, tb=tb, n_sub=n_sub, ch=ch, cs=cs)
    params = (lbl, ln1, w_hg, w_xd, w_z, hgn, cw, cb, dtb, alog, dsk, mn)
    proj_bufs = [
        pltpu.VMEM((tb, D_MODEL), BF16),
        pltpu.VMEM((tb, OFF_XBC), F32),
        pltpu.VMEM((tb, M_WIDTH), F32),
        pltpu.VMEM((tb + SUBLANE, M_CONV_DIM), F32),
        pltpu.VMEM((tb, LANE), F32),
    ]
    assert len(proj_bufs) == N_PROJ_BUFS
    return pl.pallas_call(
        kern,
        grid=(bsz, nc),
        in_specs=[pl.BlockSpec((1, ts, D_MODEL), lambda b, c: (b, c, 0))] + [_const_spec(p.shape) for p in params],
        out_specs=[
            pl.BlockSpec((1, ts, HG_WIDTH), lambda b, c: (b, c, 0)),
            pl.BlockSpec((1, ts, M_WIDTH), lambda b, c: (b, c, 0)),
            pl.BlockSpec((1, HG_HEADS, HG_DK, HG_DV), lambda b, c: (b, 0, 0, 0)),
            pl.BlockSpec((1, M_WIDTH, M_DSTATE), lambda b, c: (b, 0, 0)),
            pl.BlockSpec((1, M_CONV - 1, M_CONV_DIM), lambda b, c: (b, 0, 0)),
        ],
        out_shape=[
            jax.ShapeDtypeStruct((bsz, seq, HG_WIDTH), BF16),
            jax.ShapeDtypeStruct((bsz, seq, M_WIDTH), BF16),
            jax.ShapeDtypeStruct((bsz, HG_HEADS, HG_DK, HG_DV), F32),
            jax.ShapeDtypeStruct((bsz, M_WIDTH, M_DSTATE), F32),
            jax.ShapeDtypeStruct((bsz, M_CONV - 1, M_CONV_DIM), F32),
        ],
        scratch_shapes=proj_bufs * n_sub + [
            pltpu.VMEM((tb, M_WIDTH), F32),
            pltpu.VMEM((tb, 2 * M_GROUPS * M_DSTATE), BF16),
            pltpu.VMEM((tb, M_WIDTH), F32),
            pltpu.VMEM((HG_HEADS, HG_DV, HG_DK), F32),
            pltpu.VMEM((M_DSTATE, M_WIDTH), F32),
            pltpu.VMEM((SUBLANE, M_CONV_DIM), F32),
        ],
        compiler_params=pltpu.CompilerParams(
            dimension_semantics=("arbitrary", "arbitrary"), vmem_limit_bytes=VMEM_LIMIT),
        name="prompt_mixer",
    )(x, *params)


def _sample_in_proj_kernel(x_ref, ln1_ref, whg_ref, wxd_ref, wz_ref, o_ref, h_scr, *, n_hg, n_xd):
    n = pl.program_id(0)

    @pl.when(n == 0)
    def _():
        h_scr[...] = _rms(x_ref[...], ln1_ref[...]).astype(BF16)

    @pl.when(n < n_hg)
    def _():
        o_ref[...] = _dot(h_scr[...], whg_ref[...])

    @pl.when((n >= n_hg) & (n < n_hg + n_xd))
    def _():
        o_ref[...] = _dot(h_scr[...], wxd_ref[...])

    @pl.when(n >= n_hg + n_xd)
    def _():
        o_ref[...] = _dot(h_scr[...], wz_ref[...])


def _sample_in_proj(x, ln1, w_hg, w_xd, w_z, *, tn=1024):
    rows = x.shape[0]
    n_hg, n_xd, n_z = w_hg.shape[1] // tn, w_xd.shape[1] // tn, w_z.shape[1] // tn
    kern = functools.partial(_sample_in_proj_kernel, n_hg=n_hg, n_xd=n_xd)
    return pl.pallas_call(
        kern,
        grid=(n_hg + n_xd + n_z,),
        in_specs=[_const_spec((rows, D_MODEL)), _const_spec(ln1.shape),
                  pl.BlockSpec((D_MODEL, tn), lambda n: (0, jnp.minimum(n, n_hg - 1))),
                  pl.BlockSpec((D_MODEL, tn), lambda n: (0, jnp.clip(n - n_hg, 0, n_xd - 1))),
                  pl.BlockSpec((D_MODEL, tn), lambda n: (0, jnp.clip(n - n_hg - n_xd, 0, n_z - 1)))],
        out_specs=pl.BlockSpec((rows, tn), lambda n: (0, n)),
        out_shape=jax.ShapeDtypeStruct((rows, W_IN_COLS), F32),
        scratch_shapes=[pltpu.VMEM((rows, D_MODEL), BF16)],
        compiler_params=pltpu.CompilerParams(
            dimension_semantics=("arbitrary",), vmem_limit_bytes=VMEM_LIMIT),
        name="sample_in_proj",
    )(x, ln1, w_hg, w_xd, w_z)


SEG = 8


def _block_causal_mask(rows):
    ri = lax.broadcasted_iota(jnp.int32, (rows, rows), 0)
    ci = lax.broadcasted_iota(jnp.int32, (rows, rows), 1)
    return ((ri & -SEG) == (ci & -SEG)) & (ci <= ri)


def _seg_last(x):
    rows, width = x.shape
    x3 = x.reshape(rows // SEG, SEG, width)
    return jnp.broadcast_to(x3[:, SEG - 1:SEG, :], x3.shape).reshape(rows, width)


def _decay_tail(dec8):
    h, m, l = _split3(dec8)
    t = lax.broadcasted_iota(jnp.int32, dec8.shape, 0)
    return jnp.where(t == 0, h.astype(F32), jnp.where(t == 1, m.astype(F32), jnp.where(t == 2, l.astype(F32), 0.0)))


def _ones_tail():
    t = lax.broadcasted_iota(jnp.int32, (2 * SEG, LANE), 0)
    return jnp.where((t >= SEG) & (t < SEG + 3), 1.0, 0.0).astype(BF16)


def _sample_hgrn_kernel(p_ref, s0_ref, lbl_ref, hgn_ref, mixh_ref, s8_ref,
                        qd_scr, ke_scr, dec_scr, o_scr, *, nb):
    rows = nb * SEG
    lb = _lower_bound(lbl_ref[...], 0)
    mask = _block_causal_mask(rows)
    for hd in range(HG_HEADS):
        cl = slice(hd * LANE, (hd + 1) * LANE)
        q = p_ref[:, OFF_Q + hd * LANE:OFF_Q + (hd + 1) * LANE]
        fr = p_ref[:, OFF_F + hd * LANE:OFF_F + (hd + 1) * LANE]
        v = p_ref[:, OFF_I + hd * LANE:OFF_I + (hd + 1) * LANE]
        lbh = lb[:, cl]
        f = lbh + (1.0 - lbh) * _sigmoid(fr)
        a = jnp.log(f)
        k = 1.0 - f
        cum = _cumsum_rows(a, SEG)
        last = _seg_last(cum)
        qd = (q * jnp.exp(cum)).astype(BF16)
        kd = (k * jnp.exp(-cum)).astype(BF16)
        att = jnp.where(mask, _dot_nt(qd, kd), 0.0).astype(BF16)
        o_scr[:, cl] = _dot(att, v.astype(BF16))
        qd_scr[:, cl] = qd
        ke_scr[:, cl] = k * jnp.exp(last - cum)
        dec_scr[:, cl] = jnp.exp(last)

    ones_tail = _ones_tail()
    zeros8 = jnp.zeros((SEG, LANE), F32)

    def pair_body(m, carry):
        r16 = pl.ds(pl.multiple_of(m * 2 * SEG, 2 * SEG), 2 * SEG)
        for par in range(2):
            i = 2 * m + par
            r8 = pl.ds(pl.multiple_of(i * SEG, SEG), SEG)
            for hd in range(HG_HEADS):
                cl = slice(hd * LANE, (hd + 1) * LANE)
                s0 = s0_ref[i, hd]
                oi = _dot(qd_scr[r16, cl], s0.astype(BF16))
                o_scr[r8, cl] = o_scr[r8, cl] + oi[par * SEG:(par + 1) * SEG, :]
                aug = jnp.concatenate([ke_scr[r8, cl], _decay_tail(dec_scr[r8, cl])], axis=0).astype(BF16)
                v8 = p_ref[r8, OFF_I + hd * LANE:OFF_I + (hd + 1) * LANE]
                rhs = jnp.concatenate([jnp.concatenate([v8, zeros8], axis=0).astype(BF16), ones_tail], axis=1)
                ud = _dot_tn(aug, rhs)
                s8_ref[i, hd] = ud[:, LANE:] * s0 + ud[:, :LANE]
        return carry

    lax.fori_loop(0, nb // 2, pair_body, 0)

    for hd in range(HG_HEADS):
        cl = slice(hd * LANE, (hd + 1) * LANE)
        g = p_ref[:, OFF_G + hd * LANE:OFF_G + (hd + 1) * LANE]
        mixh_ref[:, cl] = (_rms(o_scr[:, cl], hgn_ref[:, cl]) * _silu(g)).astype(BF16)


def _sample_hgrn(proj, s0, lbl, hgn, *, nb=16):
    nseq = s0.shape[0]
    rows = nb * SEG
    kern = functools.partial(_sample_hgrn_kernel, nb=nb)
    return pl.pallas_call(
        kern,
        grid=(nseq // nb,),
        in_specs=[pl.BlockSpec((rows, OFF_XBC), lambda i: (i, 0)),
                  pl.BlockSpec((nb, HG_HEADS, HG_DK, HG_DV), lambda i: (i, 0, 0, 0)),
                  _const_spec(lbl.shape), _const_spec(hgn.shape)],
        out_specs=[pl.BlockSpec((rows, HG_WIDTH), lambda i: (i, 0)),
                   pl.BlockSpec((nb, HG_HEADS, HG_DK, HG_DV), lambda i: (i, 0, 0, 0))],
        out_shape=[jax.ShapeDtypeStruct((nseq * SEG, HG_WIDTH), BF16),
                   jax.ShapeDtypeStruct(s0.shape, F32)],
        scratch_shapes=[pltpu.VMEM((rows, HG_WIDTH), BF16),
                        pltpu.VMEM((rows, HG_WIDTH), F32),
                        pltpu.VMEM((rows, HG_WIDTH), F32),
                        pltpu.VMEM((rows, HG_WIDTH), F32)],
        compiler_params=pltpu.CompilerParams(
            dimension_semantics=("arbitrary",), vmem_limit_bytes=VMEM_LIMIT),
        name="sample_hgrn",
    )(proj, s0, lbl, hgn)


def _sample_ssd_kernel(xd_ref, z_ref, hist_ref, s0_ref, cw_ref, cb_ref, dtb_ref, alog_ref, dsk_ref, mn_ref,
                       mixm_ref, s8_ref,
                       xs_scr, bm_scr, cm_scr, u_scr, cd_scr, fs_scr, y_scr, *, nb):
    rows = nb * SEG
    t_in_seq = lax.broadcasted_iota(jnp.int32, (rows, LANE), 0) & (SEG - 1)

    for cbk in range(M_CONV_DIM // LANE):
        cl = slice(cbk * LANE, (cbk + 1) * LANE)
        x = xd_ref[:, cl]
        hist = hist_ref[:, cl]
        acc = cb_ref[:, cl] + cw_ref[M_CONV - 1:M_CONV, cl] * x
        for d in range(1, M_CONV):
            shifted = jnp.where(t_in_seq >= d, pltpu.roll(x, d, 0), pltpu.roll(hist, rows - SEG + d, 0))
            acc = acc + cw_ref[M_CONV - 1 - d:M_CONV - d, cl] * shifted
        act = _silu(acc)
        if cbk < M_WIDTH // LANE:
            xs_scr[:, cl] = act
        elif cbk < (M_WIDTH + M_GROUPS * M_DSTATE) // LANE:
            bm_scr[:, cbk * LANE - M_WIDTH:(cbk + 1) * LANE - M_WIDTH] = act
        else:
            o0 = cbk * LANE - M_WIDTH - M_GROUPS * M_DSTATE
            cm_scr[:, o0:o0 + LANE] = act.astype(BF16)

    a_row = -jnp.exp(alog_ref[...])
    dtv = _softplus(xd_ref[:, M_CONV_DIM:M_CONV_DIM + LANE] + dtb_ref[...])
    cum = _cumsum_rows(dtv * a_row, SEG)
    cum_t = cum.T
    mask = _block_causal_mask(rows)
    lane_lo = lax.broadcasted_iota(jnp.int32, (rows, LANE), 1) < M_HEADDIM
    cbs = {}
    for p in range(M_HEADS // 2):
        grp = p // (M_HEADS // 2 // M_GROUPS)
        cl = slice(p * LANE, (p + 1) * LANE)
        if grp not in cbs:
            bm = bm_scr[:, grp * M_DSTATE:(grp + 1) * M_DSTATE].astype(BF16)
            cbs[grp] = _dot_nt(cm_scr[:, grp * M_DSTATE:(grp + 1) * M_DSTATE], bm)
        cb = cbs[grp]
        cum_h = [_lane_bcast(cum, hh) for hh in (2 * p, 2 * p + 1)]
        cum_x = jnp.where(lane_lo, cum_h[0], cum_h[1])
        dt_x = jnp.where(lane_lo, _lane_bcast(dtv, 2 * p), _lane_bcast(dtv, 2 * p + 1))
        last_x = _seg_last(cum_x)
        xs = xs_scr[:, cl]
        xdt = xs * dt_x
        ms = []
        for i, hh in enumerate((2 * p, 2 * p + 1)):
            seg = cum_h[i] - cum_t[hh:hh + 1, :]
            ms.append((cb * jnp.exp(jnp.where(mask, seg, NEG_BIG))).astype(BF16))
        x_lo = jnp.where(lane_lo, xdt, 0.0).astype(BF16)
        x_hi = jnp.where(lane_lo, 0.0, xdt).astype(BF16)
        y = _dot(jnp.concatenate(ms, axis=1), jnp.concatenate([x_lo, x_hi], axis=0))
        y_scr[:, cl] = y + dsk_ref[:, cl] * xs
        u_scr[:, cl] = xdt * jnp.exp(last_x - cum_x)
        cd_scr[:, cl] = jnp.exp(last_x)
        fs_scr[:, cl] = jnp.exp(cum_x)

    ones_tail = _ones_tail()
    zeros8 = jnp.zeros((SEG, M_DSTATE), F32)

    def pair_body(m, carry):
        r16 = pl.ds(pl.multiple_of(m * 2 * SEG, 2 * SEG), 2 * SEG)
        for par in range(2):
            i = 2 * m + par
            r8 = pl.ds(pl.multiple_of(i * SEG, SEG), SEG)
            for grp in range(M_GROUPS):
                gl = slice(grp * M_GROUP_WIDTH, (grp + 1) * M_GROUP_WIDTH)
                nl = slice(grp * M_DSTATE, (grp + 1) * M_DSTATE)
                s0 = s0_ref[i, gl, :]
                yi = _dot_nt(cm_scr[r16, nl], s0.astype(BF16))
                y_scr[r8, gl] = y_scr[r8, gl] + yi[par * SEG:(par + 1) * SEG, :] * fs_scr[r8, gl]
                aug = jnp.concatenate([u_scr[r8, gl], _decay_tail(cd_scr[r8, gl])], axis=0).astype(BF16)
                rhs = jnp.concatenate(
                    [jnp.concatenate([bm_scr[r8, nl], zeros8], axis=0).astype(BF16), ones_tail], axis=1)
                ud = _dot_tn(aug, rhs)
                s8_ref[i, gl, :] = ud[:, M_DSTATE:] * s0 + ud[:, :M_DSTATE]
        return carry

    lax.fori_loop(0, nb // 2, pair_body, 0)

    for grp in range(M_GROUPS):
        gl = slice(grp * M_GROUP_WIDTH, (grp + 1) * M_GROUP_WIDTH)
        y = y_scr[:, gl] * _silu(z_ref[:, gl])
        mixm_ref[:, gl] = _rms(y, mn_ref[:, gl]).astype(BF16)


def _sample_ssd(proj, hist, s0, cw, cb, dtb, alog, dsk, mn, *, nb=16):
    nseq = s0.shape[0]
    rows = nb * SEG
    kern = functools.partial(_sample_ssd_kernel, nb=nb)
    params = (cw, cb, dtb, alog, dsk, mn)
    return pl.pallas_call(
        kern,
        grid=(nseq // nb,),
        in_specs=[pl.BlockSpec((rows, XD_WIDTH), lambda i: (i, OFF_XBC // XD_WIDTH)),
                  pl.BlockSpec((rows, M_WIDTH), lambda i: (i, OFF_Z // M_WIDTH)),
                  pl.BlockSpec((rows, M_CONV_DIM), lambda i: (i, 0)),
                  pl.BlockSpec((nb, M_WIDTH, M_DSTATE), lambda i: (i, 0, 0))]
                 + [_const_spec(p.shape) for p in params],
        out_specs=[pl.BlockSpec((rows, M_WIDTH), lambda i: (i, 0)),
                   pl.BlockSpec((nb, M_WIDTH, M_DSTATE), lambda i: (i, 0, 0))],
        out_shape=[jax.ShapeDtypeStruct((nseq * SEG, M_WIDTH), BF16),
                   jax.ShapeDtypeStruct(s0.shape, F32)],
        scratch_shapes=[pltpu.VMEM((rows, M_WIDTH), F32),
                        pltpu.VMEM((rows, M_GROUPS * M_DSTATE), F32),
                        pltpu.VMEM((rows, M_GROUPS * M_DSTATE), BF16),
                        pltpu.VMEM((rows, M_WIDTH), F32),
                        pltpu.VMEM((rows, M_WIDTH), F32),
                        pltpu.VMEM((rows, M_WIDTH), F32),
                        pltpu.VMEM((rows, M_WIDTH), F32)],
        compiler_params=pltpu.CompilerParams(
            dimension_semantics=("arbitrary",), vmem_limit_bytes=VMEM_LIMIT),
        name="sample_ssd",
    )(proj, proj, hist, s0, *params)


def _out_mlp_kernel(x_ref, mh_ref, mm_ref, wo_ref, ln2_ref, wu_ref, wd_ref, lnf_ref, o_ref, *, ff_tile):
    x1 = x_ref[...] + _dot(mh_ref[...], wo_ref[0:HG_WIDTH, :]) + _dot(mm_ref[...], wo_ref[HG_WIDTH:, :])
    hn = _rms(x1, ln2_ref[...]).astype(BF16)
    mlp = None
    for j in range(D_FF // ff_tile):
        u = jnp.maximum(_dot(hn, wu_ref[:, j * ff_tile:(j + 1) * ff_tile]), 0.0)
        d = _dot((u * u).astype(BF16), wd_ref[j * ff_tile:(j + 1) * ff_tile, :])
        mlp = d if mlp is None else mlp + d
    o_ref[...] = _rms(x1 + mlp, lnf_ref[...])


def _out_mlp(x, mix_h, mix_m, w_out, ln2, w_up, w_down, ln_f, *, tm=512, ff_tile=1024):
    rows = x.shape[0]
    kern = functools.partial(_out_mlp_kernel, ff_tile=ff_tile)
    row_spec = lambda w: pl.BlockSpec((tm, w), lambda i: (i, 0))
    return pl.pallas_call(
        kern,
        grid=(rows // tm,),
        in_specs=[row_spec(D_MODEL), row_spec(HG_WIDTH), row_spec(M_WIDTH),
                  _const_spec(w_out.shape), _const_spec(ln2.shape), _const_spec(w_up.shape),
                  _const_spec(w_down.shape), _const_spec(ln_f.shape)],
        out_specs=row_spec(D_MODEL),
        out_shape=jax.ShapeDtypeStruct((rows, D_MODEL), F32),
        compiler_params=pltpu.CompilerParams(
            dimension_semantics=("arbitrary",), vmem_limit_bytes=VMEM_LIMIT),
        name="out_mlp",
    )(x, mix_h, mix_m, w_out, ln2, w_up, w_down, ln_f)


def _pad_lanes(v):
    return jnp.pad(v, ((0, 0), (0, LANE - v.shape[1])))


def kernel(x_prompt, x_sample, state_hgrn, state_ssm, state_conv, hg_lb_logits, ln1, w_in, hg_norm, conv_w,
           conv_b, dt_bias, a_log, d_skip, m_norm, w_out, ln2, w_up, w_down, ln_f):
    l = 0
    bp, seq, _ = x_prompt.shape
    bs, dseq, _ = x_sample.shape
    assert DEPTH == 1 and dseq == SEG

    w = w_in[l]
    c_z = HG_PROJ_WIDTH + M_WIDTH
    w_hg = w[:, :HG_PROJ_WIDTH].astype(BF16)
    w_z = w[:, HG_PROJ_WIDTH:c_z].astype(BF16)
    w_xd = jnp.pad(w[:, c_z:].astype(BF16), ((0, 0), (0, XD_WIDTH - (w.shape[1] - c_z))))
    lbl = hg_lb_logits.astype(F32)
    ln1_r = ln1[l][None, :]
    hgn_r = hg_norm[l].reshape(1, HG_WIDTH)
    cw = conv_w[l]
    cb_r = conv_b[l][None, :]
    dtb_r = _pad_lanes(dt_bias[l][None, :])
    alog_r = _pad_lanes(a_log[l][None, :])
    dsk_r = jnp.repeat(d_skip[l], M_HEADDIM)[None, :]
    mn_r = m_norm[l][None, :]
    mlp_w = (w_out[l].astype(BF16), ln2[l][None, :], w_up[l].astype(BF16), w_down[l].astype(BF16), ln_f[None, :])

    mixh_p, mixm_p, hgs_p, ssm_p, conv_p = _prompt_mixer(
        x_prompt, lbl, ln1_r, w_hg, w_xd, w_z, hgn_r, cw, cb_r, dtb_r, alog_r, dsk_r, mn_r)
    y_p = _out_mlp(x_prompt.reshape(bp * seq, D_MODEL), mixh_p.reshape(bp * seq, HG_WIDTH),
                   mixm_p.reshape(bp * seq, M_WIDTH), *mlp_w)

    xs2 = x_sample.reshape(bs * SEG, D_MODEL)
    proj_s = _sample_in_proj(xs2, ln1_r, w_hg, w_xd, w_z)
    mixh_s, hgs_s = _sample_hgrn(proj_s, state_hgrn[l], lbl, hgn_r)
    hist = jnp.pad(state_conv[l], ((0, 0), (SEG - (M_CONV - 1), 0), (0, 0))).reshape(bs * SEG, M_CONV_DIM)
    mixm_s, ssm_s = _sample_ssd(proj_s, hist, state_ssm[l].reshape(bs, M_WIDTH, M_DSTATE),
                                cw, cb_r, dtb_r, alog_r, dsk_r, mn_r)
    y_s = _out_mlp(xs2, mixh_s, mixm_s, *mlp_w)
    conv_s = proj_s.reshape(bs, SEG, W_IN_COLS)[:, SEG - (M_CONV - 1):, OFF_XBC:OFF_DT]

    return (y_p.reshape(bp, seq, D_MODEL), y_s.reshape(bs, SEG, D_MODEL),
            hgs_p[None], hgs_s[None],
            ssm_p.reshape(1, bp, M_HEADS, M_HEADDIM, M_DSTATE), ssm_s.reshape(1, bs, M_HEADS, M_HEADDIM, M_DSTATE),
            conv_p[None], conv_s[None])
```

```python
import functools

import jax
import jax.numpy as jnp
from jax import lax
from jax.experimental import pallas as pl
from jax.experimental.pallas import tpu as pltpu

F32 = jnp.float32
BF16 = jnp.bfloat16

D_MODEL = 1024
DEPTH = 1
HG_HEADS = 8
HG_DK = 128
HG_DV = 128
HG_WIDTH = HG_HEADS * HG_DV
M_WIDTH = 1024
M_HEADDIM = 64
M_HEADS = M_WIDTH // M_HEADDIM
M_DSTATE = 128
M_GROUPS = 2
M_GROUP_WIDTH = M_WIDTH // M_GROUPS
M_CONV = 4
M_CONV_DIM = M_WIDTH + 2 * M_GROUPS * M_DSTATE
D_FF = 4 * D_MODEL
NORM_EPS = 1e-5

LANE = 128
SUBLANE = 8

OFF_Q, OFF_F, OFF_I, OFF_G = 0, 1024, 2048, 3072
HG_PROJ_WIDTH = 4096
OFF_XBC = HG_PROJ_WIDTH
OFF_DT = OFF_XBC + M_CONV_DIM
XD_WIDTH = 2048
OFF_Z = OFF_XBC + XD_WIDTH
W_IN_COLS = OFF_Z + M_WIDTH

VMEM_LIMIT = 56 * 1024 * 1024

NEG_BIG = -1e30
LOG2_E = 1.4426950408889634


def _dot(a, b):
    return jnp.dot(a, b, preferred_element_type=F32)


def _dot_nt(a, b):
    return lax.dot_general(a, b, (((1,), (1,)), ((), ())), preferred_element_type=F32)


def _dot_tn(a, b):
    return lax.dot_general(a, b, (((0,), (0,)), ((), ())), preferred_element_type=F32)


def _sigmoid(x):
    return 1.0 / (1.0 + jnp.exp(-x))


def _silu(x):
    return x * _sigmoid(x)


def _softplus(x):
    return jnp.maximum(x, 0.0) + jnp.log(1.0 + jnp.exp(-jnp.abs(x)))


def _rms(x, gain):
    ms = jnp.mean(x * x, axis=-1, keepdims=True)
    return x * lax.rsqrt(ms + NORM_EPS) * gain


def _lower_bound(lbl, layer):
    rows = [lbl[i:i + 1, :] for i in range(DEPTH + 1)]
    m = functools.reduce(jnp.maximum, rows)
    es = [jnp.exp(r - m) for r in rows]
    return sum(es[:layer + 1]) / sum(es)


def _cumsum_rows(x, seg):
    t = lax.broadcasted_iota(jnp.int32, x.shape, 0) & (seg - 1)
    s = 1
    while s < seg:
        x = x + jnp.where(t >= s, pltpu.roll(x, s, 0), 0.0)
        s *= 2
    return x


def _split3(x):
    h = x.astype(BF16)
    r = x - h.astype(F32)
    m = r.astype(BF16)
    l = (r - m.astype(F32)).astype(BF16)
    return h, m, l


def _lane_bcast(x, lane):
    return jnp.broadcast_to(x[:, lane:lane + 1], x.shape)


def _in_proj(x, ln1_ref, w_refs, bufs, tb):
    whg_ref, wxd_ref, wz_ref = w_refs
    h_scr, hg_scr, z_scr, xbc_scr, dt_scr = bufs
    h_scr[...] = _rms(x, ln1_ref[...]).astype(BF16)
    for n0 in range(0, HG_PROJ_WIDTH, 1024):
        hg_scr[:, n0:n0 + 1024] = _dot(h_scr[...], whg_ref[:, n0:n0 + 1024])
    xbc_scr[SUBLANE:SUBLANE + tb, :] = _dot(h_scr[...], wxd_ref[:, 0:M_CONV_DIM])
    dt_scr[...] = _dot(h_scr[...], wxd_ref[:, M_CONV_DIM:M_CONV_DIM + LANE])
    z_scr[...] = _dot(h_scr[...], wz_ref[...])


def _recurrences(bufs, prm, mixh_ref, mixm_ref, work, state, tb, ch, cs):
    _, hg_scr, z_scr, xbc_scr, dt_scr = bufs
    lbl_ref, hgn_ref, cw_ref, cb_ref, dtb_ref, alog_ref, dsk_ref, mn_ref = prm
    xs_scr, bc_scr, y_scr = work
    sth_scr, stm_scr, hist_scr = state

    xbc_scr[0:SUBLANE, :] = hist_scr[...]

    lb = _lower_bound(lbl_ref[...], 0)
    tril_h = (lax.broadcasted_iota(jnp.int32, (ch, ch), 0) >= lax.broadcasted_iota(jnp.int32, (ch, ch), 1))
    for j in range(tb // ch):
        r = slice(j * ch, (j + 1) * ch)
        for hd in range(HG_HEADS):
            cl = slice(hd * LANE, (hd + 1) * LANE)
            q = hg_scr[r, OFF_Q + hd * LANE:OFF_Q + (hd + 1) * LANE]
            fr = hg_scr[r, OFF_F + hd * LANE:OFF_F + (hd + 1) * LANE]
            v = hg_scr[r, OFF_I + hd * LANE:OFF_I + (hd + 1) * LANE]
            g = hg_scr[r, OFF_G + hd * LANE:OFF_G + (hd + 1) * LANE]
            lbh = lb[:, cl]
            f = lbh + (1.0 - lbh) * _sigmoid(fr)
            a = jnp.log2(f)
            k = 1.0 - f
            cum = _cumsum_rows(a, ch)
            mid = cum[ch // 2 - 1:ch // 2, :]
            last = cum[ch - 1:ch, :]
            qd_mid = q * jnp.exp2(cum - mid)
            kd_mid = k * jnp.exp2(mid - cum)
            qd = (qd_mid * jnp.exp2(mid)).astype(BF16)
            k_end = (kd_mid * jnp.exp2(last - mid)).astype(BF16)
            dec = jnp.exp2(last)
            vb = v.astype(BF16)
            att = _dot_nt(qd_mid.astype(BF16), kd_mid.astype(BF16))
            att = jnp.where(tril_h, att, 0.0).astype(BF16)
            st = sth_scr[hd]
            o = _dot(att, vb) + _dot_nt(qd, st.astype(BF16))
            sth_scr[hd] = dec * st + _dot_tn(vb, k_end)
            on = _rms(o, hgn_ref[:, cl]) * _silu(g)
            mixh_ref[0, r, cl] = on.astype(BF16)

    for cbk in range(M_CONV_DIM // LANE):
        cl = slice(cbk * LANE, (cbk + 1) * LANE)
        acc = cb_ref[:, cl] + cw_ref[M_CONV - 1:M_CONV, cl] * xbc_scr[SUBLANE:SUBLANE + tb, cl]
        for d in range(1, M_CONV):
            acc = acc + cw_ref[M_CONV - 1 - d:M_CONV - d, cl] * xbc_scr[SUBLANE - d:SUBLANE - d + tb, cl]
        act = _silu(acc)
        if cbk < M_WIDTH // LANE:
            xs_scr[:, cl] = act
        else:
            bc_scr[:, (cbk * LANE - M_WIDTH):(cbk * LANE - M_WIDTH) + LANE] = act.astype(BF16)
    hist_scr[...] = xbc_scr[tb:tb + SUBLANE, :]

    a_row = -jnp.exp(alog_ref[...]) * LOG2_E
    tril_m = (lax.broadcasted_iota(jnp.int32, (cs, cs), 0) >= lax.broadcasted_iota(jnp.int32, (cs, cs), 1))
    lane_lo = lax.broadcasted_iota(jnp.int32, (cs, LANE), 1) < M_HEADDIM
    for s in range(tb // cs):
        r = slice(s * cs, (s + 1) * cs)
        dtv = _softplus(dt_scr[r, :] + dtb_ref[...])
        cum = _cumsum_rows(dtv * a_row, cs)
        cum_t = cum.T
        cbs = {}
        for p in range(M_HEADS // 2):
            grp = p // (M_HEADS // 2 // M_GROUPS)
            cl = slice(p * LANE, (p + 1) * LANE)
            bm = bc_scr[r, grp * M_DSTATE:(grp + 1) * M_DSTATE]
            cm = bc_scr[r, (M_GROUPS + grp) * M_DSTATE:(M_GROUPS + grp + 1) * M_DSTATE]
            if grp not in cbs:
                cbs[grp] = _dot_nt(cm, bm)
            cb = cbs[grp]
            cum_h = [_lane_bcast(cum, hh) for hh in (2 * p, 2 * p + 1)]
            cum_x = jnp.where(lane_lo, cum_h[0], cum_h[1])
            dt_x = jnp.where(lane_lo, _lane_bcast(dtv, 2 * p), _lane_bcast(dtv, 2 * p + 1))
            last_x = cum_x[cs - 1:cs, :]
            xs = xs_scr[r, cl]
            xdt = xs * dt_x
            ms = []
            for i, hh in enumerate((2 * p, 2 * p + 1)):
                seg = cum_h[i] - cum_t[hh:hh + 1, :]
                ms.append((cb * jnp.exp2(jnp.where(tril_m, seg, NEG_BIG))).astype(BF16))
            x_lo = jnp.where(lane_lo, xdt, 0.0).astype(BF16)
            x_hi = jnp.where(lane_lo, 0.0, xdt).astype(BF16)
            y = _dot(jnp.concatenate(ms, axis=1), jnp.concatenate([x_lo, x_hi], axis=0))
            st = stm_scr[:, cl]
            y = y + _dot(cm, st.astype(BF16)) * jnp.exp2(cum_x)
            y = y + dsk_ref[:, cl] * xs
            to_end = (xdt * jnp.exp2(last_x - cum_x)).astype(BF16)
            stm_scr[:, cl] = jnp.exp2(last_x) * st + _dot_tn(bm, to_end)
            y_scr[r, cl] = y * _silu(z_scr[r, cl])
        for grp in range(M_GROUPS):
            gl = slice(grp * M_GROUP_WIDTH, (grp + 1) * M_GROUP_WIDTH)
            mixm_ref[0, r, gl] = _rms(y_scr[r, gl], mn_ref[:, gl]).astype(BF16)


def _prompt_mixer_kernel(x_ref, lbl_ref, ln1_ref, whg_ref, wxd_ref, wz_ref, hgn_ref, cw_ref, cb_ref, dtb_ref,
                         alog_ref, dsk_ref, mn_ref,
                         mixh_ref, mixm_ref, hgs_ref, ssm_ref, conv_ref,
                         h_scr, hg_scr, z_scr, xbc_scr, dt_scr, xs_scr, bc_scr, y_scr, sth_scr, stm_scr, hist_scr,
                         *, tb, ch, cs):
    c = pl.program_id(1)
    last_c = pl.num_programs(1) - 1

    @pl.when(c == 0)
    def _():
        sth_scr[...] = jnp.zeros(sth_scr.shape, F32)
        stm_scr[...] = jnp.zeros(stm_scr.shape, F32)
        hist_scr[...] = jnp.zeros(hist_scr.shape, F32)

    bufs = (h_scr, hg_scr, z_scr, xbc_scr, dt_scr)
    _in_proj(x_ref[0], ln1_ref, (whg_ref, wxd_ref, wz_ref), bufs, tb)
    prm = (lbl_ref, hgn_ref, cw_ref, cb_ref, dtb_ref, alog_ref, dsk_ref, mn_ref)
    _recurrences(bufs, prm, mixh_ref, mixm_ref, (xs_scr, bc_scr, y_scr), (sth_scr, stm_scr, hist_scr), tb, ch, cs)

    @pl.when(c == last_c)
    def _():
        for hd in range(HG_HEADS):
            hgs_ref[0, hd] = sth_scr[hd].T
        for p in range(M_WIDTH // LANE):
            ssm_ref[0, p * LANE:(p + 1) * LANE, :] = stm_scr[:, p * LANE:(p + 1) * LANE].T
        conv_ref[0] = hist_scr[SUBLANE - (M_CONV - 1):SUBLANE, :]


def _const_spec(shape):
    nd = len(shape)
    return pl.BlockSpec(shape, lambda *_: (0,) * nd, pipeline_mode=pl.Buffered(1))


def _prompt_mixer(x, lbl, ln1, w_hg, w_xd, w_z, hgn, cw, cb, dtb, alog, dsk, mn, *, tb=256, ch=64, cs=128):
    bsz, seq, _ = x.shape
    nc = seq // tb
    kern = functools.partial(_prompt_mixer_kernel, tb=tb, ch=ch, cs=cs)
    params = (lbl, ln1, w_hg, w_xd, w_z, hgn, cw, cb, dtb, alog, dsk, mn)
    return pl.pallas_call(
        kern,
        grid=(bsz, nc),
        in_specs=[pl.BlockSpec((1, tb, D_MODEL), lambda b, c: (b, c, 0))] + [_const_spec(p.shape) for p in params],
        out_specs=[
            pl.BlockSpec((1, tb, HG_WIDTH), lambda b, c: (b, c, 0)),
            pl.BlockSpec((1, tb, M_WIDTH), lambda b, c: (b, c, 0)),
            pl.BlockSpec((1, HG_HEADS, HG_DK, HG_DV), lambda b, c: (b, 0, 0, 0)),
            pl.BlockSpec((1, M_WIDTH, M_DSTATE), lambda b, c: (b, 0, 0)),
            pl.BlockSpec((1, M_CONV - 1, M_CONV_DIM), lambda b, c: (b, 0, 0)),
        ],
        out_shape=[
            jax.ShapeDtypeStruct((bsz, seq, HG_WIDTH), BF16),
            jax.ShapeDtypeStruct((bsz, seq, M_WIDTH), BF16),
            jax.ShapeDtypeStruct((bsz, HG_HEADS, HG_DK, HG_DV), F32),
            jax.ShapeDtypeStruct((bsz, M_WIDTH, M_DSTATE), F32),
            jax.ShapeDtypeStruct((bsz, M_CONV - 1, M_CONV_DIM), F32),
        ],
        scratch_shapes=[
            pltpu.VMEM((tb, D_MODEL), BF16),
            pltpu.VMEM((tb, HG_PROJ_WIDTH), F32),
            pltpu.VMEM((tb, M_WIDTH), F32),
            pltpu.VMEM((tb + SUBLANE, M_CONV_DIM), F32),
            pltpu.VMEM((tb, LANE), F32),
            pltpu.VMEM((tb, M_WIDTH), F32),
            pltpu.VMEM((tb, 2 * M_GROUPS * M_DSTATE), BF16),
            pltpu.VMEM((tb, M_WIDTH), F32),
            pltpu.VMEM((HG_HEADS, HG_DV, HG_DK), F32),
            pltpu.VMEM((M_DSTATE, M_WIDTH), F32),
            pltpu.VMEM((SUBLANE, M_CONV_DIM), F32),
        ],
        compiler_params=pltpu.CompilerParams(
            dimension_semantics=("arbitrary", "arbitrary"), vmem_limit_bytes=VMEM_LIMIT),
        name="prompt_mixer",
    )(x, *params)


def _sample_in_proj_kernel(x_ref, ln1_ref, whg_ref, wxd_ref, wz_ref, o_ref, h_scr, *, n_hg, n_xd):
    n = pl.program_id(0)

    @pl.when(n == 0)
    def _():
        h_scr[...] = _rms(x_ref[...], ln1_ref[...]).astype(BF16)

    @pl.when(n < n_hg)
    def _():
        o_ref[...] = _dot(h_scr[...], whg_ref[...])

    @pl.when((n >= n_hg) & (n < n_hg + n_xd))
    def _():
        o_ref[...] = _dot(h_scr[...], wxd_ref[...])

    @pl.when(n >= n_hg + n_xd)
    def _():
        o_ref[...] = _dot(h_scr[...], wz_ref[...])


def _sample_in_proj(x, ln1, w_hg, w_xd, w_z, *, tn=1024):
    rows = x.shape[0]
    n_hg, n_xd, n_z = w_hg.shape[1] // tn, w_xd.shape[1] // tn, w_z.shape[1] // tn
    kern = functools.partial(_sample_in_proj_kernel, n_hg=n_hg, n_xd=n_xd)
    return pl.pallas_call(
        kern,
        grid=(n_hg + n_xd + n_z,),
        in_specs=[_const_spec((rows, D_MODEL)), _const_spec(ln1.shape),
                  pl.BlockSpec((D_MODEL, tn), lambda n: (0, jnp.minimum(n, n_hg - 1))),
                  pl.BlockSpec((D_MODEL, tn), lambda n: (0, jnp.clip(n - n_hg, 0, n_xd - 1))),
                  pl.BlockSpec((D_MODEL, tn), lambda n: (0, jnp.clip(n - n_hg - n_xd, 0, n_z - 1)))],
        out_specs=pl.BlockSpec((rows, tn), lambda n: (0, n)),
        out_shape=jax.ShapeDtypeStruct((rows, W_IN_COLS), F32),
        scratch_shapes=[pltpu.VMEM((rows, D_MODEL), BF16)],
        compiler_params=pltpu.CompilerParams(
            dimension_semantics=("arbitrary",), vmem_limit_bytes=VMEM_LIMIT),
        name="sample_in_proj",
    )(x, ln1, w_hg, w_xd, w_z)


SEG = 8


def _block_causal_mask(rows):
    ri = lax.broadcasted_iota(jnp.int32, (rows, rows), 0)
    ci = lax.broadcasted_iota(jnp.int32, (rows, rows), 1)
    return ((ri & -SEG) == (ci & -SEG)) & (ci <= ri)


def _seg_last(x):
    rows, width = x.shape
    x3 = x.reshape(rows // SEG, SEG, width)
    return jnp.broadcast_to(x3[:, SEG - 1:SEG, :], x3.shape).reshape(rows, width)


def _decay_tail(dec8):
    h, m, l = _split3(dec8)
    t = lax.broadcasted_iota(jnp.int32, dec8.shape, 0)
    return jnp.where(t == 0, h.astype(F32), jnp.where(t == 1, m.astype(F32), jnp.where(t == 2, l.astype(F32), 0.0)))


def _ones_tail():
    t = lax.broadcasted_iota(jnp.int32, (2 * SEG, LANE), 0)
    return jnp.where((t >= SEG) & (t < SEG + 3), 1.0, 0.0).astype(BF16)


def _sample_hgrn_kernel(p_ref, s0_ref, lbl_ref, hgn_ref, mixh_ref, s8_ref,
                        qd_scr, ke_scr, dec_scr, o_scr, *, nb):
    rows = nb * SEG
    lb = _lower_bound(lbl_ref[...], 0)
    mask = _block_causal_mask(rows)
    for hd in range(HG_HEADS):
        cl = slice(hd * LANE, (hd + 1) * LANE)
        q = p_ref[:, OFF_Q + hd * LANE:OFF_Q + (hd + 1) * LANE]
        fr = p_ref[:, OFF_F + hd * LANE:OFF_F + (hd + 1) * LANE]
        v = p_ref[:, OFF_I + hd * LANE:OFF_I + (hd + 1) * LANE]
        lbh = lb[:, cl]
        f = lbh + (1.0 - lbh) * _sigmoid(fr)
        a = jnp.log2(f)
        k = 1.0 - f
        cum = _cumsum_rows(a, SEG)
        last = _seg_last(cum)
        qd = (q * jnp.exp2(cum)).astype(BF16)
        kd = (k * jnp.exp2(-cum)).astype(BF16)
        att = jnp.where(mask, _dot_nt(qd, kd), 0.0).astype(BF16)
        o_scr[:, cl] = _dot(att, v.astype(BF16))
        qd_scr[:, cl] = qd
        ke_scr[:, cl] = k * jnp.exp2(last - cum)
        dec_scr[:, cl] = jnp.exp2(last)

    ones_tail = _ones_tail()
    zeros8 = jnp.zeros((SEG, LANE), F32)

    def pair_body(m, carry):
        r16 = pl.ds(pl.multiple_of(m * 2 * SEG, 2 * SEG), 2 * SEG)
        for par in range(2):
            i = 2 * m + par
            r8 = pl.ds(pl.multiple_of(i * SEG, SEG), SEG)
            for hd in range(HG_HEADS):
                cl = slice(hd * LANE, (hd + 1) * LANE)
                s0 = s0_ref[i, hd]
                oi = _dot(qd_scr[r16, cl], s0.astype(BF16))
                o_scr[r8, cl] = o_scr[r8, cl] + oi[par * SEG:(par + 1) * SEG, :]
                aug = jnp.concatenate([ke_scr[r8, cl], _decay_tail(dec_scr[r8, cl])], axis=0).astype(BF16)
                v8 = p_ref[r8, OFF_I + hd * LANE:OFF_I + (hd + 1) * LANE]
                rhs = jnp.concatenate([jnp.concatenate([v8, zeros8], axis=0).astype(BF16), ones_tail], axis=1)
                ud = _dot_tn(aug, rhs)
                s8_ref[i, hd] = ud[:, LANE:] * s0 + ud[:, :LANE]
        return carry

    lax.fori_loop(0, nb // 2, pair_body, 0)

    for hd in range(HG_HEADS):
        cl = slice(hd * LANE, (hd + 1) * LANE)
        g = p_ref[:, OFF_G + hd * LANE:OFF_G + (hd + 1) * LANE]
        mixh_ref[:, cl] = (_rms(o_scr[:, cl], hgn_ref[:, cl]) * _silu(g)).astype(BF16)


def _sample_hgrn(proj, s0, lbl, hgn, *, nb=16):
    nseq = s0.shape[0]
    rows = nb * SEG
    kern = functools.partial(_sample_hgrn_kernel, nb=nb)
    return pl.pallas_call(
        kern,
        grid=(nseq // nb,),
        in_specs=[pl.BlockSpec((rows, OFF_XBC), lambda i: (i, 0)),
                  pl.BlockSpec((nb, HG_HEADS, HG_DK, HG_DV), lambda i: (i, 0, 0, 0)),
                  _const_spec(lbl.shape), _const_spec(hgn.shape)],
        out_specs=[pl.BlockSpec((rows, HG_WIDTH), lambda i: (i, 0)),
                   pl.BlockSpec((nb, HG_HEADS, HG_DK, HG_DV), lambda i: (i, 0, 0, 0))],
        out_shape=[jax.ShapeDtypeStruct((nseq * SEG, HG_WIDTH), BF16),
                   jax.ShapeDtypeStruct(s0.shape, F32)],
        scratch_shapes=[pltpu.VMEM((rows, HG_WIDTH), BF16),
                        pltpu.VMEM((rows, HG_WIDTH), F32),
                        pltpu.VMEM((rows, HG_WIDTH), F32),
                        pltpu.VMEM((rows, HG_WIDTH), F32)],
        compiler_params=pltpu.CompilerParams(
            dimension_semantics=("arbitrary",), vmem_limit_bytes=VMEM_LIMIT),
        name="sample_hgrn",
    )(proj, s0, lbl, hgn)


def _sample_ssd_kernel(xd_ref, z_ref, hist_ref, s0_ref, cw_ref, cb_ref, dtb_ref, alog_ref, dsk_ref, mn_ref,
                       mixm_ref, s8_ref,
                       xs_scr, bm_scr, cm_scr, u_scr, cd_scr, fs_scr, y_scr, *, nb):
    rows = nb * SEG
    t_in_seq = lax.broadcasted_iota(jnp.int32, (rows, LANE), 0) & (SEG - 1)

    for cbk in range(M_CONV_DIM // LANE):
        cl = slice(cbk * LANE, (cbk + 1) * LANE)
        x = xd_ref[:, cl]
        hist = hist_ref[:, cl]
        acc = cb_ref[:, cl] + cw_ref[M_CONV - 1:M_CONV, cl] * x
        for d in range(1, M_CONV):
            shifted = jnp.where(t_in_seq >= d, pltpu.roll(x, d, 0), pltpu.roll(hist, rows - SEG + d, 0))
            acc = acc + cw_ref[M_CONV - 1 - d:M_CONV - d, cl] * shifted
        act = _silu(acc)
        if cbk < M_WIDTH // LANE:
            xs_scr[:, cl] = act
        elif cbk < (M_WIDTH + M_GROUPS * M_DSTATE) // LANE:
            bm_scr[:, cbk * LANE - M_WIDTH:(cbk + 1) * LANE - M_WIDTH] = act
        else:
            o0 = cbk * LANE - M_WIDTH - M_GROUPS * M_DSTATE
            cm_scr[:, o0:o0 + LANE] = act.astype(BF16)

    a_row = -jnp.exp(alog_ref[...]) * LOG2_E
    dtv = _softplus(xd_ref[:, M_CONV_DIM:M_CONV_DIM + LANE] + dtb_ref[...])
    cum = _cumsum_rows(dtv * a_row, SEG)
    cum_t = cum.T
    mask = _block_causal_mask(rows)
    lane_lo = lax.broadcasted_iota(jnp.int32, (rows, LANE), 1) < M_HEADDIM
    cbs = {}
    for p in range(M_HEADS // 2):
        grp = p // (M_HEADS // 2 // M_GROUPS)
        cl = slice(p * LANE, (p + 1) * LANE)
        if grp not in cbs:
            bm = bm_scr[:, grp * M_DSTATE:(grp + 1) * M_DSTATE].astype(BF16)
            cbs[grp] = _dot_nt(cm_scr[:, grp * M_DSTATE:(grp + 1) * M_DSTATE], bm)
        cb = cbs[grp]
        cum_h = [_lane_bcast(cum, hh) for hh in (2 * p, 2 * p + 1)]
        cum_x = jnp.where(lane_lo, cum_h[0], cum_h[1])
        dt_x = jnp.where(lane_lo, _lane_bcast(dtv, 2 * p), _lane_bcast(dtv, 2 * p + 1))
        last_x = _seg_last(cum_x)
        xs = xs_scr[:, cl]
        xdt = xs * dt_x
        ms = []
        for i, hh in enumerate((2 * p, 2 * p + 1)):
            seg = cum_h[i] - cum_t[hh:hh + 1, :]
            ms.append((cb * jnp.exp2(jnp.where(mask, seg, NEG_BIG))).astype(BF16))
        x_lo = jnp.where(lane_lo, xdt, 0.0).astype(BF16)
        x_hi = jnp.where(lane_lo, 0.0, xdt).astype(BF16)
        y = _dot(jnp.concatenate(ms, axis=1), jnp.concatenate([x_lo, x_hi], axis=0))
        y_scr[:, cl] = y + dsk_ref[:, cl] * xs
        u_scr[:, cl] = xdt * jnp.exp2(last_x - cum_x)
        cd_scr[:, cl] = jnp.exp2(last_x)
        fs_scr[:, cl] = jnp.exp2(cum_x)

    ones_tail = _ones_tail()
    zeros8 = jnp.zeros((SEG, M_DSTATE), F32)

    def pair_body(m, carry):
        r16 = pl.ds(pl.multiple_of(m * 2 * SEG, 2 * SEG), 2 * SEG)
        for par in range(2):
            i = 2 * m + par
            r8 = pl.ds(pl.multiple_of(i * SEG, SEG), SEG)
            for grp in range(M_GROUPS):
                gl = slice(grp * M_GROUP_WIDTH, (grp + 1) * M_GROUP_WIDTH)
                nl = slice(grp * M_DSTATE, (grp + 1) * M_DSTATE)
                s0 = s0_ref[i, gl, :]
                yi = _dot_nt(cm_scr[r16, nl], s0.astype(BF16))
                y_scr[r8, gl] = y_scr[r8, gl] + yi[par * SEG:(par + 1) * SEG, :] * fs_scr[r8, gl]
                aug = jnp.concatenate([u_scr[r8, gl], _decay_tail(cd_scr[r8, gl])], axis=0).astype(BF16)
                rhs = jnp.concatenate(
                    [jnp.concatenate([bm_scr[r8, nl], zeros8], axis=0).astype(BF16), ones_tail], axis=1)
                ud = _dot_tn(aug, rhs)
                s8_ref[i, gl, :] = ud[:, M_DSTATE:] * s0 + ud[:, :M_DSTATE]
        return carry

    lax.fori_loop(0, nb // 2, pair_body, 0)

    for grp in range(M_GROUPS):
        gl = slice(grp * M_GROUP_WIDTH, (grp + 1) * M_GROUP_WIDTH)
        y = y_scr[:, gl] * _silu(z_ref[:, gl])
        mixm_ref[:, gl] = _rms(y, mn_ref[:, gl]).astype(BF16)


def _sample_ssd(proj, hist, s0, cw, cb, dtb, alog, dsk, mn, *, nb=16):
    nseq = s0.shape[0]
    rows = nb * SEG
    kern = functools.partial(_sample_ssd_kernel, nb=nb)
    params = (cw, cb, dtb, alog, dsk, mn)
    return pl.pallas_call(
        kern,
        grid=(nseq // nb,),
        in_specs=[pl.BlockSpec((rows, XD_WIDTH), lambda i: (i, OFF_XBC // XD_WIDTH)),
                  pl.BlockSpec((rows, M_WIDTH), lambda i: (i, OFF_Z // M_WIDTH)),
                  pl.BlockSpec((rows, M_CONV_DIM), lambda i: (i, 0)),
                  pl.BlockSpec((nb, M_WIDTH, M_DSTATE), lambda i: (i, 0, 0))]
                 + [_const_spec(p.shape) for p in params],
        out_specs=[pl.BlockSpec((rows, M_WIDTH), lambda i: (i, 0)),
                   pl.BlockSpec((nb, M_WIDTH, M_DSTATE), lambda i: (i, 0, 0))],
        out_shape=[jax.ShapeDtypeStruct((nseq * SEG, M_WIDTH), BF16),
                   jax.ShapeDtypeStruct(s0.shape, F32)],
        scratch_shapes=[pltpu.VMEM((rows, M_WIDTH), F32),
                        pltpu.VMEM((rows, M_GROUPS * M_DSTATE), F32),
                        pltpu.VMEM((rows, M_GROUPS * M_DSTATE), BF16),
                        pltpu.VMEM((rows, M_WIDTH), F32),
                        pltpu.VMEM((rows, M_WIDTH), F32),
                        pltpu.VMEM((rows, M_WIDTH), F32),
                        pltpu.VMEM((rows, M_WIDTH), F32)],
        compiler_params=pltpu.CompilerParams(
            dimension_semantics=("arbitrary",), vmem_limit_bytes=VMEM_LIMIT),
        name="sample_ssd",
    )(proj, proj, hist, s0, *params)


def _out_mlp_kernel(x_ref, mh_ref, mm_ref, wo_ref, ln2_ref, wu_ref, wd_ref, lnf_ref, o_ref, *, ff_tile):
    x1 = x_ref[...] + _dot(mh_ref[...], wo_ref[0:HG_WIDTH, :]) + _dot(mm_ref[...], wo_ref[HG_WIDTH:, :])
    hn = _rms(x1, ln2_ref[...]).astype(BF16)
    mlp = None
    for j in range(D_FF // ff_tile):
        u = jnp.maximum(_dot(hn, wu_ref[:, j * ff_tile:(j + 1) * ff_tile]), 0.0)
        d = _dot((u * u).astype(BF16), wd_ref[j * ff_tile:(j + 1) * ff_tile, :])
        mlp = d if mlp is None else mlp + d
    o_ref[...] = _rms(x1 + mlp, lnf_ref[...])


def _out_mlp(x, mix_h, mix_m, w_out, ln2, w_up, w_down, ln_f, *, tm=512, ff_tile=1024):
    rows = x.shape[0]
    kern = functools.partial(_out_mlp_kernel, ff_tile=ff_tile)
    row_spec = lambda w: pl.BlockSpec((tm, w), lambda i: (i, 0))
    return pl.pallas_call(
        kern,
        grid=(rows // tm,),
        in_specs=[row_spec(D_MODEL), row_spec(HG_WIDTH), row_spec(M_WIDTH),
                  _const_spec(w_out.shape), _const_spec(ln2.shape), _const_spec(w_up.shape),
                  _const_spec(w_down.shape), _const_spec(ln_f.shape)],
        out_specs=row_spec(D_MODEL),
        out_shape=jax.ShapeDtypeStruct((rows, D_MODEL), F32),
        compiler_params=pltpu.CompilerParams(
            dimension_semantics=("arbitrary",), vmem_limit_bytes=VMEM_LIMIT),
        name="out_mlp",
    )(x, mix_h, mix_m, w_out, ln2, w_up, w_down, ln_f)


def _pad_lanes(v):
    return jnp.pad(v, ((0, 0), (0, LANE - v.shape[1])))


def kernel(x_prompt, x_sample, state_hgrn, state_ssm, state_conv, hg_lb_logits, ln1, w_in, hg_norm, conv_w,
           conv_b, dt_bias, a_log, d_skip, m_norm, w_out, ln2, w_up, w_down, ln_f):
    l = 0
    bp, seq, _ = x_prompt.shape
    bs, dseq, _ = x_sample.shape
    assert DEPTH == 1 and dseq == SEG

    w = w_in[l]
    c_z = HG_PROJ_WIDTH + M_WIDTH
    w_hg = w[:, :HG_PROJ_WIDTH].astype(BF16)
    w_z = w[:, HG_PROJ_WIDTH:c_z].astype(BF16)
    w_xd = jnp.pad(w[:, c_z:].astype(BF16), ((0, 0), (0, XD_WIDTH - (w.shape[1] - c_z))))
    lbl = hg_lb_logits.astype(F32)
    ln1_r = ln1[l][None, :]
    hgn_r = hg_norm[l].reshape(1, HG_WIDTH)
    cw = conv_w[l]
    cb_r = conv_b[l][None, :]
    dtb_r = _pad_lanes(dt_bias[l][None, :])
    alog_r = _pad_lanes(a_log[l][None, :])
    dsk_r = jnp.repeat(d_skip[l], M_HEADDIM)[None, :]
    mn_r = m_norm[l][None, :]
    mlp_w = (w_out[l].astype(BF16), ln2[l][None, :], w_up[l].astype(BF16), w_down[l].astype(BF16), ln_f[None, :])

    mixh_p, mixm_p, hgs_p, ssm_p, conv_p = _prompt_mixer(
        x_prompt, lbl, ln1_r, w_hg, w_xd, w_z, hgn_r, cw, cb_r, dtb_r, alog_r, dsk_r, mn_r)
    y_p = _out_mlp(x_prompt.reshape(bp * seq, D_MODEL), mixh_p.reshape(bp * seq, HG_WIDTH),
                   mixm_p.reshape(bp * seq, M_WIDTH), *mlp_w)

    xs2 = x_sample.reshape(bs * SEG, D_MODEL)
    proj_s = _sample_in_proj(xs2, ln1_r, w_hg, w_xd, w_z)
    mixh_s, hgs_s = _sample_hgrn(proj_s, state_hgrn[l], lbl, hgn_r)
    hist = jnp.pad(state_conv[l], ((0, 0), (SEG - (M_CONV - 1), 0), (0, 0))).reshape(bs * SEG, M_CONV_DIM)
    mixm_s, ssm_s = _sample_ssd(proj_s, hist, state_ssm[l].reshape(bs, M_WIDTH, M_DSTATE),
                                cw, cb_r, dtb_r, alog_r, dsk_r, mn_r)
    y_s = _out_mlp(xs2, mixh_s, mixm_s, *mlp_w)
    conv_s = proj_s.reshape(bs, SEG, W_IN_COLS)[:, SEG - (M_CONV - 1):, OFF_XBC:OFF_DT]

    return (y_p.reshape(bp, seq, D_MODEL), y_s.reshape(bs, SEG, D_MODEL),
            hgs_p[None], hgs_s[None],
            ssm_p.reshape(1, bp, M_HEADS, M_HEADDIM, M_DSTATE), ssm_s.reshape(1, bs, M_HEADS, M_HEADDIM, M_DSTATE),
            conv_p[None], conv_s[None])
```

```python
import functools

import jax
import jax.numpy as jnp
from jax import lax
from jax.experimental import pallas as pl
from jax.experimental.pallas import tpu as pltpu

F32 = jnp.float32
BF16 = jnp.bfloat16

D_MODEL = 1024
DEPTH = 1
HG_HEADS = 8
HG_DK = 128
HG_DV = 128
HG_WIDTH = HG_HEADS * HG_DV
M_WIDTH = 1024
M_HEADDIM = 64
M_HEADS = M_WIDTH // M_HEADDIM
M_DSTATE = 128
M_GROUPS = 2
M_GROUP_WIDTH = M_WIDTH // M_GROUPS
M_CONV = 4
M_CONV_DIM = M_WIDTH + 2 * M_GROUPS * M_DSTATE
D_FF = 4 * D_MODEL
NORM_EPS = 1e-5

LANE = 128
SUBLANE = 8

OFF_Q, OFF_F, OFF_I, OFF_G = 0, 1024, 2048, 3072
HG_PROJ_WIDTH = 4096
OFF_XBC = HG_PROJ_WIDTH
OFF_DT = OFF_XBC + M_CONV_DIM
XD_WIDTH = 2048
OFF_Z = OFF_XBC + XD_WIDTH
W_IN_COLS = OFF_Z + M_WIDTH
PACK_TILE = 1024

VMEM_LIMIT = 56 * 1024 * 1024

NEG_BIG = -1e30


def _dot(a, b):
    return jnp.dot(a, b, preferred_element_type=F32)


def _dot_nt(a, b):
    return lax.dot_general(a, b, (((1,), (1,)), ((), ())), preferred_element_type=F32)


def _dot_tn(a, b):
    return lax.dot_general(a, b, (((0,), (0,)), ((), ())), preferred_element_type=F32)


def _sigmoid(x):
    return 1.0 / (1.0 + jnp.exp(-x))


def _silu(x):
    return x * _sigmoid(x)


def _softplus(x):
    return jnp.maximum(x, 0.0) + jnp.log(1.0 + jnp.exp(-jnp.abs(x)))


def _rms(x, gain):
    ms = jnp.mean(x * x, axis=-1, keepdims=True)
    return x * lax.rsqrt(ms + NORM_EPS) * gain


def _lower_bound(lbl, layer):
    rows = [lbl[i:i + 1, :] for i in range(DEPTH + 1)]
    m = functools.reduce(jnp.maximum, rows)
    es = [jnp.exp(r - m) for r in rows]
    return sum(es[:layer + 1]) / sum(es)


def _cumsum_rows(x, seg):
    t = lax.broadcasted_iota(jnp.int32, x.shape, 0) & (seg - 1)
    s = 1
    while s < seg:
        x = x + jnp.where(t >= s, pltpu.roll(x, s, 0), 0.0)
        s *= 2
    return x


def _split3(x):
    h = x.astype(BF16)
    r = x - h.astype(F32)
    m = r.astype(BF16)
    l = (r - m.astype(F32)).astype(BF16)
    return h, m, l


def _lane_bcast(x, lane):
    return jnp.broadcast_to(x[:, lane:lane + 1], x.shape)


def _pack_in_proj_kernel(w_ref, o_ref, *, tn, n_hg, n_xd, src_cols):
    j = pl.program_id(0)
    col0 = _pack_src_tile(j, n_hg, n_xd) * tn
    col = col0 + lax.broadcasted_iota(jnp.int32, w_ref.shape, 1)
    o_ref[...] = jnp.where(col < src_cols, w_ref[...], 0.0).astype(BF16)


def _pack_src_tile(j, n_hg, n_xd):
    n_z = M_WIDTH // PACK_TILE
    return jnp.where(j < n_hg, j, jnp.where(j < n_hg + n_xd, j + n_z, j - n_xd))


def _pack_in_proj(w):
    tn = PACK_TILE
    n_hg, n_xd = HG_PROJ_WIDTH // tn, XD_WIDTH // tn
    kern = functools.partial(_pack_in_proj_kernel, tn=tn, n_hg=n_hg, n_xd=n_xd, src_cols=w.shape[1])
    return pl.pallas_call(
        kern,
        grid=(W_IN_COLS // tn,),
        in_specs=[pl.BlockSpec((D_MODEL, tn), lambda j: (0, _pack_src_tile(j, n_hg, n_xd)))],
        out_specs=pl.BlockSpec((D_MODEL, tn), lambda j: (0, j)),
        out_shape=jax.ShapeDtypeStruct((D_MODEL, W_IN_COLS), BF16),
        compiler_params=pltpu.CompilerParams(
            dimension_semantics=("arbitrary",), vmem_limit_bytes=VMEM_LIMIT),
        name="pack_in_proj",
    )(w)


def _in_proj(x, ln1_ref, w_ref, bufs, tb):
    h_scr, hg_scr, z_scr, xbc_scr, dt_scr = bufs
    h_scr[...] = _rms(x, ln1_ref[...]).astype(BF16)
    for n0 in range(0, HG_PROJ_WIDTH, 1024):
        hg_scr[:, n0:n0 + 1024] = _dot(h_scr[...], w_ref[:, n0:n0 + 1024])
    xbc_scr[SUBLANE:SUBLANE + tb, :] = _dot(h_scr[...], w_ref[:, OFF_XBC:OFF_DT])
    dt_scr[...] = _dot(h_scr[...], w_ref[:, OFF_DT:OFF_DT + LANE])
    z_scr[...] = _dot(h_scr[...], w_ref[:, OFF_Z:OFF_Z + M_WIDTH])


def _recurrences(bufs, prm, mixh_ref, mixm_ref, work, state, tb, ch, cs):
    _, hg_scr, z_scr, xbc_scr, dt_scr = bufs
    lbl_ref, hgn_ref, cw_ref, cb_ref, dtb_ref, alog_ref, dsk_ref, mn_ref = prm
    xs_scr, bc_scr, y_scr = work
    sth_scr, stm_scr, hist_scr = state

    xbc_scr[0:SUBLANE, :] = hist_scr[...]

    lb = _lower_bound(lbl_ref[...], 0)
    tril_h = (lax.broadcasted_iota(jnp.int32, (ch, ch), 0) >= lax.broadcasted_iota(jnp.int32, (ch, ch), 1))
    for j in range(tb // ch):
        r = slice(j * ch, (j + 1) * ch)
        for hd in range(HG_HEADS):
            cl = slice(hd * LANE, (hd + 1) * LANE)
            q = hg_scr[r, OFF_Q + hd * LANE:OFF_Q + (hd + 1) * LANE]
            fr = hg_scr[r, OFF_F + hd * LANE:OFF_F + (hd + 1) * LANE]
            v = hg_scr[r, OFF_I + hd * LANE:OFF_I + (hd + 1) * LANE]
            g = hg_scr[r, OFF_G + hd * LANE:OFF_G + (hd + 1) * LANE]
            lbh = lb[:, cl]
            f = lbh + (1.0 - lbh) * _sigmoid(fr)
            a = jnp.log(f)
            k = 1.0 - f
            cum = _cumsum_rows(a, ch)
            mid = cum[ch // 2 - 1:ch // 2, :]
            last = cum[ch - 1:ch, :]
            qd_mid = q * jnp.exp(cum - mid)
            kd_mid = k * jnp.exp(mid - cum)
            qd = (qd_mid * jnp.exp(mid)).astype(BF16)
            k_end = (kd_mid * jnp.exp(last - mid)).astype(BF16)
            dec = jnp.exp(last)
            vb = v.astype(BF16)
            att = _dot_nt(qd_mid.astype(BF16), kd_mid.astype(BF16))
            att = jnp.where(tril_h, att, 0.0).astype(BF16)
            st = sth_scr[hd]
            o = _dot(att, vb) + _dot_nt(qd, st.astype(BF16))
            sth_scr[hd] = dec * st + _dot_tn(vb, k_end)
            on = _rms(o, hgn_ref[:, cl]) * _silu(g)
            mixh_ref[0, r, cl] = on.astype(BF16)

    for cbk in range(M_CONV_DIM // LANE):
        cl = slice(cbk * LANE, (cbk + 1) * LANE)
        acc = cb_ref[:, cl] + cw_ref[M_CONV - 1:M_CONV, cl] * xbc_scr[SUBLANE:SUBLANE + tb, cl]
        for d in range(1, M_CONV):
            acc = acc + cw_ref[M_CONV - 1 - d:M_CONV - d, cl] * xbc_scr[SUBLANE - d:SUBLANE - d + tb, cl]
        act = _silu(acc)
        if cbk < M_WIDTH // LANE:
            xs_scr[:, cl] = act
        else:
            bc_scr[:, (cbk * LANE - M_WIDTH):(cbk * LANE - M_WIDTH) + LANE] = act.astype(BF16)
    hist_scr[...] = xbc_scr[tb:tb + SUBLANE, :]

    a_row = -jnp.exp(alog_ref[...])
    tril_m = (lax.broadcasted_iota(jnp.int32, (cs, cs), 0) >= lax.broadcasted_iota(jnp.int32, (cs, cs), 1))
    lane_lo = lax.broadcasted_iota(jnp.int32, (cs, LANE), 1) < M_HEADDIM
    for s in range(tb // cs):
        r = slice(s * cs, (s + 1) * cs)
        dtv = _softplus(dt_scr[r, :] + dtb_ref[...])
        cum = _cumsum_rows(dtv * a_row, cs)
        cum_t = cum.T
        cbs = {}
        for p in range(M_HEADS // 2):
            grp = p // (M_HEADS // 2 // M_GROUPS)
            cl = slice(p * LANE, (p + 1) * LANE)
            bm = bc_scr[r, grp * M_DSTATE:(grp + 1) * M_DSTATE]
            cm = bc_scr[r, (M_GROUPS + grp) * M_DSTATE:(M_GROUPS + grp + 1) * M_DSTATE]
            if grp not in cbs:
                cbs[grp] = _dot_nt(cm, bm)
            cb = cbs[grp]
            cum_h = [_lane_bcast(cum, hh) for hh in (2 * p, 2 * p + 1)]
            cum_x = jnp.where(lane_lo, cum_h[0], cum_h[1])
            dt_x = jnp.where(lane_lo, _lane_bcast(dtv, 2 * p), _lane_bcast(dtv, 2 * p + 1))
            last_x = cum_x[cs - 1:cs, :]
            xs = xs_scr[r, cl]
            xdt = xs * dt_x
            ms = []
            for i, hh in enumerate((2 * p, 2 * p + 1)):
                seg = cum_h[i] - cum_t[hh:hh + 1, :]
                ms.append((cb * jnp.exp(jnp.where(tril_m, seg, NEG_BIG))).astype(BF16))
            x_lo = jnp.where(lane_lo, xdt, 0.0).astype(BF16)
            x_hi = jnp.where(lane_lo, 0.0, xdt).astype(BF16)
            y = _dot(jnp.concatenate(ms, axis=1), jnp.concatenate([x_lo, x_hi], axis=0))
            st = stm_scr[:, cl]
            y = y + _dot(cm, st.astype(BF16)) * jnp.exp(cum_x)
            y = y + dsk_ref[:, cl] * xs
            to_end = (xdt * jnp.exp(last_x - cum_x)).astype(BF16)
            stm_scr[:, cl] = jnp.exp(last_x) * st + _dot_tn(bm, to_end)
            y_scr[r, cl] = y * _silu(z_scr[r, cl])
        for grp in range(M_GROUPS):
            gl = slice(grp * M_GROUP_WIDTH, (grp + 1) * M_GROUP_WIDTH)
            mixm_ref[0, r, gl] = _rms(y_scr[r, gl], mn_ref[:, gl]).astype(BF16)


def _prompt_mixer_kernel(x_ref, lbl_ref, ln1_ref, w_ref, hgn_ref, cw_ref, cb_ref, dtb_ref, alog_ref, dsk_ref, mn_ref,
                         mixh_ref, mixm_ref, hgs_ref, ssm_ref, conv_ref,
                         h_scr, hg_scr, z_scr, xbc_scr, dt_scr, xs_scr, bc_scr, y_scr, sth_scr, stm_scr, hist_scr,
                         *, tb, ch, cs):
    c = pl.program_id(1)
    last_c = pl.num_programs(1) - 1

    @pl.when(c == 0)
    def _():
        sth_scr[...] = jnp.zeros(sth_scr.shape, F32)
        stm_scr[...] = jnp.zeros(stm_scr.shape, F32)
        hist_scr[...] = jnp.zeros(hist_scr.shape, F32)

    bufs = (h_scr, hg_scr, z_scr, xbc_scr, dt_scr)
    _in_proj(x_ref[0], ln1_ref, w_ref, bufs, tb)
    prm = (lbl_ref, hgn_ref, cw_ref, cb_ref, dtb_ref, alog_ref, dsk_ref, mn_ref)
    _recurrences(bufs, prm, mixh_ref, mixm_ref, (xs_scr, bc_scr, y_scr), (sth_scr, stm_scr, hist_scr), tb, ch, cs)

    @pl.when(c == last_c)
    def _():
        for hd in range(HG_HEADS):
            hgs_ref[0, hd] = sth_scr[hd].T
        for p in range(M_WIDTH // LANE):
            ssm_ref[0, p * LANE:(p + 1) * LANE, :] = stm_scr[:, p * LANE:(p + 1) * LANE].T
        conv_ref[0] = hist_scr[SUBLANE - (M_CONV - 1):SUBLANE, :]


def _const_spec(shape):
    nd = len(shape)
    return pl.BlockSpec(shape, lambda *_: (0,) * nd, pipeline_mode=pl.Buffered(1))


def _prompt_mixer(x, lbl, ln1, w_all, hgn, cw, cb, dtb, alog, dsk, mn, *, tb=256, ch=64, cs=128):
    bsz, seq, _ = x.shape
    nc = seq // tb
    kern = functools.partial(_prompt_mixer_kernel, tb=tb, ch=ch, cs=cs)
    params = (lbl, ln1, w_all, hgn, cw, cb, dtb, alog, dsk, mn)
    return pl.pallas_call(
        kern,
        grid=(bsz, nc),
        in_specs=[pl.BlockSpec((1, tb, D_MODEL), lambda b, c: (b, c, 0))] + [_const_spec(p.shape) for p in params],
        out_specs=[
            pl.BlockSpec((1, tb, HG_WIDTH), lambda b, c: (b, c, 0)),
            pl.BlockSpec((1, tb, M_WIDTH), lambda b, c: (b, c, 0)),
            pl.BlockSpec((1, HG_HEADS, HG_DK, HG_DV), lambda b, c: (b, 0, 0, 0)),
            pl.BlockSpec((1, M_WIDTH, M_DSTATE), lambda b, c: (b, 0, 0)),
            pl.BlockSpec((1, M_CONV - 1, M_CONV_DIM), lambda b, c: (b, 0, 0)),
        ],
        out_shape=[
            jax.ShapeDtypeStruct((bsz, seq, HG_WIDTH), BF16),
            jax.ShapeDtypeStruct((bsz, seq, M_WIDTH), BF16),
            jax.ShapeDtypeStruct((bsz, HG_HEADS, HG_DK, HG_DV), F32),
            jax.ShapeDtypeStruct((bsz, M_WIDTH, M_DSTATE), F32),
            jax.ShapeDtypeStruct((bsz, M_CONV - 1, M_CONV_DIM), F32),
        ],
        scratch_shapes=[
            pltpu.VMEM((tb, D_MODEL), BF16),
            pltpu.VMEM((tb, HG_PROJ_WIDTH), F32),
            pltpu.VMEM((tb, M_WIDTH), F32),
            pltpu.VMEM((tb + SUBLANE, M_CONV_DIM), F32),
            pltpu.VMEM((tb, LANE), F32),
            pltpu.VMEM((tb, M_WIDTH), F32),
            pltpu.VMEM((tb, 2 * M_GROUPS * M_DSTATE), BF16),
            pltpu.VMEM((tb, M_WIDTH), F32),
            pltpu.VMEM((HG_HEADS, HG_DV, HG_DK), F32),
            pltpu.VMEM((M_DSTATE, M_WIDTH), F32),
            pltpu.VMEM((SUBLANE, M_CONV_DIM), F32),
        ],
        compiler_params=pltpu.CompilerParams(
            dimension_semantics=("arbitrary", "arbitrary"), vmem_limit_bytes=VMEM_LIMIT),
        name="prompt_mixer",
    )(x, *params)


def _sample_in_proj_kernel(x_ref, ln1_ref, w_ref, o_ref, h_scr):
    @pl.when(pl.program_id(0) == 0)
    def _():
        h_scr[...] = _rms(x_ref[...], ln1_ref[...]).astype(BF16)

    o_ref[...] = _dot(h_scr[...], w_ref[...])


def _sample_in_proj(x, ln1, w_all, *, tn=1024):
    rows = x.shape[0]
    return pl.pallas_call(
        _sample_in_proj_kernel,
        grid=(W_IN_COLS // tn,),
        in_specs=[_const_spec((rows, D_MODEL)), _const_spec(ln1.shape),
                  pl.BlockSpec((D_MODEL, tn), lambda n: (0, n))],
        out_specs=pl.BlockSpec((rows, tn), lambda n: (0, n)),
        out_shape=jax.ShapeDtypeStruct((rows, W_IN_COLS), F32),
        scratch_shapes=[pltpu.VMEM((rows, D_MODEL), BF16)],
        compiler_params=pltpu.CompilerParams(
            dimension_semantics=("arbitrary",), vmem_limit_bytes=VMEM_LIMIT),
        name="sample_in_proj",
    )(x, ln1, w_all)


SEG = 8


def _block_causal_mask(rows):
    ri = lax.broadcasted_iota(jnp.int32, (rows, rows), 0)
    ci = lax.broadcasted_iota(jnp.int32, (rows, rows), 1)
    return ((ri & -SEG) == (ci & -SEG)) & (ci <= ri)


def _seg_last(x):
    rows, width = x.shape
    x3 = x.reshape(rows // SEG, SEG, width)
    return jnp.broadcast_to(x3[:, SEG - 1:SEG, :], x3.shape).reshape(rows, width)


def _decay_tail(dec8):
    h, m, l = _split3(dec8)
    t = lax.broadcasted_iota(jnp.int32, dec8.shape, 0)
    return jnp.where(t == 0, h.astype(F32), jnp.where(t == 1, m.astype(F32), jnp.where(t == 2, l.astype(F32), 0.0)))


def _ones_tail():
    t = lax.broadcasted_iota(jnp.int32, (2 * SEG, LANE), 0)
    return jnp.where((t >= SEG) & (t < SEG + 3), 1.0, 0.0).astype(BF16)


def _sample_hgrn_kernel(p_ref, s0_ref, lbl_ref, hgn_ref, mixh_ref, s8_ref,
                        qd_scr, ke_scr, dec_scr, o_scr, *, nb):
    rows = nb * SEG
    lb = _lower_bound(lbl_ref[...], 0)
    mask = _block_causal_mask(rows)
    for hd in range(HG_HEADS):
        cl = slice(hd * LANE, (hd + 1) * LANE)
        q = p_ref[:, OFF_Q + hd * LANE:OFF_Q + (hd + 1) * LANE]
        fr = p_ref[:, OFF_F + hd * LANE:OFF_F + (hd + 1) * LANE]
        v = p_ref[:, OFF_I + hd * LANE:OFF_I + (hd + 1) * LANE]
        lbh = lb[:, cl]
        f = lbh + (1.0 - lbh) * _sigmoid(fr)
        a = jnp.log(f)
        k = 1.0 - f
        cum = _cumsum_rows(a, SEG)
        last = _seg_last(cum)
        qd = (q * jnp.exp(cum)).astype(BF16)
        kd = (k * jnp.exp(-cum)).astype(BF16)
        att = jnp.where(mask, _dot_nt(qd, kd), 0.0).astype(BF16)
        o_scr[:, cl] = _dot(att, v.astype(BF16))
        qd_scr[:, cl] = qd
        ke_scr[:, cl] = k * jnp.exp(last - cum)
        dec_scr[:, cl] = jnp.exp(last)

    ones_tail = _ones_tail()
    zeros8 = jnp.zeros((SEG, LANE), F32)

    def pair_body(m, carry):
        r16 = pl.ds(pl.multiple_of(m * 2 * SEG, 2 * SEG), 2 * SEG)
        for par in range(2):
            i = 2 * m + par
            r8 = pl.ds(pl.multiple_of(i * SEG, SEG), SEG)
            for hd in range(HG_HEADS):
                cl = slice(hd * LANE, (hd + 1) * LANE)
                s0 = s0_ref[i, hd]
                oi = _dot(qd_scr[r16, cl], s0.astype(BF16))
                o_scr[r8, cl] = o_scr[r8, cl] + oi[par * SEG:(par + 1) * SEG, :]
                aug = jnp.concatenate([ke_scr[r8, cl], _decay_tail(dec_scr[r8, cl])], axis=0).astype(BF16)
                v8 = p_ref[r8, OFF_I + hd * LANE:OFF_I + (hd + 1) * LANE]
                rhs = jnp.concatenate([jnp.concatenate([v8, zeros8], axis=0).astype(BF16), ones_tail], axis=1)
                ud = _dot_tn(aug, rhs)
                s8_ref[i, hd] = ud[:, LANE:] * s0 + ud[:, :LANE]
        return carry

    lax.fori_loop(0, nb // 2, pair_body, 0)

    for hd in range(HG_HEADS):
        cl = slice(hd * LANE, (hd + 1) * LANE)
        g = p_ref[:, OFF_G + hd * LANE:OFF_G + (hd + 1) * LANE]
        mixh_ref[:, cl] = (_rms(o_scr[:, cl], hgn_ref[:, cl]) * _silu(g)).astype(BF16)


def _sample_hgrn(proj, s0, lbl, hgn, *, nb=16):
    nseq = s0.shape[0]
    rows = nb * SEG
    kern = functools.partial(_sample_hgrn_kernel, nb=nb)
    return pl.pallas_call(
        kern,
        grid=(nseq // nb,),
        in_specs=[pl.BlockSpec((rows, OFF_XBC), lambda i: (i, 0)),
                  pl.BlockSpec((nb, HG_HEADS, HG_DK, HG_DV), lambda i: (i, 0, 0, 0)),
                  _const_spec(lbl.shape), _const_spec(hgn.shape)],
        out_specs=[pl.BlockSpec((rows, HG_WIDTH), lambda i: (i, 0)),
                   pl.BlockSpec((nb, HG_HEADS, HG_DK, HG_DV), lambda i: (i, 0, 0, 0))],
        out_shape=[jax.ShapeDtypeStruct((nseq * SEG, HG_WIDTH), BF16),
                   jax.ShapeDtypeStruct(s0.shape, F32)],
        scratch_shapes=[pltpu.VMEM((rows, HG_WIDTH), BF16),
                        pltpu.VMEM((rows, HG_WIDTH), F32),
                        pltpu.VMEM((rows, HG_WIDTH), F32),
                        pltpu.VMEM((rows, HG_WIDTH), F32)],
        compiler_params=pltpu.CompilerParams(
            dimension_semantics=("arbitrary",), vmem_limit_bytes=VMEM_LIMIT),
        name="sample_hgrn",
    )(proj, s0, lbl, hgn)


def _sample_ssd_kernel(xd_ref, z_ref, hist_ref, s0_ref, cw_ref, cb_ref, dtb_ref, alog_ref, dsk_ref, mn_ref,
                       mixm_ref, s8_ref,
                       xs_scr, bm_scr, cm_scr, u_scr, cd_scr, fs_scr, y_scr, *, nb):
    rows = nb * SEG
    t_in_seq = lax.broadcasted_iota(jnp.int32, (rows, LANE), 0) & (SEG - 1)

    for cbk in range(M_CONV_DIM // LANE):
        cl = slice(cbk * LANE, (cbk + 1) * LANE)
        x = xd_ref[:, cl]
        hist = hist_ref[:, cl]
        acc = cb_ref[:, cl] + cw_ref[M_CONV - 1:M_CONV, cl] * x
        for d in range(1, M_CONV):
            shifted = jnp.where(t_in_seq >= d, pltpu.roll(x, d, 0), pltpu.roll(hist, rows - SEG + d, 0))
            acc = acc + cw_ref[M_CONV - 1 - d:M_CONV - d, cl] * shifted
        act = _silu(acc)
        if cbk < M_WIDTH // LANE:
            xs_scr[:, cl] = act
        elif cbk < (M_WIDTH + M_GROUPS * M_DSTATE) // LANE:
            bm_scr[:, cbk * LANE - M_WIDTH:(cbk + 1) * LANE - M_WIDTH] = act
        else:
            o0 = cbk * LANE - M_WIDTH - M_GROUPS * M_DSTATE
            cm_scr[:, o0:o0 + LANE] = act.astype(BF16)

    a_row = -jnp.exp(alog_ref[...])
    dtv = _softplus(xd_ref[:, M_CONV_DIM:M_CONV_DIM + LANE] + dtb_ref[...])
    cum = _cumsum_rows(dtv * a_row, SEG)
    cum_t = cum.T
    mask = _block_causal_mask(rows)
    lane_lo = lax.broadcasted_iota(jnp.int32, (rows, LANE), 1) < M_HEADDIM
    cbs = {}
    for p in range(M_HEADS // 2):
        grp = p // (M_HEADS // 2 // M_GROUPS)
        cl = slice(p * LANE, (p + 1) * LANE)
        if grp not in cbs:
            bm = bm_scr[:, grp * M_DSTATE:(grp + 1) * M_DSTATE].astype(BF16)
            cbs[grp] = _dot_nt(cm_scr[:, grp * M_DSTATE:(grp + 1) * M_DSTATE], bm)
        cb = cbs[grp]
        cum_h = [_lane_bcast(cum, hh) for hh in (2 * p, 2 * p + 1)]
        cum_x = jnp.where(lane_lo, cum_h[0], cum_h[1])
        dt_x = jnp.where(lane_lo, _lane_bcast(dtv, 2 * p), _lane_bcast(dtv, 2 * p + 1))
        last_x = _seg_last(cum_x)
        xs = xs_scr[:, cl]
        xdt = xs * dt_x
        ms = []
        for i, hh in enumerate((2 * p, 2 * p + 1)):
            seg = cum_h[i] - cum_t[hh:hh + 1, :]
            ms.append((cb * jnp.exp(jnp.where(mask, seg, NEG_BIG))).astype(BF16))
        x_lo = jnp.where(lane_lo, xdt, 0.0).astype(BF16)
        x_hi = jnp.where(lane_lo, 0.0, xdt).astype(BF16)
        y = _dot(jnp.concatenate(ms, axis=1), jnp.concatenate([x_lo, x_hi], axis=0))
        y_scr[:, cl] = y + dsk_ref[:, cl] * xs
        u_scr[:, cl] = xdt * jnp.exp(last_x - cum_x)
        cd_scr[:, cl] = jnp.exp(last_x)
        fs_scr[:, cl] = jnp.exp(cum_x)

    ones_tail = _ones_tail()
    zeros8 = jnp.zeros((SEG, M_DSTATE), F32)

    def pair_body(m, carry):
        r16 = pl.ds(pl.multiple_of(m * 2 * SEG, 2 * SEG), 2 * SEG)
        for par in range(2):
            i = 2 * m + par
            r8 = pl.ds(pl.multiple_of(i * SEG, SEG), SEG)
            for grp in range(M_GROUPS):
                gl = slice(grp * M_GROUP_WIDTH, (grp + 1) * M_GROUP_WIDTH)
                nl = slice(grp * M_DSTATE, (grp + 1) * M_DSTATE)
                s0 = s0_ref[i, gl, :]
                yi = _dot_nt(cm_scr[r16, nl], s0.astype(BF16))
                y_scr[r8, gl] = y_scr[r8, gl] + yi[par * SEG:(par + 1) * SEG, :] * fs_scr[r8, gl]
                aug = jnp.concatenate([u_scr[r8, gl], _decay_tail(cd_scr[r8, gl])], axis=0).astype(BF16)
                rhs = jnp.concatenate(
                    [jnp.concatenate([bm_scr[r8, nl], zeros8], axis=0).astype(BF16), ones_tail], axis=1)
                ud = _dot_tn(aug, rhs)
                s8_ref[i, gl, :] = ud[:, M_DSTATE:] * s0 + ud[:, :M_DSTATE]
        return carry

    lax.fori_loop(0, nb // 2, pair_body, 0)

    for grp in range(M_GROUPS):
        gl = slice(grp * M_GROUP_WIDTH, (grp + 1) * M_GROUP_WIDTH)
        y = y_scr[:, gl] * _silu(z_ref[:, gl])
        mixm_ref[:, gl] = _rms(y, mn_ref[:, gl]).astype(BF16)


def _sample_ssd(proj, hist, s0, cw, cb, dtb, alog, dsk, mn, *, nb=16):
    nseq = s0.shape[0]
    rows = nb * SEG
    kern = functools.partial(_sample_ssd_kernel, nb=nb)
    params = (cw, cb, dtb, alog, dsk, mn)
    return pl.pallas_call(
        kern,
        grid=(nseq // nb,),
        in_specs=[pl.BlockSpec((rows, XD_WIDTH), lambda i: (i, OFF_XBC // XD_WIDTH)),
                  pl.BlockSpec((rows, M_WIDTH), lambda i: (i, OFF_Z // M_WIDTH)),
                  pl.BlockSpec((rows, M_CONV_DIM), lambda i: (i, 0)),
                  pl.BlockSpec((nb, M_WIDTH, M_DSTATE), lambda i: (i, 0, 0))]
                 + [_const_spec(p.shape) for p in params],
        out_specs=[pl.BlockSpec((rows, M_WIDTH), lambda i: (i, 0)),
                   pl.BlockSpec((nb, M_WIDTH, M_DSTATE), lambda i: (i, 0, 0))],
        out_shape=[jax.ShapeDtypeStruct((nseq * SEG, M_WIDTH), BF16),
                   jax.ShapeDtypeStruct(s0.shape, F32)],
        scratch_shapes=[pltpu.VMEM((rows, M_WIDTH), F32),
                        pltpu.VMEM((rows, M_GROUPS * M_DSTATE), F32),
                        pltpu.VMEM((rows, M_GROUPS * M_DSTATE), BF16),
                        pltpu.VMEM((rows, M_WIDTH), F32),
                        pltpu.VMEM((rows, M_WIDTH), F32),
                        pltpu.VMEM((rows, M_WIDTH), F32),
                        pltpu.VMEM((rows, M_WIDTH), F32)],
        compiler_params=pltpu.CompilerParams(
            dimension_semantics=("arbitrary",), vmem_limit_bytes=VMEM_LIMIT),
        name="sample_ssd",
    )(proj, proj, hist, s0, *params)


def _out_mlp_kernel(x_ref, mh_ref, mm_ref, wo_ref, ln2_ref, wu_ref, wd_ref, lnf_ref, o_ref, *, ff_tile):
    x1 = x_ref[...] + _dot(mh_ref[...], wo_ref[0:HG_WIDTH, :]) + _dot(mm_ref[...], wo_ref[HG_WIDTH:, :])
    hn = _rms(x1, ln2_ref[...]).astype(BF16)
    mlp = None
    for j in range(D_FF // ff_tile):
        u = jnp.maximum(_dot(hn, wu_ref[:, j * ff_tile:(j + 1) * ff_tile]), 0.0)
        d = _dot((u * u).astype(BF16), wd_ref[j * ff_tile:(j + 1) * ff_tile, :])
        mlp = d if mlp is None else mlp + d
    o_ref[...] = _rms(x1 + mlp, lnf_ref[...])


def _out_mlp(x, mix_h, mix_m, w_out, ln2, w_up, w_down, ln_f, *, tm=512, ff_tile=1024):
    rows = x.shape[0]
    kern = functools.partial(_out_mlp_kernel, ff_tile=ff_tile)
    row_spec = lambda w: pl.BlockSpec((tm, w), lambda i: (i, 0))
    return pl.pallas_call(
        kern,
        grid=(rows // tm,),
        in_specs=[row_spec(D_MODEL), row_spec(HG_WIDTH), row_spec(M_WIDTH),
                  _const_spec(w_out.shape), _const_spec(ln2.shape), _const_spec(w_up.shape),
                  _const_spec(w_down.shape), _const_spec(ln_f.shape)],
        out_specs=row_spec(D_MODEL),
        out_shape=jax.ShapeDtypeStruct((rows, D_MODEL), F32),
        compiler_params=pltpu.CompilerParams(
            dimension_semantics=("arbitrary",), vmem_limit_bytes=VMEM_LIMIT),
        name="out_mlp",
    )(x, mix_h, mix_m, w_out, ln2, w_up, w_down, ln_f)


def _pad_lanes(v):
    return jnp.pad(v, ((0, 0), (0, LANE - v.shape[1])))


def kernel(x_prompt, x_sample, state_hgrn, state_ssm, state_conv, hg_lb_logits, ln1, w_in, hg_norm, conv_w,
           conv_b, dt_bias, a_log, d_skip, m_norm, w_out, ln2, w_up, w_down, ln_f):
    l = 0
    bp, seq, _ = x_prompt.shape
    bs, dseq, _ = x_sample.shape
    assert DEPTH == 1 and dseq == SEG

    w_all = _pack_in_proj(w_in[l])
    lbl = hg_lb_logits.astype(F32)
    ln1_r = ln1[l][None, :]
    hgn_r = hg_norm[l].reshape(1, HG_WIDTH)
    cw = conv_w[l]
    cb_r = conv_b[l][None, :]
    dtb_r = _pad_lanes(dt_bias[l][None, :])
    alog_r = _pad_lanes(a_log[l][None, :])
    dsk_r = jnp.repeat(d_skip[l], M_HEADDIM)[None, :]
    mn_r = m_norm[l][None, :]
    mlp_w = (w_out[l].astype(BF16), ln2[l][None, :], w_up[l].astype(BF16), w_down[l].astype(BF16), ln_f[None, :])

    mixh_p, mixm_p, hgs_p, ssm_p, conv_p = _prompt_mixer(
        x_prompt, lbl, ln1_r, w_all, hgn_r, cw, cb_r, dtb_r, alog_r, dsk_r, mn_r)
    y_p = _out_mlp(x_prompt.reshape(bp * seq, D_MODEL), mixh_p.reshape(bp * seq, HG_WIDTH),
                   mixm_p.reshape(bp * seq, M_WIDTH), *mlp_w)

    xs2 = x_sample.reshape(bs * SEG, D_MODEL)
    proj_s = _sample_in_proj(xs2, ln1_r, w_all)
    mixh_s, hgs_s = _sample_hgrn(proj_s, state_hgrn[l], lbl, hgn_r)
    hist = jnp.pad(state_conv[l], ((0, 0), (SEG - (M_CONV - 1), 0), (0, 0))).reshape(bs * SEG, M_CONV_DIM)
    mixm_s, ssm_s = _sample_ssd(proj_s, hist, state_ssm[l].reshape(bs, M_WIDTH, M_DSTATE),
                                cw, cb_r, dtb_r, alog_r, dsk_r, mn_r)
    y_s = _out_mlp(xs2, mixh_s, mixm_s, *mlp_w)
    conv_s = proj_s.reshape(bs, SEG, W_IN_COLS)[:, SEG - (M_CONV - 1):, OFF_XBC:OFF_DT]

    return (y_p.reshape(bp, seq, D_MODEL), y_s.reshape(bs, SEG, D_MODEL),
            hgs_p[None], hgs_s[None],
            ssm_p.reshape(1, bp, M_HEADS, M_HEADDIM, M_DSTATE), ssm_s.reshape(1, bs, M_HEADS, M_HEADDIM, M_DSTATE),
            conv_p[None], conv_s[None])
```

```python
import functools

import jax
import jax.numpy as jnp
from jax import lax
from jax.experimental import pallas as pl
from jax.experimental.pallas import tpu as pltpu

F32 = jnp.float32
BF16 = jnp.bfloat16

D_MODEL = 1024
DEPTH = 1
HG_HEADS = 8
HG_DK = 128
HG_DV = 128
HG_WIDTH = HG_HEADS * HG_DV
M_WIDTH = 1024
M_HEADDIM = 64
M_HEADS = M_WIDTH // M_HEADDIM
M_DSTATE = 128
M_GROUPS = 2
M_GROUP_WIDTH = M_WIDTH // M_GROUPS
M_CONV = 4
M_CONV_DIM = M_WIDTH + 2 * M_GROUPS * M_DSTATE
D_FF = 4 * D_MODEL
NORM_EPS = 1e-5

LANE = 128
SUBLANE = 8

OFF_Q, OFF_F, OFF_I, OFF_G = 0, 1024, 2048, 3072
HG_PROJ_WIDTH = 4096
OFF_XBC = HG_PROJ_WIDTH
OFF_DT = OFF_XBC + M_CONV_DIM
XD_WIDTH = 2048
OFF_Z = OFF_XBC + XD_WIDTH
W_IN_COLS = OFF_Z + M_WIDTH
PACK_TILE = 1024

VMEM_LIMIT = 56 * 1024 * 1024

NEG_BIG = -1e30


def _dot(a, b):
    return jnp.dot(a, b, preferred_element_type=F32)


def _dot_nt(a, b):
    return lax.dot_general(a, b, (((1,), (1,)), ((), ())), preferred_element_type=F32)


def _dot_tn(a, b):
    return lax.dot_general(a, b, (((0,), (0,)), ((), ())), preferred_element_type=F32)


def _sigmoid(x):
    return 1.0 / (1.0 + jnp.exp(-x))


def _silu(x):
    return x * _sigmoid(x)


def _softplus(x):
    return jnp.maximum(x, 0.0) + jnp.log(1.0 + jnp.exp(-jnp.abs(x)))


def _rms(x, gain):
    ms = jnp.mean(x * x, axis=-1, keepdims=True)
    return x * lax.rsqrt(ms + NORM_EPS) * gain


def _lower_bound(lbl, layer):
    rows = [lbl[i:i + 1, :] for i in range(DEPTH + 1)]
    m = functools.reduce(jnp.maximum, rows)
    es = [jnp.exp(r - m) for r in rows]
    return sum(es[:layer + 1]) / sum(es)


def _cumsum_rows(x, seg):
    t = lax.broadcasted_iota(jnp.int32, x.shape, 0) & (seg - 1)
    s = 1
    while s < seg:
        x = x + jnp.where(t >= s, pltpu.roll(x, s, 0), 0.0)
        s *= 2
    return x


def _split3(x):
    h = x.astype(BF16)
    r = x - h.astype(F32)
    m = r.astype(BF16)
    l = (r - m.astype(F32)).astype(BF16)
    return h, m, l


def _lane_bcast(x, lane):
    return jnp.broadcast_to(x[:, lane:lane + 1], x.shape)


def _pack_in_proj_kernel(wt_ref, o_ref, *, tn, n_hg, n_xd, src_cols):
    j = pl.program_id(0)
    col0 = _pack_src_tile(j, n_hg, n_xd) * tn
    col = col0 + lax.broadcasted_iota(jnp.int32, wt_ref.shape, 0)
    o_ref[...] = jnp.where(col < src_cols, wt_ref[...], 0.0).T.astype(BF16)


def _pack_src_tile(j, n_hg, n_xd):
    n_z = M_WIDTH // PACK_TILE
    return jnp.where(j < n_hg, j, jnp.where(j < n_hg + n_xd, j + n_z, j - n_xd))


def _pack_in_proj(wt):
    tn = PACK_TILE
    n_hg, n_xd = HG_PROJ_WIDTH // tn, XD_WIDTH // tn
    kern = functools.partial(_pack_in_proj_kernel, tn=tn, n_hg=n_hg, n_xd=n_xd, src_cols=wt.shape[0])
    return pl.pallas_call(
        kern,
        grid=(W_IN_COLS // tn,),
        in_specs=[pl.BlockSpec((tn, D_MODEL), lambda j: (_pack_src_tile(j, n_hg, n_xd), 0))],
        out_specs=pl.BlockSpec((D_MODEL, tn), lambda j: (0, j)),
        out_shape=jax.ShapeDtypeStruct((D_MODEL, W_IN_COLS), BF16),
        compiler_params=pltpu.CompilerParams(
            dimension_semantics=("arbitrary",), vmem_limit_bytes=VMEM_LIMIT),
        name="pack_in_proj",
    )(wt)


def _in_proj(x, ln1_ref, w_ref, bufs, tb):
    h_scr, hg_scr, z_scr, xbc_scr, dt_scr = bufs
    h_scr[...] = _rms(x, ln1_ref[...]).astype(BF16)
    for n0 in range(0, HG_PROJ_WIDTH, 1024):
        hg_scr[:, n0:n0 + 1024] = _dot(h_scr[...], w_ref[:, n0:n0 + 1024])
    xbc_scr[SUBLANE:SUBLANE + tb, :] = _dot(h_scr[...], w_ref[:, OFF_XBC:OFF_DT])
    dt_scr[...] = _dot(h_scr[...], w_ref[:, OFF_DT:OFF_DT + LANE])
    z_scr[...] = _dot(h_scr[...], w_ref[:, OFF_Z:OFF_Z + M_WIDTH])


def _recurrences(bufs, prm, mixh_ref, mixm_ref, work, state, tb, ch, cs):
    _, hg_scr, z_scr, xbc_scr, dt_scr = bufs
    lbl_ref, hgn_ref, cw_ref, cb_ref, dtb_ref, alog_ref, dsk_ref, mn_ref = prm
    xs_scr, bc_scr, y_scr = work
    sth_scr, stm_scr, hist_scr = state

    xbc_scr[0:SUBLANE, :] = hist_scr[...]

    lb = _lower_bound(lbl_ref[...], 0)
    tril_h = (lax.broadcasted_iota(jnp.int32, (ch, ch), 0) >= lax.broadcasted_iota(jnp.int32, (ch, ch), 1))
    for j in range(tb // ch):
        r = slice(j * ch, (j + 1) * ch)
        for hd in range(HG_HEADS):
            cl = slice(hd * LANE, (hd + 1) * LANE)
            q = hg_scr[r, OFF_Q + hd * LANE:OFF_Q + (hd + 1) * LANE]
            fr = hg_scr[r, OFF_F + hd * LANE:OFF_F + (hd + 1) * LANE]
            v = hg_scr[r, OFF_I + hd * LANE:OFF_I + (hd + 1) * LANE]
            g = hg_scr[r, OFF_G + hd * LANE:OFF_G + (hd + 1) * LANE]
            lbh = lb[:, cl]
            f = lbh + (1.0 - lbh) * _sigmoid(fr)
            a = jnp.log(f)
            k = 1.0 - f
            cum = _cumsum_rows(a, ch)
            mid = cum[ch // 2 - 1:ch // 2, :]
            last = cum[ch - 1:ch, :]
            qd_mid = q * jnp.exp(cum - mid)
            kd_mid = k * jnp.exp(mid - cum)
            qd = (qd_mid * jnp.exp(mid)).astype(BF16)
            k_end = (kd_mid * jnp.exp(last - mid)).astype(BF16)
            dec = jnp.exp(last)
            vb = v.astype(BF16)
            att = _dot_nt(qd_mid.astype(BF16), kd_mid.astype(BF16))
            att = jnp.where(tril_h, att, 0.0).astype(BF16)
            st = sth_scr[hd]
            o = _dot(att, vb) + _dot_nt(qd, st.astype(BF16))
            sth_scr[hd] = dec * st + _dot_tn(vb, k_end)
            on = _rms(o, hgn_ref[:, cl]) * _silu(g)
            mixh_ref[0, r, cl] = on.astype(BF16)

    for cbk in range(M_CONV_DIM // LANE):
        cl = slice(cbk * LANE, (cbk + 1) * LANE)
        acc = cb_ref[:, cl] + cw_ref[M_CONV - 1:M_CONV, cl] * xbc_scr[SUBLANE:SUBLANE + tb, cl]
        for d in range(1, M_CONV):
            acc = acc + cw_ref[M_CONV - 1 - d:M_CONV - d, cl] * xbc_scr[SUBLANE - d:SUBLANE - d + tb, cl]
        act = _silu(acc)
        if cbk < M_WIDTH // LANE:
            xs_scr[:, cl] = act
        else:
            bc_scr[:, (cbk * LANE - M_WIDTH):(cbk * LANE - M_WIDTH) + LANE] = act.astype(BF16)
    hist_scr[...] = xbc_scr[tb:tb + SUBLANE, :]

    a_row = -jnp.exp(alog_ref[...])
    tril_m = (lax.broadcasted_iota(jnp.int32, (cs, cs), 0) >= lax.broadcasted_iota(jnp.int32, (cs, cs), 1))
    lane_lo = lax.broadcasted_iota(jnp.int32, (cs, LANE), 1) < M_HEADDIM
    for s in range(tb // cs):
        r = slice(s * cs, (s + 1) * cs)
        dtv = _softplus(dt_scr[r, :] + dtb_ref[...])
        cum = _cumsum_rows(dtv * a_row, cs)
        cum_t = cum.T
        cbs = {}
        for p in range(M_HEADS // 2):
            grp = p // (M_HEADS // 2 // M_GROUPS)
            cl = slice(p * LANE, (p + 1) * LANE)
            bm = bc_scr[r, grp * M_DSTATE:(grp + 1) * M_DSTATE]
            cm = bc_scr[r, (M_GROUPS + grp) * M_DSTATE:(M_GROUPS + grp + 1) * M_DSTATE]
            if grp not in cbs:
                cbs[grp] = _dot_nt(cm, bm)
            cb = cbs[grp]
            cum_h = [_lane_bcast(cum, hh) for hh in (2 * p, 2 * p + 1)]
            cum_x = jnp.where(lane_lo, cum_h[0], cum_h[1])
            dt_x = jnp.where(lane_lo, _lane_bcast(dtv, 2 * p), _lane_bcast(dtv, 2 * p + 1))
            last_x = cum_x[cs - 1:cs, :]
            xs = xs_scr[r, cl]
            xdt = xs * dt_x
            ms = []
            for i, hh in enumerate((2 * p, 2 * p + 1)):
                seg = cum_h[i] - cum_t[hh:hh + 1, :]
                ms.append((cb * jnp.exp(jnp.where(tril_m, seg, NEG_BIG))).astype(BF16))
            x_lo = jnp.where(lane_lo, xdt, 0.0).astype(BF16)
            x_hi = jnp.where(lane_lo, 0.0, xdt).astype(BF16)
            y = _dot(jnp.concatenate(ms, axis=1), jnp.concatenate([x_lo, x_hi], axis=0))
            st = stm_scr[:, cl]
            y = y + _dot(cm, st.astype(BF16)) * jnp.exp(cum_x)
            y = y + dsk_ref[:, cl] * xs
            to_end = (xdt * jnp.exp(last_x - cum_x)).astype(BF16)
            stm_scr[:, cl] = jnp.exp(last_x) * st + _dot_tn(bm, to_end)
            y_scr[r, cl] = y * _silu(z_scr[r, cl])
        for grp in range(M_GROUPS):
            gl = slice(grp * M_GROUP_WIDTH, (grp + 1) * M_GROUP_WIDTH)
            mixm_ref[0, r, gl] = _rms(y_scr[r, gl], mn_ref[:, gl]).astype(BF16)


def _prompt_mixer_kernel(x_ref, lbl_ref, ln1_ref, w_ref, hgn_ref, cw_ref, cb_ref, dtb_ref, alog_ref, dsk_ref, mn_ref,
                         mixh_ref, mixm_ref, hgs_ref, ssm_ref, conv_ref,
                         h_scr, hg_scr, z_scr, xbc_scr, dt_scr, xs_scr, bc_scr, y_scr, sth_scr, stm_scr, hist_scr,
                         *, tb, ch, cs):
    c = pl.program_id(1)
    last_c = pl.num_programs(1) - 1

    @pl.when(c == 0)
    def _():
        sth_scr[...] = jnp.zeros(sth_scr.shape, F32)
        stm_scr[...] = jnp.zeros(stm_scr.shape, F32)
        hist_scr[...] = jnp.zeros(hist_scr.shape, F32)

    bufs = (h_scr, hg_scr, z_scr, xbc_scr, dt_scr)
    _in_proj(x_ref[0], ln1_ref, w_ref, bufs, tb)
    prm = (lbl_ref, hgn_ref, cw_ref, cb_ref, dtb_ref, alog_ref, dsk_ref, mn_ref)
    _recurrences(bufs, prm, mixh_ref, mixm_ref, (xs_scr, bc_scr, y_scr), (sth_scr, stm_scr, hist_scr), tb, ch, cs)

    @pl.when(c == last_c)
    def _():
        for hd in range(HG_HEADS):
            hgs_ref[0, hd] = sth_scr[hd].T
        for p in range(M_WIDTH // LANE):
            ssm_ref[0, p * LANE:(p + 1) * LANE, :] = stm_scr[:, p * LANE:(p + 1) * LANE].T
        conv_ref[0] = hist_scr[SUBLANE - (M_CONV - 1):SUBLANE, :]


def _const_spec(shape):
    nd = len(shape)
    return pl.BlockSpec(shape, lambda *_: (0,) * nd, pipeline_mode=pl.Buffered(1))


def _prompt_mixer(x, lbl, ln1, w_all, hgn, cw, cb, dtb, alog, dsk, mn, *, tb=256, ch=64, cs=128):
    bsz, seq, _ = x.shape
    nc = seq // tb
    kern = functools.partial(_prompt_mixer_kernel, tb=tb, ch=ch, cs=cs)
    params = (lbl, ln1, w_all, hgn, cw, cb, dtb, alog, dsk, mn)
    return pl.pallas_call(
        kern,
        grid=(bsz, nc),
        in_specs=[pl.BlockSpec((1, tb, D_MODEL), lambda b, c: (b, c, 0))] + [_const_spec(p.shape) for p in params],
        out_specs=[
            pl.BlockSpec((1, tb, HG_WIDTH), lambda b, c: (b, c, 0)),
            pl.BlockSpec((1, tb, M_WIDTH), lambda b, c: (b, c, 0)),
            pl.BlockSpec((1, HG_HEADS, HG_DK, HG_DV), lambda b, c: (b, 0, 0, 0)),
            pl.BlockSpec((1, M_WIDTH, M_DSTATE), lambda b, c: (b, 0, 0)),
            pl.BlockSpec((1, M_CONV - 1, M_CONV_DIM), lambda b, c: (b, 0, 0)),
        ],
        out_shape=[
            jax.ShapeDtypeStruct((bsz, seq, HG_WIDTH), BF16),
            jax.ShapeDtypeStruct((bsz, seq, M_WIDTH), BF16),
            jax.ShapeDtypeStruct((bsz, HG_HEADS, HG_DK, HG_DV), F32),
            jax.ShapeDtypeStruct((bsz, M_WIDTH, M_DSTATE), F32),
            jax.ShapeDtypeStruct((bsz, M_CONV - 1, M_CONV_DIM), F32),
        ],
        scratch_shapes=[
            pltpu.VMEM((tb, D_MODEL), BF16),
            pltpu.VMEM((tb, HG_PROJ_WIDTH), F32),
            pltpu.VMEM((tb, M_WIDTH), F32),
            pltpu.VMEM((tb + SUBLANE, M_CONV_DIM), F32),
            pltpu.VMEM((tb, LANE), F32),
            pltpu.VMEM((tb, M_WIDTH), F32),
            pltpu.VMEM((tb, 2 * M_GROUPS * M_DSTATE), BF16),
            pltpu.VMEM((tb, M_WIDTH), F32),
            pltpu.VMEM((HG_HEADS, HG_DV, HG_DK), F32),
            pltpu.VMEM((M_DSTATE, M_WIDTH), F32),
            pltpu.VMEM((SUBLANE, M_CONV_DIM), F32),
        ],
        compiler_params=pltpu.CompilerParams(
            dimension_semantics=("arbitrary", "arbitrary"), vmem_limit_bytes=VMEM_LIMIT),
        name="prompt_mixer",
    )(x, *params)


def _sample_in_proj_kernel(x_ref, ln1_ref, w_ref, o_ref, h_scr):
    @pl.when(pl.program_id(0) == 0)
    def _():
        h_scr[...] = _rms(x_ref[...], ln1_ref[...]).astype(BF16)

    o_ref[...] = _dot(h_scr[...], w_ref[...])


def _sample_in_proj(x, ln1, w_all, *, tn=1024):
    rows = x.shape[0]
    return pl.pallas_call(
        _sample_in_proj_kernel,
        grid=(W_IN_COLS // tn,),
        in_specs=[_const_spec((rows, D_MODEL)), _const_spec(ln1.shape),
                  pl.BlockSpec((D_MODEL, tn), lambda n: (0, n))],
        out_specs=pl.BlockSpec((rows, tn), lambda n: (0, n)),
        out_shape=jax.ShapeDtypeStruct((rows, W_IN_COLS), F32),
        scratch_shapes=[pltpu.VMEM((rows, D_MODEL), BF16)],
        compiler_params=pltpu.CompilerParams(
            dimension_semantics=("arbitrary",), vmem_limit_bytes=VMEM_LIMIT),
        name="sample_in_proj",
    )(x, ln1, w_all)


SEG = 8


def _block_causal_mask(rows):
    ri = lax.broadcasted_iota(jnp.int32, (rows, rows), 0)
    ci = lax.broadcasted_iota(jnp.int32, (rows, rows), 1)
    return ((ri & -SEG) == (ci & -SEG)) & (ci <= ri)


def _seg_last(x):
    rows, width = x.shape
    x3 = x.reshape(rows // SEG, SEG, width)
    return jnp.broadcast_to(x3[:, SEG - 1:SEG, :], x3.shape).reshape(rows, width)


def _decay_tail(dec8):
    h, m, l = _split3(dec8)
    t = lax.broadcasted_iota(jnp.int32, dec8.shape, 0)
    return jnp.where(t == 0, h.astype(F32), jnp.where(t == 1, m.astype(F32), jnp.where(t == 2, l.astype(F32), 0.0)))


def _ones_tail():
    t = lax.broadcasted_iota(jnp.int32, (2 * SEG, LANE), 0)
    return jnp.where((t >= SEG) & (t < SEG + 3), 1.0, 0.0).astype(BF16)


def _sample_hgrn_kernel(p_ref, s0_ref, lbl_ref, hgn_ref, mixh_ref, s8_ref,
                        qd_scr, ke_scr, dec_scr, o_scr, *, nb):
    rows = nb * SEG
    lb = _lower_bound(lbl_ref[...], 0)
    mask = _block_causal_mask(rows)
    for hd in range(HG_HEADS):
        cl = slice(hd * LANE, (hd + 1) * LANE)
        q = p_ref[:, OFF_Q + hd * LANE:OFF_Q + (hd + 1) * LANE]
        fr = p_ref[:, OFF_F + hd * LANE:OFF_F + (hd + 1) * LANE]
        v = p_ref[:, OFF_I + hd * LANE:OFF_I + (hd + 1) * LANE]
        lbh = lb[:, cl]
        f = lbh + (1.0 - lbh) * _sigmoid(fr)
        a = jnp.log(f)
        k = 1.0 - f
        cum = _cumsum_rows(a, SEG)
        last = _seg_last(cum)
        qd = (q * jnp.exp(cum)).astype(BF16)
        kd = (k * jnp.exp(-cum)).astype(BF16)
        att = jnp.where(mask, _dot_nt(qd, kd), 0.0).astype(BF16)
        o_scr[:, cl] = _dot(att, v.astype(BF16))
        qd_scr[:, cl] = qd
        ke_scr[:, cl] = k * jnp.exp(last - cum)
        dec_scr[:, cl] = jnp.exp(last)

    ones_tail = _ones_tail()
    zeros8 = jnp.zeros((SEG, LANE), F32)

    def pair_body(m, carry):
        r16 = pl.ds(pl.multiple_of(m * 2 * SEG, 2 * SEG), 2 * SEG)
        for par in range(2):
            i = 2 * m + par
            r8 = pl.ds(pl.multiple_of(i * SEG, SEG), SEG)
            for hd in range(HG_HEADS):
                cl = slice(hd * LANE, (hd + 1) * LANE)
                s0 = s0_ref[i, hd]
                oi = _dot(qd_scr[r16, cl], s0.astype(BF16))
                o_scr[r8, cl] = o_scr[r8, cl] + oi[par * SEG:(par + 1) * SEG, :]
                aug = jnp.concatenate([ke_scr[r8, cl], _decay_tail(dec_scr[r8, cl])], axis=0).astype(BF16)
                v8 = p_ref[r8, OFF_I + hd * LANE:OFF_I + (hd + 1) * LANE]
                rhs = jnp.concatenate([jnp.concatenate([v8, zeros8], axis=0).astype(BF16), ones_tail], axis=1)
                ud = _dot_tn(aug, rhs)
                s8_ref[i, hd] = ud[:, LANE:] * s0 + ud[:, :LANE]
        return carry

    lax.fori_loop(0, nb // 2, pair_body, 0)

    for hd in range(HG_HEADS):
        cl = slice(hd * LANE, (hd + 1) * LANE)
        g = p_ref[:, OFF_G + hd * LANE:OFF_G + (hd + 1) * LANE]
        mixh_ref[:, cl] = (_rms(o_scr[:, cl], hgn_ref[:, cl]) * _silu(g)).astype(BF16)


def _sample_hgrn(proj, s0, lbl, hgn, *, nb=16):
    nseq = s0.shape[0]
    rows = nb * SEG
    kern = functools.partial(_sample_hgrn_kernel, nb=nb)
    return pl.pallas_call(
        kern,
        grid=(nseq // nb,),
        in_specs=[pl.BlockSpec((rows, OFF_XBC), lambda i: (i, 0)),
                  pl.BlockSpec((nb, HG_HEADS, HG_DK, HG_DV), lambda i: (i, 0, 0, 0)),
                  _const_spec(lbl.shape), _const_spec(hgn.shape)],
        out_specs=[pl.BlockSpec((rows, HG_WIDTH), lambda i: (i, 0)),
                   pl.BlockSpec((nb, HG_HEADS, HG_DK, HG_DV), lambda i: (i, 0, 0, 0))],
        out_shape=[jax.ShapeDtypeStruct((nseq * SEG, HG_WIDTH), BF16),
                   jax.ShapeDtypeStruct(s0.shape, F32)],
        scratch_shapes=[pltpu.VMEM((rows, HG_WIDTH), BF16),
                        pltpu.VMEM((rows, HG_WIDTH), F32),
                        pltpu.VMEM((rows, HG_WIDTH), F32),
                        pltpu.VMEM((rows, HG_WIDTH), F32)],
        compiler_params=pltpu.CompilerParams(
            dimension_semantics=("arbitrary",), vmem_limit_bytes=VMEM_LIMIT),
        name="sample_hgrn",
    )(proj, s0, lbl, hgn)


def _sample_ssd_kernel(xd_ref, z_ref, hist_ref, s0_ref, cw_ref, cb_ref, dtb_ref, alog_ref, dsk_ref, mn_ref,
                       mixm_ref, s8_ref,
                       xs_scr, bm_scr, cm_scr, u_scr, cd_scr, fs_scr, y_scr, *, nb):
    rows = nb * SEG
    t_in_seq = lax.broadcasted_iota(jnp.int32, (rows, LANE), 0) & (SEG - 1)

    for cbk in range(M_CONV_DIM // LANE):
        cl = slice(cbk * LANE, (cbk + 1) * LANE)
        x = xd_ref[:, cl]
        hist = hist_ref[:, cl]
        acc = cb_ref[:, cl] + cw_ref[M_CONV - 1:M_CONV, cl] * x
        for d in range(1, M_CONV):
            shifted = jnp.where(t_in_seq >= d, pltpu.roll(x, d, 0), pltpu.roll(hist, rows - SEG + d, 0))
            acc = acc + cw_ref[M_CONV - 1 - d:M_CONV - d, cl] * shifted
        act = _silu(acc)
        if cbk < M_WIDTH // LANE:
            xs_scr[:, cl] = act
        elif cbk < (M_WIDTH + M_GROUPS * M_DSTATE) // LANE:
            bm_scr[:, cbk * LANE - M_WIDTH:(cbk + 1) * LANE - M_WIDTH] = act
        else:
            o0 = cbk * LANE - M_WIDTH - M_GROUPS * M_DSTATE
            cm_scr[:, o0:o0 + LANE] = act.astype(BF16)

    a_row = -jnp.exp(alog_ref[...])
    dtv = _softplus(xd_ref[:, M_CONV_DIM:M_CONV_DIM + LANE] + dtb_ref[...])
    cum = _cumsum_rows(dtv * a_row, SEG)
    cum_t = cum.T
    mask = _block_causal_mask(rows)
    lane_lo = lax.broadcasted_iota(jnp.int32, (rows, LANE), 1) < M_HEADDIM
    cbs = {}
    for p in range(M_HEADS // 2):
        grp = p // (M_HEADS // 2 // M_GROUPS)
        cl = slice(p * LANE, (p + 1) * LANE)
        if grp not in cbs:
            bm = bm_scr[:, grp * M_DSTATE:(grp + 1) * M_DSTATE].astype(BF16)
            cbs[grp] = _dot_nt(cm_scr[:, grp * M_DSTATE:(grp + 1) * M_DSTATE], bm)
        cb = cbs[grp]
        cum_h = [_lane_bcast(cum, hh) for hh in (2 * p, 2 * p + 1)]
        cum_x = jnp.where(lane_lo, cum_h[0], cum_h[1])
        dt_x = jnp.where(lane_lo, _lane_bcast(dtv, 2 * p), _lane_bcast(dtv, 2 * p + 1))
        last_x = _seg_last(cum_x)
        xs = xs_scr[:, cl]
        xdt = xs * dt_x
        ms = []
        for i, hh in enumerate((2 * p, 2 * p + 1)):
            seg = cum_h[i] - cum_t[hh:hh + 1, :]
            ms.append((cb * jnp.exp(jnp.where(mask, seg, NEG_BIG))).astype(BF16))
        x_lo = jnp.where(lane_lo, xdt, 0.0).astype(BF16)
        x_hi = jnp.where(lane_lo, 0.0, xdt).astype(BF16)
        y = _dot(jnp.concatenate(ms, axis=1), jnp.concatenate([x_lo, x_hi], axis=0))
        y_scr[:, cl] = y + dsk_ref[:, cl] * xs
        u_scr[:, cl] = xdt * jnp.exp(last_x - cum_x)
        cd_scr[:, cl] = jnp.exp(last_x)
        fs_scr[:, cl] = jnp.exp(cum_x)

    ones_tail = _ones_tail()
    zeros8 = jnp.zeros((SEG, M_DSTATE), F32)

    def pair_body(m, carry):
        r16 = pl.ds(pl.multiple_of(m * 2 * SEG, 2 * SEG), 2 * SEG)
        for par in range(2):
            i = 2 * m + par
            r8 = pl.ds(pl.multiple_of(i * SEG, SEG), SEG)
            for grp in range(M_GROUPS):
                gl = slice(grp * M_GROUP_WIDTH, (grp + 1) * M_GROUP_WIDTH)
                nl = slice(grp * M_DSTATE, (grp + 1) * M_DSTATE)
                s0 = s0_ref[i, gl, :]
                yi = _dot_nt(cm_scr[r16, nl], s0.astype(BF16))
                y_scr[r8, gl] = y_scr[r8, gl] + yi[par * SEG:(par + 1) * SEG, :] * fs_scr[r8, gl]
                aug = jnp.concatenate([u_scr[r8, gl], _decay_tail(cd_scr[r8, gl])], axis=0).astype(BF16)
                rhs = jnp.concatenate(
                    [jnp.concatenate([bm_scr[r8, nl], zeros8], axis=0).astype(BF16), ones_tail], axis=1)
                ud = _dot_tn(aug, rhs)
                s8_ref[i, gl, :] = ud[:, M_DSTATE:] * s0 + ud[:, :M_DSTATE]
        return carry

    lax.fori_loop(0, nb // 2, pair_body, 0)

    for grp in range(M_GROUPS):
        gl = slice(grp * M_GROUP_WIDTH, (grp + 1) * M_GROUP_WIDTH)
        y = y_scr[:, gl] * _silu(z_ref[:, gl])
        mixm_ref[:, gl] = _rms(y, mn_ref[:, gl]).astype(BF16)


def _sample_ssd(proj, hist, s0, cw, cb, dtb, alog, dsk, mn, *, nb=16):
    nseq = s0.shape[0]
    rows = nb * SEG
    kern = functools.partial(_sample_ssd_kernel, nb=nb)
    params = (cw, cb, dtb, alog, dsk, mn)
    return pl.pallas_call(
        kern,
        grid=(nseq // nb,),
        in_specs=[pl.BlockSpec((rows, XD_WIDTH), lambda i: (i, OFF_XBC // XD_WIDTH)),
                  pl.BlockSpec((rows, M_WIDTH), lambda i: (i, OFF_Z // M_WIDTH)),
                  pl.BlockSpec((rows, M_CONV_DIM), lambda i: (i, 0)),
                  pl.BlockSpec((nb, M_WIDTH, M_DSTATE), lambda i: (i, 0, 0))]
                 + [_const_spec(p.shape) for p in params],
        out_specs=[pl.BlockSpec((rows, M_WIDTH), lambda i: (i, 0)),
                   pl.BlockSpec((nb, M_WIDTH, M_DSTATE), lambda i: (i, 0, 0))],
        out_shape=[jax.ShapeDtypeStruct((nseq * SEG, M_WIDTH), BF16),
                   jax.ShapeDtypeStruct(s0.shape, F32)],
        scratch_shapes=[pltpu.VMEM((rows, M_WIDTH), F32),
                        pltpu.VMEM((rows, M_GROUPS * M_DSTATE), F32),
                        pltpu.VMEM((rows, M_GROUPS * M_DSTATE), BF16),
                        pltpu.VMEM((rows, M_WIDTH), F32),
                        pltpu.VMEM((rows, M_WIDTH), F32),
                        pltpu.VMEM((rows, M_WIDTH), F32),
                        pltpu.VMEM((rows, M_WIDTH), F32)],
        compiler_params=pltpu.CompilerParams(
            dimension_semantics=("arbitrary",), vmem_limit_bytes=VMEM_LIMIT),
        name="sample_ssd",
    )(proj, proj, hist, s0, *params)


def _out_mlp_kernel(x_ref, mh_ref, mm_ref, wo_ref, ln2_ref, wu_ref, wd_ref, lnf_ref, o_ref, *, ff_tile):
    x1 = x_ref[...] + _dot(mh_ref[...], wo_ref[0:HG_WIDTH, :]) + _dot(mm_ref[...], wo_ref[HG_WIDTH:, :])
    hn = _rms(x1, ln2_ref[...]).astype(BF16)
    mlp = None
    for j in range(D_FF // ff_tile):
        u = jnp.maximum(_dot(hn, wu_ref[:, j * ff_tile:(j + 1) * ff_tile]), 0.0)
        d = _dot((u * u).astype(BF16), wd_ref[j * ff_tile:(j + 1) * ff_tile, :])
        mlp = d if mlp is None else mlp + d
    o_ref[...] = _rms(x1 + mlp, lnf_ref[...])


def _out_mlp(x, mix_h, mix_m, w_out, ln2, w_up, w_down, ln_f, *, tm=512, ff_tile=1024):
    rows = x.shape[0]
    kern = functools.partial(_out_mlp_kernel, ff_tile=ff_tile)
    row_spec = lambda w: pl.BlockSpec((tm, w), lambda i: (i, 0))
    return pl.pallas_call(
        kern,
        grid=(rows // tm,),
        in_specs=[row_spec(D_MODEL), row_spec(HG_WIDTH), row_spec(M_WIDTH),
                  _const_spec(w_out.shape), _const_spec(ln2.shape), _const_spec(w_up.shape),
                  _const_spec(w_down.shape), _const_spec(ln_f.shape)],
        out_specs=row_spec(D_MODEL),
        out_shape=jax.ShapeDtypeStruct((rows, D_MODEL), F32),
        compiler_params=pltpu.CompilerParams(
            dimension_semantics=("arbitrary",), vmem_limit_bytes=VMEM_LIMIT),
        name="out_mlp",
    )(x, mix_h, mix_m, w_out, ln2, w_up, w_down, ln_f)


def _pad_lanes(v):
    return jnp.pad(v, ((0, 0), (0, LANE - v.shape[1])))


def kernel(x_prompt, x_sample, state_hgrn, state_ssm, state_conv, hg_lb_logits, ln1, w_in, hg_norm, conv_w,
           conv_b, dt_bias, a_log, d_skip, m_norm, w_out, ln2, w_up, w_down, ln_f):
    l = 0
    bp, seq, _ = x_prompt.shape
    bs, dseq, _ = x_sample.shape
    assert DEPTH == 1 and dseq == SEG

    w_all = _pack_in_proj(jnp.transpose(w_in[l]))
    lbl = hg_lb_logits.astype(F32)
    ln1_r = ln1[l][None, :]
    hgn_r = hg_norm[l].reshape(1, HG_WIDTH)
    cw = conv_w[l]
    cb_r = conv_b[l][None, :]
    dtb_r = _pad_lanes(dt_bias[l][None, :])
    alog_r = _pad_lanes(a_log[l][None, :])
    dsk_r = jnp.repeat(d_skip[l], M_HEADDIM)[None, :]
    mn_r = m_norm[l][None, :]
    mlp_w = (w_out[l].astype(BF16), ln2[l][None, :], w_up[l].astype(BF16), w_down[l].astype(BF16), ln_f[None, :])

    mixh_p, mixm_p, hgs_p, ssm_p, conv_p = _prompt_mixer(
        x_prompt, lbl, ln1_r, w_all, hgn_r, cw, cb_r, dtb_r, alog_r, dsk_r, mn_r)
    y_p = _out_mlp(x_prompt.reshape(bp * seq, D_MODEL), mixh_p.reshape(bp * seq, HG_WIDTH),
                   mixm_p.reshape(bp * seq, M_WIDTH), *mlp_w)

    xs2 = x_sample.reshape(bs * SEG, D_MODEL)
    proj_s = _sample_in_proj(xs2, ln1_r, w_all)
    mixh_s, hgs_s = _sample_hgrn(proj_s, state_hgrn[l], lbl, hgn_r)
    hist = jnp.pad(state_conv[l], ((0, 0), (SEG - (M_CONV - 1), 0), (0, 0))).reshape(bs * SEG, M_CONV_DIM)
    mixm_s, ssm_s = _sample_ssd(proj_s, hist, state_ssm[l].reshape(bs, M_WIDTH, M_DSTATE),
                                cw, cb_r, dtb_r, alog_r, dsk_r, mn_r)
    y_s = _out_mlp(xs2, mixh_s, mixm_s, *mlp_w)
    conv_s = proj_s.reshape(bs, SEG, W_IN_COLS)[:, SEG - (M_CONV - 1):, OFF_XBC:OFF_DT]

    return (y_p.reshape(bp, seq, D_MODEL), y_s.reshape(bs, SEG, D_MODEL),
            hgs_p[None], hgs_s[None],
            ssm_p.reshape(1, bp, M_HEADS, M_HEADDIM, M_DSTATE), ssm_s.reshape(1, bs, M_HEADS, M_HEADDIM, M_DSTATE),
            conv_p[None], conv_s[None])
```

```python
import functools

import jax
import jax.numpy as jnp
from jax import lax
from jax.experimental import pallas as pl
from jax.experimental.pallas import tpu as pltpu

F32 = jnp.float32
BF16 = jnp.bfloat16

D_MODEL = 1024
DEPTH = 1
HG_HEADS = 8
HG_DK = 128
HG_DV = 128
HG_WIDTH = HG_HEADS * HG_DV
M_WIDTH = 1024
M_HEADDIM = 64
M_HEADS = M_WIDTH // M_HEADDIM
M_DSTATE = 128
M_GROUPS = 2
M_GROUP_WIDTH = M_WIDTH // M_GROUPS
M_CONV = 4
M_CONV_DIM = M_WIDTH + 2 * M_GROUPS * M_DSTATE
D_FF = 4 * D_MODEL
NORM_EPS = 1e-5

LANE = 128
SUBLANE = 8

OFF_Q, OFF_F, OFF_I, OFF_G = 0, 1024, 2048, 3072
HG_PROJ_WIDTH = 4096
OFF_XBC = HG_PROJ_WIDTH
OFF_DT = OFF_XBC + M_CONV_DIM
XD_WIDTH = 2048
OFF_Z = OFF_XBC + XD_WIDTH
W_IN_COLS = OFF_Z + M_WIDTH
PACK_TILE = 1024

VMEM_LIMIT = 56 * 1024 * 1024

NEG_BIG = -1e30


def _dot(a, b):
    return jnp.dot(a, b, preferred_element_type=F32)


def _dot_nt(a, b):
    return lax.dot_general(a, b, (((1,), (1,)), ((), ())), preferred_element_type=F32)


def _dot_tn(a, b):
    return lax.dot_general(a, b, (((0,), (0,)), ((), ())), preferred_element_type=F32)


def _sigmoid(x):
    return 1.0 / (1.0 + jnp.exp(-x))


def _silu(x):
    return x * _sigmoid(x)


def _softplus(x):
    return jnp.maximum(x, 0.0) + jnp.log(1.0 + jnp.exp(-jnp.abs(x)))


def _rms(x, gain):
    ms = jnp.mean(x * x, axis=-1, keepdims=True)
    return x * lax.rsqrt(ms + NORM_EPS) * gain


def _lower_bound(lbl, layer):
    rows = [lbl[i:i + 1, :] for i in range(DEPTH + 1)]
    m = functools.reduce(jnp.maximum, rows)
    es = [jnp.exp(r - m) for r in rows]
    return sum(es[:layer + 1]) / sum(es)


def _cumsum_rows(x, seg):
    t = lax.broadcasted_iota(jnp.int32, x.shape, 0) & (seg - 1)
    s = 1
    while s < seg:
        x = x + jnp.where(t >= s, pltpu.roll(x, s, 0), 0.0)
        s *= 2
    return x


def _split3(x):
    h = x.astype(BF16)
    r = x - h.astype(F32)
    m = r.astype(BF16)
    l = (r - m.astype(F32)).astype(BF16)
    return h, m, l


def _lane_bcast(x, lane):
    return jnp.broadcast_to(x[:, lane:lane + 1], x.shape)


def _pack_in_proj_kernel(wt_ref, o_ref, *, tn, n_hg, n_xd, src_cols):
    j = pl.program_id(0)
    col0 = _pack_src_tile(j, n_hg, n_xd) * tn
    col = col0 + lax.broadcasted_iota(jnp.int32, wt_ref.shape, 0)
    o_ref[...] = jnp.where(col < src_cols, wt_ref[...], 0.0).T.astype(BF16)


def _pack_src_tile(j, n_hg, n_xd):
    n_z = M_WIDTH // PACK_TILE
    return jnp.where(j < n_hg, j, jnp.where(j < n_hg + n_xd, j + n_z, j - n_xd))


def _pack_in_proj(wt):
    tn = PACK_TILE
    n_hg, n_xd = HG_PROJ_WIDTH // tn, XD_WIDTH // tn
    kern = functools.partial(_pack_in_proj_kernel, tn=tn, n_hg=n_hg, n_xd=n_xd, src_cols=wt.shape[0])
    return pl.pallas_call(
        kern,
        grid=(W_IN_COLS // tn,),
        in_specs=[pl.BlockSpec((tn, D_MODEL), lambda j: (_pack_src_tile(j, n_hg, n_xd), 0))],
        out_specs=pl.BlockSpec((D_MODEL, tn), lambda j: (0, j)),
        out_shape=jax.ShapeDtypeStruct((D_MODEL, W_IN_COLS), BF16),
        compiler_params=pltpu.CompilerParams(
            dimension_semantics=("arbitrary",), vmem_limit_bytes=VMEM_LIMIT),
        name="pack_in_proj",
    )(wt)


def _in_proj(x, ln1_ref, w_ref, bufs, tb):
    h_scr, hg_scr, z_scr, xbc_scr, dt_scr = bufs
    h_scr[...] = _rms(x, ln1_ref[...]).astype(BF16)
    for n0 in range(0, HG_PROJ_WIDTH, 1024):
        hg_scr[:, n0:n0 + 1024] = _dot(h_scr[...], w_ref[:, n0:n0 + 1024])
    xbc_scr[SUBLANE:SUBLANE + tb, :] = _dot(h_scr[...], w_ref[:, OFF_XBC:OFF_DT])
    dt_scr[...] = _dot(h_scr[...], w_ref[:, OFF_DT:OFF_DT + LANE])
    z_scr[...] = _dot(h_scr[...], w_ref[:, OFF_Z:OFF_Z + M_WIDTH])


def _recurrences(bufs, prm, mixh_ref, mixm_ref, work, state, tb, ch, cs):
    _, hg_scr, z_scr, xbc_scr, dt_scr = bufs
    lbl_ref, hgn_ref, cw_ref, cb_ref, dtb_ref, alog_ref, dsk_ref, mn_ref = prm
    xs_scr, bc_scr, y_scr = work
    sth_scr, stm_scr, hist_scr = state

    xbc_scr[0:SUBLANE, :] = hist_scr[...]

    lb = _lower_bound(lbl_ref[...], 0)
    tril_h = (lax.broadcasted_iota(jnp.int32, (ch, ch), 0) >= lax.broadcasted_iota(jnp.int32, (ch, ch), 1))
    for j in range(tb // ch):
        r = slice(j * ch, (j + 1) * ch)
        for hd in range(HG_HEADS):
            cl = slice(hd * LANE, (hd + 1) * LANE)
            q = hg_scr[r, OFF_Q + hd * LANE:OFF_Q + (hd + 1) * LANE]
            fr = hg_scr[r, OFF_F + hd * LANE:OFF_F + (hd + 1) * LANE]
            v = hg_scr[r, OFF_I + hd * LANE:OFF_I + (hd + 1) * LANE]
            g = hg_scr[r, OFF_G + hd * LANE:OFF_G + (hd + 1) * LANE]
            lbh = lb[:, cl]
            f = lbh + (1.0 - lbh) * _sigmoid(fr)
            a = jnp.log(f)
            k = 1.0 - f
            cum = _cumsum_rows(a, ch)
            mid = cum[ch // 2 - 1:ch // 2, :]
            last = cum[ch - 1:ch, :]
            qd_mid = q * jnp.exp(cum - mid)
            kd_mid = k * jnp.exp(mid - cum)
            qd = (qd_mid * jnp.exp(mid)).astype(BF16)
            k_end = (kd_mid * jnp.exp(last - mid)).astype(BF16)
            dec = jnp.exp(last)
            vb = v.astype(BF16)
            att = _dot_nt(qd_mid.astype(BF16), kd_mid.astype(BF16))
            att = jnp.where(tril_h, att, 0.0).astype(BF16)
            st = sth_scr[hd]
            o = _dot(att, vb) + _dot_nt(qd, st.astype(BF16))
            sth_scr[hd] = dec * st + _dot_tn(vb, k_end)
            on = _rms(o, hgn_ref[:, cl]) * _silu(g)
            mixh_ref[0, r, cl] = on.astype(BF16)

    for cbk in range(M_CONV_DIM // LANE):
        cl = slice(cbk * LANE, (cbk + 1) * LANE)
        acc = cb_ref[:, cl] + cw_ref[M_CONV - 1:M_CONV, cl] * xbc_scr[SUBLANE:SUBLANE + tb, cl]
        for d in range(1, M_CONV):
            acc = acc + cw_ref[M_CONV - 1 - d:M_CONV - d, cl] * xbc_scr[SUBLANE - d:SUBLANE - d + tb, cl]
        act = _silu(acc)
        if cbk < M_WIDTH // LANE:
            xs_scr[:, cl] = act
        else:
            bc_scr[:, (cbk * LANE - M_WIDTH):(cbk * LANE - M_WIDTH) + LANE] = act.astype(BF16)
    hist_scr[...] = xbc_scr[tb:tb + SUBLANE, :]

    a_row = -jnp.exp(alog_ref[...])
    tril_m = (lax.broadcasted_iota(jnp.int32, (cs, cs), 0) >= lax.broadcasted_iota(jnp.int32, (cs, cs), 1))
    lane_lo = lax.broadcasted_iota(jnp.int32, (cs, LANE), 1) < M_HEADDIM
    for s in range(tb // cs):
        r = slice(s * cs, (s + 1) * cs)
        dtv = _softplus(dt_scr[r, :] + dtb_ref[...])
        cum = _cumsum_rows(dtv * a_row, cs)
        cum_t = cum.T
        cbs = {}
        for p in range(M_HEADS // 2):
            grp = p // (M_HEADS // 2 // M_GROUPS)
            cl = slice(p * LANE, (p + 1) * LANE)
            bm = bc_scr[r, grp * M_DSTATE:(grp + 1) * M_DSTATE]
            cm = bc_scr[r, (M_GROUPS + grp) * M_DSTATE:(M_GROUPS + grp + 1) * M_DSTATE]
            if grp not in cbs:
                cbs[grp] = _dot_nt(cm, bm)
            cb = cbs[grp]
            cum_h = [_lane_bcast(cum, hh) for hh in (2 * p, 2 * p + 1)]
            cum_x = jnp.where(lane_lo, cum_h[0], cum_h[1])
            dt_x = jnp.where(lane_lo, _lane_bcast(dtv, 2 * p), _lane_bcast(dtv, 2 * p + 1))
            last_x = cum_x[cs - 1:cs, :]
            xs = xs_scr[r, cl]
            xdt = xs * dt_x
            ms = []
            for i, hh in enumerate((2 * p, 2 * p + 1)):
                seg = cum_h[i] - cum_t[hh:hh + 1, :]
                ms.append((cb * jnp.exp(jnp.where(tril_m, seg, NEG_BIG))).astype(BF16))
            x_lo = jnp.where(lane_lo, xdt, 0.0).astype(BF16)
            x_hi = jnp.where(lane_lo, 0.0, xdt).astype(BF16)
            y = _dot(jnp.concatenate(ms, axis=1), jnp.concatenate([x_lo, x_hi], axis=0))
            st = stm_scr[:, cl]
            y = y + _dot(cm, st.astype(BF16)) * jnp.exp(cum_x)
            y = y + dsk_ref[:, cl] * xs
            to_end = (xdt * jnp.exp(last_x - cum_x)).astype(BF16)
            stm_scr[:, cl] = jnp.exp(last_x) * st + _dot_tn(bm, to_end)
            y_scr[r, cl] = y * _silu(z_scr[r, cl])
        for grp in range(M_GROUPS):
            gl = slice(grp * M_GROUP_WIDTH, (grp + 1) * M_GROUP_WIDTH)
            mixm_ref[0, r, gl] = _rms(y_scr[r, gl], mn_ref[:, gl]).astype(BF16)


def _prompt_mixer_kernel(x_ref, lbl_ref, ln1_ref, w_ref, hgn_ref, cw_ref, cb_ref, dtb_ref, alog_ref, dsk_ref, mn_ref,
                         mixh_ref, mixm_ref, hgs_ref, ssm_ref, conv_ref,
                         h_scr, hg_scr, z_scr, xbc_scr, dt_scr, xs_scr, bc_scr, y_scr, sth_scr, stm_scr, hist_scr,
                         *, tb, ch, cs):
    c = pl.program_id(1)
    last_c = pl.num_programs(1) - 1

    @pl.when(c == 0)
    def _():
        sth_scr[...] = jnp.zeros(sth_scr.shape, F32)
        stm_scr[...] = jnp.zeros(stm_scr.shape, F32)
        hist_scr[...] = jnp.zeros(hist_scr.shape, F32)

    bufs = (h_scr, hg_scr, z_scr, xbc_scr, dt_scr)
    _in_proj(x_ref[0], ln1_ref, w_ref, bufs, tb)
    prm = (lbl_ref, hgn_ref, cw_ref, cb_ref, dtb_ref, alog_ref, dsk_ref, mn_ref)
    _recurrences(bufs, prm, mixh_ref, mixm_ref, (xs_scr, bc_scr, y_scr), (sth_scr, stm_scr, hist_scr), tb, ch, cs)

    @pl.when(c == last_c)
    def _():
        for hd in range(HG_HEADS):
            hgs_ref[0, hd] = sth_scr[hd].T
        for p in range(M_WIDTH // LANE):
            ssm_ref[0, p * LANE:(p + 1) * LANE, :] = stm_scr[:, p * LANE:(p + 1) * LANE].T
        conv_ref[0] = hist_scr[SUBLANE - (M_CONV - 1):SUBLANE, :]


def _const_spec(shape):
    nd = len(shape)
    return pl.BlockSpec(shape, lambda *_: (0,) * nd, pipeline_mode=pl.Buffered(1))


def _prompt_mixer(x, lbl, ln1, w_all, hgn, cw, cb, dtb, alog, dsk, mn, *, tb=512, ch=64, cs=128):
    bsz, seq, _ = x.shape
    nc = seq // tb
    kern = functools.partial(_prompt_mixer_kernel, tb=tb, ch=ch, cs=cs)
    params = (lbl, ln1, w_all, hgn, cw, cb, dtb, alog, dsk, mn)
    return pl.pallas_call(
        kern,
        grid=(bsz, nc),
        in_specs=[pl.BlockSpec((1, tb, D_MODEL), lambda b, c: (b, c, 0))] + [_const_spec(p.shape) for p in params],
        out_specs=[
            pl.BlockSpec((1, tb, HG_WIDTH), lambda b, c: (b, c, 0)),
            pl.BlockSpec((1, tb, M_WIDTH), lambda b, c: (b, c, 0)),
            pl.BlockSpec((1, HG_HEADS, HG_DK, HG_DV), lambda b, c: (b, 0, 0, 0)),
            pl.BlockSpec((1, M_WIDTH, M_DSTATE), lambda b, c: (b, 0, 0)),
            pl.BlockSpec((1, M_CONV - 1, M_CONV_DIM), lambda b, c: (b, 0, 0)),
        ],
        out_shape=[
            jax.ShapeDtypeStruct((bsz, seq, HG_WIDTH), BF16),
            jax.ShapeDtypeStruct((bsz, seq, M_WIDTH), BF16),
            jax.ShapeDtypeStruct((bsz, HG_HEADS, HG_DK, HG_DV), F32),
            jax.ShapeDtypeStruct((bsz, M_WIDTH, M_DSTATE), F32),
            jax.ShapeDtypeStruct((bsz, M_CONV - 1, M_CONV_DIM), F32),
        ],
        scratch_shapes=[
            pltpu.VMEM((tb, D_MODEL), BF16),
            pltpu.VMEM((tb, HG_PROJ_WIDTH), F32),
            pltpu.VMEM((tb, M_WIDTH), F32),
            pltpu.VMEM((tb + SUBLANE, M_CONV_DIM), F32),
            pltpu.VMEM((tb, LANE), F32),
            pltpu.VMEM((tb, M_WIDTH), F32),
            pltpu.VMEM((tb, 2 * M_GROUPS * M_DSTATE), BF16),
            pltpu.VMEM((tb, M_WIDTH), F32),
            pltpu.VMEM((HG_HEADS, HG_DV, HG_DK), F32),
            pltpu.VMEM((M_DSTATE, M_WIDTH), F32),
            pltpu.VMEM((SUBLANE, M_CONV_DIM), F32),
        ],
        compiler_params=pltpu.CompilerParams(
            dimension_semantics=("arbitrary", "arbitrary"), vmem_limit_bytes=VMEM_LIMIT),
        name="prompt_mixer",
    )(x, *params)


def _sample_in_proj_kernel(x_ref, ln1_ref, w_ref, o_ref, h_scr):
    @pl.when(pl.program_id(0) == 0)
    def _():
        h_scr[...] = _rms(x_ref[...], ln1_ref[...]).astype(BF16)

    o_ref[...] = _dot(h_scr[...], w_ref[...])


def _sample_in_proj(x, ln1, w_all, *, tn=1024):
    rows = x.shape[0]
    return pl.pallas_call(
        _sample_in_proj_kernel,
        grid=(W_IN_COLS // tn,),
        in_specs=[_const_spec((rows, D_MODEL)), _const_spec(ln1.shape),
                  pl.BlockSpec((D_MODEL, tn), lambda n: (0, n))],
        out_specs=pl.BlockSpec((rows, tn), lambda n: (0, n)),
        out_shape=jax.ShapeDtypeStruct((rows, W_IN_COLS), F32),
        scratch_shapes=[pltpu.VMEM((rows, D_MODEL), BF16)],
        compiler_params=pltpu.CompilerParams(
            dimension_semantics=("arbitrary",), vmem_limit_bytes=VMEM_LIMIT),
        name="sample_in_proj",
    )(x, ln1, w_all)


SEG = 8


def _block_causal_mask(rows):
    ri = lax.broadcasted_iota(jnp.int32, (rows, rows), 0)
    ci = lax.broadcasted_iota(jnp.int32, (rows, rows), 1)
    return ((ri & -SEG) == (ci & -SEG)) & (ci <= ri)


def _seg_last(x):
    rows, width = x.shape
    x3 = x.reshape(rows // SEG, SEG, width)
    return jnp.broadcast_to(x3[:, SEG - 1:SEG, :], x3.shape).reshape(rows, width)


def _decay_tail(dec8):
    h, m, l = _split3(dec8)
    t = lax.broadcasted_iota(jnp.int32, dec8.shape, 0)
    return jnp.where(t == 0, h.astype(F32), jnp.where(t == 1, m.astype(F32), jnp.where(t == 2, l.astype(F32), 0.0)))


def _ones_tail():
    t = lax.broadcasted_iota(jnp.int32, (2 * SEG, LANE), 0)
    return jnp.where((t >= SEG) & (t < SEG + 3), 1.0, 0.0).astype(BF16)


def _sample_hgrn_kernel(p_ref, s0_ref, lbl_ref, hgn_ref, mixh_ref, s8_ref,
                        qd_scr, ke_scr, dec_scr, o_scr, *, nb):
    rows = nb * SEG
    lb = _lower_bound(lbl_ref[...], 0)
    mask = _block_causal_mask(rows)
    for hd in range(HG_HEADS):
        cl = slice(hd * LANE, (hd + 1) * LANE)
        q = p_ref[:, OFF_Q + hd * LANE:OFF_Q + (hd + 1) * LANE]
        fr = p_ref[:, OFF_F + hd * LANE:OFF_F + (hd + 1) * LANE]
        v = p_ref[:, OFF_I + hd * LANE:OFF_I + (hd + 1) * LANE]
        lbh = lb[:, cl]
        f = lbh + (1.0 - lbh) * _sigmoid(fr)
        a = jnp.log(f)
        k = 1.0 - f
        cum = _cumsum_rows(a, SEG)
        last = _seg_last(cum)
        qd = (q * jnp.exp(cum)).astype(BF16)
        kd = (k * jnp.exp(-cum)).astype(BF16)
        att = jnp.where(mask, _dot_nt(qd, kd), 0.0).astype(BF16)
        o_scr[:, cl] = _dot(att, v.astype(BF16))
        qd_scr[:, cl] = qd
        ke_scr[:, cl] = k * jnp.exp(last - cum)
        dec_scr[:, cl] = jnp.exp(last)

    ones_tail = _ones_tail()
    zeros8 = jnp.zeros((SEG, LANE), F32)

    def pair_body(m, carry):
        r16 = pl.ds(pl.multiple_of(m * 2 * SEG, 2 * SEG), 2 * SEG)
        for par in range(2):
            i = 2 * m + par
            r8 = pl.ds(pl.multiple_of(i * SEG, SEG), SEG)
            for hd in range(HG_HEADS):
                cl = slice(hd * LANE, (hd + 1) * LANE)
                s0 = s0_ref[i, hd]
                oi = _dot(qd_scr[r16, cl], s0.astype(BF16))
                o_scr[r8, cl] = o_scr[r8, cl] + oi[par * SEG:(par + 1) * SEG, :]
                aug = jnp.concatenate([ke_scr[r8, cl], _decay_tail(dec_scr[r8, cl])], axis=0).astype(BF16)
                v8 = p_ref[r8, OFF_I + hd * LANE:OFF_I + (hd + 1) * LANE]
                rhs = jnp.concatenate([jnp.concatenate([v8, zeros8], axis=0).astype(BF16), ones_tail], axis=1)
                ud = _dot_tn(aug, rhs)
                s8_ref[i, hd] = ud[:, LANE:] * s0 + ud[:, :LANE]
        return carry

    lax.fori_loop(0, nb // 2, pair_body, 0)

    for hd in range(HG_HEADS):
        cl = slice(hd * LANE, (hd + 1) * LANE)
        g = p_ref[:, OFF_G + hd * LANE:OFF_G + (hd + 1) * LANE]
        mixh_ref[:, cl] = (_rms(o_scr[:, cl], hgn_ref[:, cl]) * _silu(g)).astype(BF16)


def _sample_hgrn(proj, s0, lbl, hgn, *, nb=16):
    nseq = s0.shape[0]
    rows = nb * SEG
    kern = functools.partial(_sample_hgrn_kernel, nb=nb)
    return pl.pallas_call(
        kern,
        grid=(nseq // nb,),
        in_specs=[pl.BlockSpec((rows, OFF_XBC), lambda i: (i, 0)),
                  pl.BlockSpec((nb, HG_HEADS, HG_DK, HG_DV), lambda i: (i, 0, 0, 0)),
                  _const_spec(lbl.shape), _const_spec(hgn.shape)],
        out_specs=[pl.BlockSpec((rows, HG_WIDTH), lambda i: (i, 0)),
                   pl.BlockSpec((nb, HG_HEADS, HG_DK, HG_DV), lambda i: (i, 0, 0, 0))],
        out_shape=[jax.ShapeDtypeStruct((nseq * SEG, HG_WIDTH), BF16),
                   jax.ShapeDtypeStruct(s0.shape, F32)],
        scratch_shapes=[pltpu.VMEM((rows, HG_WIDTH), BF16),
                        pltpu.VMEM((rows, HG_WIDTH), F32),
                        pltpu.VMEM((rows, HG_WIDTH), F32),
                        pltpu.VMEM((rows, HG_WIDTH), F32)],
        compiler_params=pltpu.CompilerParams(
            dimension_semantics=("arbitrary",), vmem_limit_bytes=VMEM_LIMIT),
        name="sample_hgrn",
    )(proj, s0, lbl, hgn)


def _sample_ssd_kernel(xd_ref, z_ref, hist_ref, s0_ref, cw_ref, cb_ref, dtb_ref, alog_ref, dsk_ref, mn_ref,
                       mixm_ref, s8_ref,
                       xs_scr, bm_scr, cm_scr, u_scr, cd_scr, fs_scr, y_scr, *, nb):
    rows = nb * SEG
    t_in_seq = lax.broadcasted_iota(jnp.int32, (rows, LANE), 0) & (SEG - 1)

    for cbk in range(M_CONV_DIM // LANE):
        cl = slice(cbk * LANE, (cbk + 1) * LANE)
        x = xd_ref[:, cl]
        hist = hist_ref[:, cl]
        acc = cb_ref[:, cl] + cw_ref[M_CONV - 1:M_CONV, cl] * x
        for d in range(1, M_CONV):
            shifted = jnp.where(t_in_seq >= d, pltpu.roll(x, d, 0), pltpu.roll(hist, rows - SEG + d, 0))
            acc = acc + cw_ref[M_CONV - 1 - d:M_CONV - d, cl] * shifted
        act = _silu(acc)
        if cbk < M_WIDTH // LANE:
            xs_scr[:, cl] = act
        elif cbk < (M_WIDTH + M_GROUPS * M_DSTATE) // LANE:
            bm_scr[:, cbk * LANE - M_WIDTH:(cbk + 1) * LANE - M_WIDTH] = act
        else:
            o0 = cbk * LANE - M_WIDTH - M_GROUPS * M_DSTATE
            cm_scr[:, o0:o0 + LANE] = act.astype(BF16)

    a_row = -jnp.exp(alog_ref[...])
    dtv = _softplus(xd_ref[:, M_CONV_DIM:M_CONV_DIM + LANE] + dtb_ref[...])
    cum = _cumsum_rows(dtv * a_row, SEG)
    cum_t = cum.T
    mask = _block_causal_mask(rows)
    lane_lo = lax.broadcasted_iota(jnp.int32, (rows, LANE), 1) < M_HEADDIM
    cbs = {}
    for p in range(M_HEADS // 2):
        grp = p // (M_HEADS // 2 // M_GROUPS)
        cl = slice(p * LANE, (p + 1) * LANE)
        if grp not in cbs:
            bm = bm_scr[:, grp * M_DSTATE:(grp + 1) * M_DSTATE].astype(BF16)
            cbs[grp] = _dot_nt(cm_scr[:, grp * M_DSTATE:(grp + 1) * M_DSTATE], bm)
        cb = cbs[grp]
        cum_h = [_lane_bcast(cum, hh) for hh in (2 * p, 2 * p + 1)]
        cum_x = jnp.where(lane_lo, cum_h[0], cum_h[1])
        dt_x = jnp.where(lane_lo, _lane_bcast(dtv, 2 * p), _lane_bcast(dtv, 2 * p + 1))
        last_x = _seg_last(cum_x)
        xs = xs_scr[:, cl]
        xdt = xs * dt_x
        ms = []
        for i, hh in enumerate((2 * p, 2 * p + 1)):
            seg = cum_h[i] - cum_t[hh:hh + 1, :]
            ms.append((cb * jnp.exp(jnp.where(mask, seg, NEG_BIG))).astype(BF16))
        x_lo = jnp.where(lane_lo, xdt, 0.0).astype(BF16)
        x_hi = jnp.where(lane_lo, 0.0, xdt).astype(BF16)
        y = _dot(jnp.concatenate(ms, axis=1), jnp.concatenate([x_lo, x_hi], axis=0))
        y_scr[:, cl] = y + dsk_ref[:, cl] * xs
        u_scr[:, cl] = xdt * jnp.exp(last_x - cum_x)
        cd_scr[:, cl] = jnp.exp(last_x)
        fs_scr[:, cl] = jnp.exp(cum_x)

    ones_tail = _ones_tail()
    zeros8 = jnp.zeros((SEG, M_DSTATE), F32)

    def pair_body(m, carry):
        r16 = pl.ds(pl.multiple_of(m * 2 * SEG, 2 * SEG), 2 * SEG)
        for par in range(2):
            i = 2 * m + par
            r8 = pl.ds(pl.multiple_of(i * SEG, SEG), SEG)
            for grp in range(M_GROUPS):
                gl = slice(grp * M_GROUP_WIDTH, (grp + 1) * M_GROUP_WIDTH)
                nl = slice(grp * M_DSTATE, (grp + 1) * M_DSTATE)
                s0 = s0_ref[i, gl, :]
                yi = _dot_nt(cm_scr[r16, nl], s0.astype(BF16))
                y_scr[r8, gl] = y_scr[r8, gl] + yi[par * SEG:(par + 1) * SEG, :] * fs_scr[r8, gl]
                aug = jnp.concatenate([u_scr[r8, gl], _decay_tail(cd_scr[r8, gl])], axis=0).astype(BF16)
                rhs = jnp.concatenate(
                    [jnp.concatenate([bm_scr[r8, nl], zeros8], axis=0).astype(BF16), ones_tail], axis=1)
                ud = _dot_tn(aug, rhs)
                s8_ref[i, gl, :] = ud[:, M_DSTATE:] * s0 + ud[:, :M_DSTATE]
        return carry

    lax.fori_loop(0, nb // 2, pair_body, 0)

    for grp in range(M_GROUPS):
        gl = slice(grp * M_GROUP_WIDTH, (grp + 1) * M_GROUP_WIDTH)
        y = y_scr[:, gl] * _silu(z_ref[:, gl])
        mixm_ref[:, gl] = _rms(y, mn_ref[:, gl]).astype(BF16)


def _sample_ssd(proj, hist, s0, cw, cb, dtb, alog, dsk, mn, *, nb=16):
    nseq = s0.shape[0]
    rows = nb * SEG
    kern = functools.partial(_sample_ssd_kernel, nb=nb)
    params = (cw, cb, dtb, alog, dsk, mn)
    return pl.pallas_call(
        kern,
        grid=(nseq // nb,),
        in_specs=[pl.BlockSpec((rows, XD_WIDTH), lambda i: (i, OFF_XBC // XD_WIDTH)),
                  pl.BlockSpec((rows, M_WIDTH), lambda i: (i, OFF_Z // M_WIDTH)),
                  pl.BlockSpec((rows, M_CONV_DIM), lambda i: (i, 0)),
                  pl.BlockSpec((nb, M_WIDTH, M_DSTATE), lambda i: (i, 0, 0))]
                 + [_const_spec(p.shape) for p in params],
        out_specs=[pl.BlockSpec((rows, M_WIDTH), lambda i: (i, 0)),
                   pl.BlockSpec((nb, M_WIDTH, M_DSTATE), lambda i: (i, 0, 0))],
        out_shape=[jax.ShapeDtypeStruct((nseq * SEG, M_WIDTH), BF16),
                   jax.ShapeDtypeStruct(s0.shape, F32)],
        scratch_shapes=[pltpu.VMEM((rows, M_WIDTH), F32),
                        pltpu.VMEM((rows, M_GROUPS * M_DSTATE), F32),
                        pltpu.VMEM((rows, M_GROUPS * M_DSTATE), BF16),
                        pltpu.VMEM((rows, M_WIDTH), F32),
                        pltpu.VMEM((rows, M_WIDTH), F32),
                        pltpu.VMEM((rows, M_WIDTH), F32),
                        pltpu.VMEM((rows, M_WIDTH), F32)],
        compiler_params=pltpu.CompilerParams(
            dimension_semantics=("arbitrary",), vmem_limit_bytes=VMEM_LIMIT),
        name="sample_ssd",
    )(proj, proj, hist, s0, *params)


def _out_mlp_kernel(x_ref, mh_ref, mm_ref, wo_ref, ln2_ref, wu_ref, wd_ref, lnf_ref, o_ref, *, ff_tile):
    x1 = x_ref[...] + _dot(mh_ref[...], wo_ref[0:HG_WIDTH, :]) + _dot(mm_ref[...], wo_ref[HG_WIDTH:, :])
    hn = _rms(x1, ln2_ref[...]).astype(BF16)
    mlp = None
    for j in range(D_FF // ff_tile):
        u = jnp.maximum(_dot(hn, wu_ref[:, j * ff_tile:(j + 1) * ff_tile]), 0.0)
        d = _dot((u * u).astype(BF16), wd_ref[j * ff_tile:(j + 1) * ff_tile, :])
        mlp = d if mlp is None else mlp + d
    o_ref[...] = _rms(x1 + mlp, lnf_ref[...])


def _out_mlp(x, mix_h, mix_m, w_out, ln2, w_up, w_down, ln_f, *, tm=512, ff_tile=1024):
    rows = x.shape[0]
    kern = functools.partial(_out_mlp_kernel, ff_tile=ff_tile)
    row_spec = lambda w: pl.BlockSpec((tm, w), lambda i: (i, 0))
    return pl.pallas_call(
        kern,
        grid=(rows // tm,),
        in_specs=[row_spec(D_MODEL), row_spec(HG_WIDTH), row_spec(M_WIDTH),
                  _const_spec(w_out.shape), _const_spec(ln2.shape), _const_spec(w_up.shape),
                  _const_spec(w_down.shape), _const_spec(ln_f.shape)],
        out_specs=row_spec(D_MODEL),
        out_shape=jax.ShapeDtypeStruct((rows, D_MODEL), F32),
        compiler_params=pltpu.CompilerParams(
            dimension_semantics=("arbitrary",), vmem_limit_bytes=VMEM_LIMIT),
        name="out_mlp",
    )(x, mix_h, mix_m, w_out, ln2, w_up, w_down, ln_f)


def _pad_lanes(v):
    return jnp.pad(v, ((0, 0), (0, LANE - v.shape[1])))


def kernel(x_prompt, x_sample, state_hgrn, state_ssm, state_conv, hg_lb_logits, ln1, w_in, hg_norm, conv_w,
           conv_b, dt_bias, a_log, d_skip, m_norm, w_out, ln2, w_up, w_down, ln_f):
    l = 0
    bp, seq, _ = x_prompt.shape
    bs, dseq, _ = x_sample.shape
    assert DEPTH == 1 and dseq == SEG

    w_all = _pack_in_proj(jnp.transpose(w_in[l]))
    lbl = hg_lb_logits.astype(F32)
    ln1_r = ln1[l][None, :]
    hgn_r = hg_norm[l].reshape(1, HG_WIDTH)
    cw = conv_w[l]
    cb_r = conv_b[l][None, :]
    dtb_r = _pad_lanes(dt_bias[l][None, :])
    alog_r = _pad_lanes(a_log[l][None, :])
    dsk_r = jnp.repeat(d_skip[l], M_HEADDIM)[None, :]
    mn_r = m_norm[l][None, :]
    mlp_w = (w_out[l].astype(BF16), ln2[l][None, :], w_up[l].astype(BF16), w_down[l].astype(BF16), ln_f[None, :])

    mixh_p, mixm_p, hgs_p, ssm_p, conv_p = _prompt_mixer(
        x_prompt, lbl, ln1_r, w_all, hgn_r, cw, cb_r, dtb_r, alog_r, dsk_r, mn_r)
    y_p = _out_mlp(x_prompt.reshape(bp * seq, D_MODEL), mixh_p.reshape(bp * seq, HG_WIDTH),
                   mixm_p.reshape(bp * seq, M_WIDTH), *mlp_w)

    xs2 = x_sample.reshape(bs * SEG, D_MODEL)
    proj_s = _sample_in_proj(xs2, ln1_r, w_all)
    mixh_s, hgs_s = _sample_hgrn(proj_s, state_hgrn[l], lbl, hgn_r)
    hist = jnp.pad(state_conv[l], ((0, 0), (SEG - (M_CONV - 1), 0), (0, 0))).reshape(bs * SEG, M_CONV_DIM)
    mixm_s, ssm_s = _sample_ssd(proj_s, hist, state_ssm[l].reshape(bs, M_WIDTH, M_DSTATE),
                                cw, cb_r, dtb_r, alog_r, dsk_r, mn_r)
    y_s = _out_mlp(xs2, mixh_s, mixm_s, *mlp_w)
    conv_s = proj_s.reshape(bs, SEG, W_IN_COLS)[:, SEG - (M_CONV - 1):, OFF_XBC:OFF_DT]

    return (y_p.reshape(bp, seq, D_MODEL), y_s.reshape(bs, SEG, D_MODEL),
            hgs_p[None], hgs_s[None],
            ssm_p.reshape(1, bp, M_HEADS, M_HEADDIM, M_DSTATE), ssm_s.reshape(1, bs, M_HEADS, M_HEADDIM, M_DSTATE),
            conv_p[None], conv_s[None])
```

```python
import functools

import jax
import jax.numpy as jnp
from jax import lax
from jax.experimental import pallas as pl
from jax.experimental.pallas import tpu as pltpu

F32 = jnp.float32
BF16 = jnp.bfloat16

D_MODEL = 1024
DEPTH = 1
HG_HEADS = 8
HG_DK = 128
HG_DV = 128
HG_WIDTH = HG_HEADS * HG_DV
M_WIDTH = 1024
M_HEADDIM = 64
M_HEADS = M_WIDTH // M_HEADDIM
M_DSTATE = 128
M_GROUPS = 2
M_GROUP_WIDTH = M_WIDTH // M_GROUPS
M_CONV = 4
M_CONV_DIM = M_WIDTH + 2 * M_GROUPS * M_DSTATE
D_FF = 4 * D_MODEL
NORM_EPS = 1e-5

LANE = 128
SUBLANE = 8

OFF_Q, OFF_F, OFF_I, OFF_G = 0, 1024, 2048, 3072
HG_PROJ_WIDTH = 4096
OFF_XBC = HG_PROJ_WIDTH
OFF_DT = OFF_XBC + M_CONV_DIM
XD_WIDTH = 2048
OFF_Z = OFF_XBC + XD_WIDTH
W_IN_COLS = OFF_Z + M_WIDTH
PACK_TILE = 1024

VMEM_LIMIT = 56 * 1024 * 1024

NEG_BIG = -1e30


def _dot(a, b):
    return jnp.dot(a, b, preferred_element_type=F32)


def _dot_nt(a, b):
    return lax.dot_general(a, b, (((1,), (1,)), ((), ())), preferred_element_type=F32)


def _dot_tn(a, b):
    return lax.dot_general(a, b, (((0,), (0,)), ((), ())), preferred_element_type=F32)


def _silu(x):
    hx = 0.5 * x
    return hx + hx * jnp.tanh(hx)


def _forget_gate(fr, lb):
    half = 0.5 * (1.0 - lb)
    return (lb + half) + half * jnp.tanh(0.5 * fr)


def _softplus(x):
    return jnp.maximum(x, 0.0) + jnp.log(1.0 + jnp.exp(-jnp.abs(x)))


def _rms(x, gain):
    ms = jnp.mean(x * x, axis=-1, keepdims=True)
    return x * lax.rsqrt(ms + NORM_EPS) * gain


def _lower_bound(lbl, layer):
    rows = [lbl[i:i + 1, :] for i in range(DEPTH + 1)]
    m = functools.reduce(jnp.maximum, rows)
    es = [jnp.exp(r - m) for r in rows]
    return sum(es[:layer + 1]) / sum(es)


def _cumsum_rows(x, seg):
    t = lax.broadcasted_iota(jnp.int32, x.shape, 0) & (seg - 1)
    s = 1
    while s < seg:
        x = x + jnp.where(t >= s, pltpu.roll(x, s, 0), 0.0)
        s *= 2
    return x


def _cumsum_all_rows(x):
    t = lax.broadcasted_iota(jnp.int32, (SUBLANE, x.shape[1]), 0)
    parts, carry = [], None
    for i in range(x.shape[0] // SUBLANE):
        xi = x[i * SUBLANE:(i + 1) * SUBLANE, :]
        for s in (1, 2, 4):
            xi = xi + jnp.where(t >= s, pltpu.roll(xi, s, 0), 0.0)
        if carry is not None:
            xi = xi + carry
        carry = xi[SUBLANE - 1:SUBLANE, :]
        parts.append(xi)
    return jnp.concatenate(parts, axis=0)


def _split3(x):
    h = x.astype(BF16)
    r = x - h.astype(F32)
    m = r.astype(BF16)
    l = (r - m.astype(F32)).astype(BF16)
    return h, m, l


def _lane_bcast(x, lane):
    return jnp.broadcast_to(x[:, lane:lane + 1], x.shape)


def _pack_in_proj_kernel(wt_ref, o_ref, *, tn, n_hg, n_xd, src_cols):
    j = pl.program_id(0)
    col0 = _pack_src_tile(j, n_hg, n_xd) * tn
    col = col0 + lax.broadcasted_iota(jnp.int32, wt_ref.shape, 0)
    o_ref[...] = jnp.where(col < src_cols, wt_ref[...], 0.0).T.astype(BF16)


def _pack_src_tile(j, n_hg, n_xd):
    n_z = M_WIDTH // PACK_TILE
    return jnp.where(j < n_hg, j, jnp.where(j < n_hg + n_xd, j + n_z, j - n_xd))


def _pack_in_proj(wt):
    tn = PACK_TILE
    n_hg, n_xd = HG_PROJ_WIDTH // tn, XD_WIDTH // tn
    kern = functools.partial(_pack_in_proj_kernel, tn=tn, n_hg=n_hg, n_xd=n_xd, src_cols=wt.shape[0])
    return pl.pallas_call(
        kern,
        grid=(W_IN_COLS // tn,),
        in_specs=[pl.BlockSpec((tn, D_MODEL), lambda j: (_pack_src_tile(j, n_hg, n_xd), 0))],
        out_specs=pl.BlockSpec((D_MODEL, tn), lambda j: (0, j)),
        out_shape=jax.ShapeDtypeStruct((D_MODEL, W_IN_COLS), BF16),
        compiler_params=pltpu.CompilerParams(
            dimension_semantics=("arbitrary",), vmem_limit_bytes=VMEM_LIMIT),
        name="pack_in_proj",
    )(wt)


def _in_proj(x, ln1_ref, w_ref, bufs, tb):
    h_scr, hg_scr, z_scr, xbc_scr, dt_scr = bufs
    h_scr[...] = _rms(x, ln1_ref[...]).astype(BF16)
    for n0 in range(0, HG_PROJ_WIDTH, 1024):
        hg_scr[:, n0:n0 + 1024] = _dot(h_scr[...], w_ref[:, n0:n0 + 1024])
    xbc_scr[SUBLANE:SUBLANE + tb, :] = _dot(h_scr[...], w_ref[:, OFF_XBC:OFF_DT])
    dt_scr[...] = _dot(h_scr[...], w_ref[:, OFF_DT:OFF_DT + LANE])
    z_scr[...] = _dot(h_scr[...], w_ref[:, OFF_Z:OFF_Z + M_WIDTH])


def _recurrences(bufs, prm, mixh_ref, mixm_ref, work, state, tb, ch, cs):
    _, hg_scr, z_scr, xbc_scr, dt_scr = bufs
    lbl_ref, hgn_ref, cw_ref, cb_ref, dtb_ref, alog_ref, dsk_ref, mn_ref = prm
    xs_scr, bc_scr, y_scr = work
    sth_scr, stm_scr, hist_scr = state

    xbc_scr[0:SUBLANE, :] = hist_scr[...]

    lb = _lower_bound(lbl_ref[...], 0)
    tril_h = (lax.broadcasted_iota(jnp.int32, (ch, ch), 0) >= lax.broadcasted_iota(jnp.int32, (ch, ch), 1))
    for j in range(tb // ch):
        r = slice(j * ch, (j + 1) * ch)
        for hd in range(HG_HEADS):
            cl = slice(hd * LANE, (hd + 1) * LANE)
            q = hg_scr[r, OFF_Q + hd * LANE:OFF_Q + (hd + 1) * LANE]
            fr = hg_scr[r, OFF_F + hd * LANE:OFF_F + (hd + 1) * LANE]
            v = hg_scr[r, OFF_I + hd * LANE:OFF_I + (hd + 1) * LANE]
            g = hg_scr[r, OFF_G + hd * LANE:OFF_G + (hd + 1) * LANE]
            lbh = lb[:, cl]
            f = _forget_gate(fr, lbh)
            a = jnp.log(f)
            k = 1.0 - f
            cum = _cumsum_all_rows(a)
            mid = cum[ch // 2 - 1:ch // 2, :]
            last = cum[ch - 1:ch, :]
            qd_mid = q * jnp.exp(cum - mid)
            kd_mid = k * jnp.exp(mid - cum)
            qd = (qd_mid * jnp.exp(mid)).astype(BF16)
            k_end = (kd_mid * jnp.exp(last - mid)).astype(BF16)
            dec = jnp.exp(last)
            vb = v.astype(BF16)
            att = _dot_nt(qd_mid.astype(BF16), kd_mid.astype(BF16))
            att = jnp.where(tril_h, att, 0.0).astype(BF16)
            st = sth_scr[hd]
            o = _dot(att, vb) + _dot_nt(qd, st.astype(BF16))
            sth_scr[hd] = dec * st + _dot_tn(vb, k_end)
            on = _rms(o, hgn_ref[:, cl]) * _silu(g)
            mixh_ref[0, r, cl] = on.astype(BF16)

    assert M_CONV == 4
    for cbk in range(M_CONV_DIM // LANE):
        cl = slice(cbk * LANE, (cbk + 1) * LANE)
        x = xbc_scr[:, cl]
        x1 = pltpu.roll(x, 1, 0)
        u = cw_ref[1:2, cl] * x + cw_ref[0:1, cl] * x1
        acc = cb_ref[:, cl] + cw_ref[3:4, cl] * x + cw_ref[2:3, cl] * x1 + pltpu.roll(u, 2, 0)
        act = _silu(acc[SUBLANE:, :])
        if cbk < M_WIDTH // LANE:
            xs_scr[:, cl] = act
        else:
            bc_scr[:, (cbk * LANE - M_WIDTH):(cbk * LANE - M_WIDTH) + LANE] = act.astype(BF16)
    hist_scr[...] = xbc_scr[tb:tb + SUBLANE, :]

    a_row = -jnp.exp(alog_ref[...])
    tril_m = (lax.broadcasted_iota(jnp.int32, (cs, cs), 0) >= lax.broadcasted_iota(jnp.int32, (cs, cs), 1))
    lane_lo = lax.broadcasted_iota(jnp.int32, (cs, LANE), 1) < M_HEADDIM
    for s in range(tb // cs):
        r = slice(s * cs, (s + 1) * cs)
        dtv = _softplus(dt_scr[r, :] + dtb_ref[...])
        cum = _cumsum_all_rows(dtv * a_row)
        cum_t = cum.T
        cbs = {}
        for p in range(M_HEADS // 2):
            grp = p // (M_HEADS // 2 // M_GROUPS)
            cl = slice(p * LANE, (p + 1) * LANE)
            bm = bc_scr[r, grp * M_DSTATE:(grp + 1) * M_DSTATE]
            cm = bc_scr[r, (M_GROUPS + grp) * M_DSTATE:(M_GROUPS + grp + 1) * M_DSTATE]
            if grp not in cbs:
                cbs[grp] = _dot_nt(cm, bm)
            cb = cbs[grp]
            cum_h = [_lane_bcast(cum, hh) for hh in (2 * p, 2 * p + 1)]
            cum_x = jnp.where(lane_lo, cum_h[0], cum_h[1])
            dt_x = jnp.where(lane_lo, _lane_bcast(dtv, 2 * p), _lane_bcast(dtv, 2 * p + 1))
            last_x = cum_x[cs - 1:cs, :]
            xs = xs_scr[r, cl]
            xdt = xs * dt_x
            ms = []
            for i, hh in enumerate((2 * p, 2 * p + 1)):
                seg = cum_h[i] - cum_t[hh:hh + 1, :]
                ms.append((cb * jnp.exp(jnp.where(tril_m, seg, NEG_BIG))).astype(BF16))
            x_lo = jnp.where(lane_lo, xdt, 0.0).astype(BF16)
            x_hi = jnp.where(lane_lo, 0.0, xdt).astype(BF16)
            y = _dot(jnp.concatenate(ms, axis=1), jnp.concatenate([x_lo, x_hi], axis=0))
            st = stm_scr[:, cl]
            y = y + _dot(cm, st.astype(BF16)) * jnp.exp(cum_x)
            y = y + dsk_ref[:, cl] * xs
            to_end = (xdt * jnp.exp(last_x - cum_x)).astype(BF16)
            stm_scr[:, cl] = jnp.exp(last_x) * st + _dot_tn(bm, to_end)
            y_scr[r, cl] = y * _silu(z_scr[r, cl])
        for grp in range(M_GROUPS):
            gl = slice(grp * M_GROUP_WIDTH, (grp + 1) * M_GROUP_WIDTH)
            mixm_ref[0, r, gl] = _rms(y_scr[r, gl], mn_ref[:, gl]).astype(BF16)


def _prompt_mixer_kernel(x_ref, lbl_ref, ln1_ref, w_ref, hgn_ref, cw_ref, cb_ref, dtb_ref, alog_ref, dsk_ref, mn_ref,
                         mixh_ref, mixm_ref, hgs_ref, ssm_ref, conv_ref,
                         h_scr, hg_scr, z_scr, xbc_scr, dt_scr, xs_scr, bc_scr, y_scr, sth_scr, stm_scr, hist_scr,
                         *, tb, ch, cs):
    c = pl.program_id(1)
    last_c = pl.num_programs(1) - 1

    @pl.when(c == 0)
    def _():
        sth_scr[...] = jnp.zeros(sth_scr.shape, F32)
        stm_scr[...] = jnp.zeros(stm_scr.shape, F32)
        hist_scr[...] = jnp.zeros(hist_scr.shape, F32)

    bufs = (h_scr, hg_scr, z_scr, xbc_scr, dt_scr)
    _in_proj(x_ref[0], ln1_ref, w_ref, bufs, tb)
    prm = (lbl_ref, hgn_ref, cw_ref, cb_ref, dtb_ref, alog_ref, dsk_ref, mn_ref)
    _recurrences(bufs, prm, mixh_ref, mixm_ref, (xs_scr, bc_scr, y_scr), (sth_scr, stm_scr, hist_scr), tb, ch, cs)

    @pl.when(c == last_c)
    def _():
        for hd in range(HG_HEADS):
            hgs_ref[0, hd] = sth_scr[hd].T
        for p in range(M_WIDTH // LANE):
            ssm_ref[0, p * LANE:(p + 1) * LANE, :] = stm_scr[:, p * LANE:(p + 1) * LANE].T
        conv_ref[0] = hist_scr[SUBLANE - (M_CONV - 1):SUBLANE, :]


def _const_spec(shape):
    nd = len(shape)
    return pl.BlockSpec(shape, lambda *_: (0,) * nd, pipeline_mode=pl.Buffered(1))


def _prompt_mixer(x, lbl, ln1, w_all, hgn, cw, cb, dtb, alog, dsk, mn, *, tb=256, ch=64, cs=128):
    bsz, seq, _ = x.shape
    nc = seq // tb
    kern = functools.partial(_prompt_mixer_kernel, tb=tb, ch=ch, cs=cs)
    params = (lbl, ln1, w_all, hgn, cw, cb, dtb, alog, dsk, mn)
    return pl.pallas_call(
        kern,
        grid=(bsz, nc),
        in_specs=[pl.BlockSpec((1, tb, D_MODEL), lambda b, c: (b, c, 0))] + [_const_spec(p.shape) for p in params],
        out_specs=[
            pl.BlockSpec((1, tb, HG_WIDTH), lambda b, c: (b, c, 0)),
            pl.BlockSpec((1, tb, M_WIDTH), lambda b, c: (b, c, 0)),
            pl.BlockSpec((1, HG_HEADS, HG_DK, HG_DV), lambda b, c: (b, 0, 0, 0)),
            pl.BlockSpec((1, M_WIDTH, M_DSTATE), lambda b, c: (b, 0, 0)),
            pl.BlockSpec((1, M_CONV - 1, M_CONV_DIM), lambda b, c: (b, 0, 0)),
        ],
        out_shape=[
            jax.ShapeDtypeStruct((bsz, seq, HG_WIDTH), BF16),
            jax.ShapeDtypeStruct((bsz, seq, M_WIDTH), BF16),
            jax.ShapeDtypeStruct((bsz, HG_HEADS, HG_DK, HG_DV), F32),
            jax.ShapeDtypeStruct((bsz, M_WIDTH, M_DSTATE), F32),
            jax.ShapeDtypeStruct((bsz, M_CONV - 1, M_CONV_DIM), F32),
        ],
        scratch_shapes=[
            pltpu.VMEM((tb, D_MODEL), BF16),
            pltpu.VMEM((tb, HG_PROJ_WIDTH), F32),
            pltpu.VMEM((tb, M_WIDTH), F32),
            pltpu.VMEM((tb + SUBLANE, M_CONV_DIM), F32),
            pltpu.VMEM((tb, LANE), F32),
            pltpu.VMEM((tb, M_WIDTH), F32),
            pltpu.VMEM((tb, 2 * M_GROUPS * M_DSTATE), BF16),
            pltpu.VMEM((tb, M_WIDTH), F32),
            pltpu.VMEM((HG_HEADS, HG_DV, HG_DK), F32),
            pltpu.VMEM((M_DSTATE, M_WIDTH), F32),
            pltpu.VMEM((SUBLANE, M_CONV_DIM), F32),
        ],
        compiler_params=pltpu.CompilerParams(
            dimension_semantics=("arbitrary", "arbitrary"), vmem_limit_bytes=VMEM_LIMIT),
        name="prompt_mixer",
    )(x, *params)


def _sample_in_proj_kernel(x_ref, ln1_ref, w_ref, o_ref, h_scr):
    @pl.when(pl.program_id(0) == 0)
    def _():
        h_scr[...] = _rms(x_ref[...], ln1_ref[...]).astype(BF16)

    o_ref[...] = _dot(h_scr[...], w_ref[...])


def _sample_in_proj(x, ln1, w_all, *, tn=1024):
    rows = x.shape[0]
    return pl.pallas_call(
        _sample_in_proj_kernel,
        grid=(W_IN_COLS // tn,),
        in_specs=[_const_spec((rows, D_MODEL)), _const_spec(ln1.shape),
                  pl.BlockSpec((D_MODEL, tn), lambda n: (0, n))],
        out_specs=pl.BlockSpec((rows, tn), lambda n: (0, n)),
        out_shape=jax.ShapeDtypeStruct((rows, W_IN_COLS), F32),
        scratch_shapes=[pltpu.VMEM((rows, D_MODEL), BF16)],
        compiler_params=pltpu.CompilerParams(
            dimension_semantics=("arbitrary",), vmem_limit_bytes=VMEM_LIMIT),
        name="sample_in_proj",
    )(x, ln1, w_all)


SEG = 8


def _block_causal_mask(rows):
    ri = lax.broadcasted_iota(jnp.int32, (rows, rows), 0)
    ci = lax.broadcasted_iota(jnp.int32, (rows, rows), 1)
    return ((ri & -SEG) == (ci & -SEG)) & (ci <= ri)


def _seg_last(x):
    rows, width = x.shape
    x3 = x.reshape(rows // SEG, SEG, width)
    return jnp.broadcast_to(x3[:, SEG - 1:SEG, :], x3.shape).reshape(rows, width)


def _decay_tail(dec8):
    h, m, l = _split3(dec8)
    t = lax.broadcasted_iota(jnp.int32, dec8.shape, 0)
    return jnp.where(t == 0, h.astype(F32), jnp.where(t == 1, m.astype(F32), jnp.where(t == 2, l.astype(F32), 0.0)))


def _ones_tail():
    t = lax.broadcasted_iota(jnp.int32, (2 * SEG, LANE), 0)
    return jnp.where((t >= SEG) & (t < SEG + 3), 1.0, 0.0).astype(BF16)


def _sample_hgrn_kernel(p_ref, s0_ref, lbl_ref, hgn_ref, mixh_ref, s8_ref,
                        qd_scr, ke_scr, dec_scr, o_scr, *, nb):
    rows = nb * SEG
    lb = _lower_bound(lbl_ref[...], 0)
    mask = _block_causal_mask(rows)
    for hd in range(HG_HEADS):
        cl = slice(hd * LANE, (hd + 1) * LANE)
        q = p_ref[:, OFF_Q + hd * LANE:OFF_Q + (hd + 1) * LANE]
        fr = p_ref[:, OFF_F + hd * LANE:OFF_F + (hd + 1) * LANE]
        v = p_ref[:, OFF_I + hd * LANE:OFF_I + (hd + 1) * LANE]
        lbh = lb[:, cl]
        f = _forget_gate(fr, lbh)
        a = jnp.log(f)
        k = 1.0 - f
        cum = _cumsum_rows(a, SEG)
        last = _seg_last(cum)
        qd = (q * jnp.exp(cum)).astype(BF16)
        kd = (k * jnp.exp(-cum)).astype(BF16)
        att = jnp.where(mask, _dot_nt(qd, kd), 0.0).astype(BF16)
        o_scr[:, cl] = _dot(att, v.astype(BF16))
        qd_scr[:, cl] = qd
        ke_scr[:, cl] = k * jnp.exp(last - cum)
        dec_scr[:, cl] = jnp.exp(last)

    ones_tail = _ones_tail()
    zeros8 = jnp.zeros((SEG, LANE), F32)

    def pair_body(m, carry):
        r16 = pl.ds(pl.multiple_of(m * 2 * SEG, 2 * SEG), 2 * SEG)
        for par in range(2):
            i = 2 * m + par
            r8 = pl.ds(pl.multiple_of(i * SEG, SEG), SEG)
            for hd in range(HG_HEADS):
                cl = slice(hd * LANE, (hd + 1) * LANE)
                s0 = s0_ref[i, hd]
                oi = _dot(qd_scr[r16, cl], s0.astype(BF16))
                o_scr[r8, cl] = o_scr[r8, cl] + oi[par * SEG:(par + 1) * SEG, :]
                aug = jnp.concatenate([ke_scr[r8, cl], _decay_tail(dec_scr[r8, cl])], axis=0).astype(BF16)
                v8 = p_ref[r8, OFF_I + hd * LANE:OFF_I + (hd + 1) * LANE]
                rhs = jnp.concatenate([jnp.concatenate([v8, zeros8], axis=0).astype(BF16), ones_tail], axis=1)
                ud = _dot_tn(aug, rhs)
                s8_ref[i, hd] = ud[:, LANE:] * s0 + ud[:, :LANE]
        return carry

    lax.fori_loop(0, nb // 2, pair_body, 0)

    for hd in range(HG_HEADS):
        cl = slice(hd * LANE, (hd + 1) * LANE)
        g = p_ref[:, OFF_G + hd * LANE:OFF_G + (hd + 1) * LANE]
        mixh_ref[:, cl] = (_rms(o_scr[:, cl], hgn_ref[:, cl]) * _silu(g)).astype(BF16)


def _sample_hgrn(proj, s0, lbl, hgn, *, nb=16):
    nseq = s0.shape[0]
    rows = nb * SEG
    kern = functools.partial(_sample_hgrn_kernel, nb=nb)
    return pl.pallas_call(
        kern,
        grid=(nseq // nb,),
        in_specs=[pl.BlockSpec((rows, OFF_XBC), lambda i: (i, 0)),
                  pl.BlockSpec((nb, HG_HEADS, HG_DK, HG_DV), lambda i: (i, 0, 0, 0)),
                  _const_spec(lbl.shape), _const_spec(hgn.shape)],
        out_specs=[pl.BlockSpec((rows, HG_WIDTH), lambda i: (i, 0)),
                   pl.BlockSpec((nb, HG_HEADS, HG_DK, HG_DV), lambda i: (i, 0, 0, 0))],
        out_shape=[jax.ShapeDtypeStruct((nseq * SEG, HG_WIDTH), BF16),
                   jax.ShapeDtypeStruct(s0.shape, F32)],
        scratch_shapes=[pltpu.VMEM((rows, HG_WIDTH), BF16),
                        pltpu.VMEM((rows, HG_WIDTH), F32),
                        pltpu.VMEM((rows, HG_WIDTH), F32),
                        pltpu.VMEM((rows, HG_WIDTH), F32)],
        compiler_params=pltpu.CompilerParams(
            dimension_semantics=("arbitrary",), vmem_limit_bytes=VMEM_LIMIT),
        name="sample_hgrn",
    )(proj, s0, lbl, hgn)


def _sample_ssd_kernel(xd_ref, z_ref, hist_ref, s0_ref, cw_ref, cb_ref, dtb_ref, alog_ref, dsk_ref, mn_ref,
                       mixm_ref, s8_ref,
                       xs_scr, bm_scr, cm_scr, u_scr, cd_scr, fs_scr, y_scr, *, nb):
    rows = nb * SEG
    t_in_seq = lax.broadcasted_iota(jnp.int32, (rows, LANE), 0) & (SEG - 1)

    for cbk in range(M_CONV_DIM // LANE):
        cl = slice(cbk * LANE, (cbk + 1) * LANE)
        x = xd_ref[:, cl]
        hist = hist_ref[:, cl]
        acc = cb_ref[:, cl] + cw_ref[M_CONV - 1:M_CONV, cl] * x
        for d in range(1, M_CONV):
            shifted = jnp.where(t_in_seq >= d, pltpu.roll(x, d, 0), pltpu.roll(hist, rows - SEG + d, 0))
            acc = acc + cw_ref[M_CONV - 1 - d:M_CONV - d, cl] * shifted
        act = _silu(acc)
        if cbk < M_WIDTH // LANE:
            xs_scr[:, cl] = act
        elif cbk < (M_WIDTH + M_GROUPS * M_DSTATE) // LANE:
            bm_scr[:, cbk * LANE - M_WIDTH:(cbk + 1) * LANE - M_WIDTH] = act
        else:
            o0 = cbk * LANE - M_WIDTH - M_GROUPS * M_DSTATE
            cm_scr[:, o0:o0 + LANE] = act.astype(BF16)

    a_row = -jnp.exp(alog_ref[...])
    dtv = _softplus(xd_ref[:, M_CONV_DIM:M_CONV_DIM + LANE] + dtb_ref[...])
    cum = _cumsum_rows(dtv * a_row, SEG)
    cum_t = cum.T
    mask = _block_causal_mask(rows)
    lane_lo = lax.broadcasted_iota(jnp.int32, (rows, LANE), 1) < M_HEADDIM
    cbs = {}
    for p in range(M_HEADS // 2):
        grp = p // (M_HEADS // 2 // M_GROUPS)
        cl = slice(p * LANE, (p + 1) * LANE)
        if grp not in cbs:
            bm = bm_scr[:, grp * M_DSTATE:(grp + 1) * M_DSTATE].astype(BF16)
            cbs[grp] = _dot_nt(cm_scr[:, grp * M_DSTATE:(grp + 1) * M_DSTATE], bm)
        cb = cbs[grp]
        cum_h = [_lane_bcast(cum, hh) for hh in (2 * p, 2 * p + 1)]
        cum_x = jnp.where(lane_lo, cum_h[0], cum_h[1])
        dt_x = jnp.where(lane_lo, _lane_bcast(dtv, 2 * p), _lane_bcast(dtv, 2 * p + 1))
        last_x = _seg_last(cum_x)
        xs = xs_scr[:, cl]
        xdt = xs * dt_x
        ms = []
        for i, hh in enumerate((2 * p, 2 * p + 1)):
            seg = cum_h[i] - cum_t[hh:hh + 1, :]
            ms.append((cb * jnp.exp(jnp.where(mask, seg, NEG_BIG))).astype(BF16))
        x_lo = jnp.where(lane_lo, xdt, 0.0).astype(BF16)
        x_hi = jnp.where(lane_lo, 0.0, xdt).astype(BF16)
        y = _dot(jnp.concatenate(ms, axis=1), jnp.concatenate([x_lo, x_hi], axis=0))
        y_scr[:, cl] = y + dsk_ref[:, cl] * xs
        u_scr[:, cl] = xdt * jnp.exp(last_x - cum_x)
        cd_scr[:, cl] = jnp.exp(last_x)
        fs_scr[:, cl] = jnp.exp(cum_x)

    ones_tail = _ones_tail()
    zeros8 = jnp.zeros((SEG, M_DSTATE), F32)

    def pair_body(m, carry):
        r16 = pl.ds(pl.multiple_of(m * 2 * SEG, 2 * SEG), 2 * SEG)
        for par in range(2):
            i = 2 * m + par
            r8 = pl.ds(pl.multiple_of(i * SEG, SEG), SEG)
            for grp in range(M_GROUPS):
                gl = slice(grp * M_GROUP_WIDTH, (grp + 1) * M_GROUP_WIDTH)
                nl = slice(grp * M_DSTATE, (grp + 1) * M_DSTATE)
                s0 = s0_ref[i, gl, :]
                yi = _dot_nt(cm_scr[r16, nl], s0.astype(BF16))
                y_scr[r8, gl] = y_scr[r8, gl] + yi[par * SEG:(par + 1) * SEG, :] * fs_scr[r8, gl]
                aug = jnp.concatenate([u_scr[r8, gl], _decay_tail(cd_scr[r8, gl])], axis=0).astype(BF16)
                rhs = jnp.concatenate(
                    [jnp.concatenate([bm_scr[r8, nl], zeros8], axis=0).astype(BF16), ones_tail], axis=1)
                ud = _dot_tn(aug, rhs)
                s8_ref[i, gl, :] = ud[:, M_DSTATE:] * s0 + ud[:, :M_DSTATE]
        return carry

    lax.fori_loop(0, nb // 2, pair_body, 0)

    for grp in range(M_GROUPS):
        gl = slice(grp * M_GROUP_WIDTH, (grp + 1) * M_GROUP_WIDTH)
        y = y_scr[:, gl] * _silu(z_ref[:, gl])
        mixm_ref[:, gl] = _rms(y, mn_ref[:, gl]).astype(BF16)


def _sample_ssd(proj, hist, s0, cw, cb, dtb, alog, dsk, mn, *, nb=16):
    nseq = s0.shape[0]
    rows = nb * SEG
    kern = functools.partial(_sample_ssd_kernel, nb=nb)
    params = (cw, cb, dtb, alog, dsk, mn)
    return pl.pallas_call(
        kern,
        grid=(nseq // nb,),
        in_specs=[pl.BlockSpec((rows, XD_WIDTH), lambda i: (i, OFF_XBC // XD_WIDTH)),
                  pl.BlockSpec((rows, M_WIDTH), lambda i: (i, OFF_Z // M_WIDTH)),
                  pl.BlockSpec((rows, M_CONV_DIM), lambda i: (i, 0)),
                  pl.BlockSpec((nb, M_WIDTH, M_DSTATE), lambda i: (i, 0, 0))]
                 + [_const_spec(p.shape) for p in params],
        out_specs=[pl.BlockSpec((rows, M_WIDTH), lambda i: (i, 0)),
                   pl.BlockSpec((nb, M_WIDTH, M_DSTATE), lambda i: (i, 0, 0))],
        out_shape=[jax.ShapeDtypeStruct((nseq * SEG, M_WIDTH), BF16),
                   jax.ShapeDtypeStruct(s0.shape, F32)],
        scratch_shapes=[pltpu.VMEM((rows, M_WIDTH), F32),
                        pltpu.VMEM((rows, M_GROUPS * M_DSTATE), F32),
                        pltpu.VMEM((rows, M_GROUPS * M_DSTATE), BF16),
                        pltpu.VMEM((rows, M_WIDTH), F32),
                        pltpu.VMEM((rows, M_WIDTH), F32),
                        pltpu.VMEM((rows, M_WIDTH), F32),
                        pltpu.VMEM((rows, M_WIDTH), F32)],
        compiler_params=pltpu.CompilerParams(
            dimension_semantics=("arbitrary",), vmem_limit_bytes=VMEM_LIMIT),
        name="sample_ssd",
    )(proj, proj, hist, s0, *params)


def _out_mlp_kernel(x_ref, mh_ref, mm_ref, wo_ref, ln2_ref, wu_ref, wd_ref, lnf_ref, o_ref, *, ff_tile):
    x1 = x_ref[...] + _dot(mh_ref[...], wo_ref[0:HG_WIDTH, :]) + _dot(mm_ref[...], wo_ref[HG_WIDTH:, :])
    hn = _rms(x1, ln2_ref[...]).astype(BF16)
    mlp = None
    for j in range(D_FF // ff_tile):
        u = jnp.maximum(_dot(hn, wu_ref[:, j * ff_tile:(j + 1) * ff_tile]), 0.0)
        d = _dot((u * u).astype(BF16), wd_ref[j * ff_tile:(j + 1) * ff_tile, :])
        mlp = d if mlp is None else mlp + d
    o_ref[...] = _rms(x1 + mlp, lnf_ref[...])


def _out_mlp(x, mix_h, mix_m, w_out, ln2, w_up, w_down, ln_f, *, tm=512, ff_tile=1024):
    rows = x.shape[0]
    kern = functools.partial(_out_mlp_kernel, ff_tile=ff_tile)
    row_spec = lambda w: pl.BlockSpec((tm, w), lambda i: (i, 0))
    return pl.pallas_call(
        kern,
        grid=(rows // tm,),
        in_specs=[row_spec(D_MODEL), row_spec(HG_WIDTH), row_spec(M_WIDTH),
                  _const_spec(w_out.shape), _const_spec(ln2.shape), _const_spec(w_up.shape),
                  _const_spec(w_down.shape), _const_spec(ln_f.shape)],
        out_specs=row_spec(D_MODEL),
        out_shape=jax.ShapeDtypeStruct((rows, D_MODEL), F32),
        compiler_params=pltpu.CompilerParams(
            dimension_semantics=("arbitrary",), vmem_limit_bytes=VMEM_LIMIT),
        name="out_mlp",
    )(x, mix_h, mix_m, w_out, ln2, w_up, w_down, ln_f)


def _pad_lanes(v):
    return jnp.pad(v, ((0, 0), (0, LANE - v.shape[1])))


def kernel(x_prompt, x_sample, state_hgrn, state_ssm, state_conv, hg_lb_logits, ln1, w_in, hg_norm, conv_w,
           conv_b, dt_bias, a_log, d_skip, m_norm, w_out, ln2, w_up, w_down, ln_f):
    l = 0
    bp, seq, _ = x_prompt.shape
    bs, dseq, _ = x_sample.shape
    assert DEPTH == 1 and dseq == SEG

    w_all = _pack_in_proj(jnp.transpose(w_in[l]))
    lbl = hg_lb_logits.astype(F32)
    ln1_r = ln1[l][None, :]
    hgn_r = hg_norm[l].reshape(1, HG_WIDTH)
    cw = conv_w[l]
    cb_r = conv_b[l][None, :]
    dtb_r = _pad_lanes(dt_bias[l][None, :])
    alog_r = _pad_lanes(a_log[l][None, :])
    dsk_r = jnp.repeat(d_skip[l], M_HEADDIM)[None, :]
    mn_r = m_norm[l][None, :]
    mlp_w = (w_out[l].astype(BF16), ln2[l][None, :], w_up[l].astype(BF16), w_down[l].astype(BF16), ln_f[None, :])

    mixh_p, mixm_p, hgs_p, ssm_p, conv_p = _prompt_mixer(
        x_prompt, lbl, ln1_r, w_all, hgn_r, cw, cb_r, dtb_r, alog_r, dsk_r, mn_r)
    y_p = _out_mlp(x_prompt.reshape(bp * seq, D_MODEL), mixh_p.reshape(bp * seq, HG_WIDTH),
                   mixm_p.reshape(bp * seq, M_WIDTH), *mlp_w)

    xs2 = x_sample.reshape(bs * SEG, D_MODEL)
    proj_s = _sample_in_proj(xs2, ln1_r, w_all)
    mixh_s, hgs_s = _sample_hgrn(proj_s, state_hgrn[l], lbl, hgn_r)
    hist = jnp.pad(state_conv[l], ((0, 0), (SEG - (M_CONV - 1), 0), (0, 0))).reshape(bs * SEG, M_CONV_DIM)
    mixm_s, ssm_s = _sample_ssd(proj_s, hist, state_ssm[l].reshape(bs, M_WIDTH, M_DSTATE),
                                cw, cb_r, dtb_r, alog_r, dsk_r, mn_r)
    y_s = _out_mlp(xs2, mixh_s, mixm_s, *mlp_w)
    conv_s = proj_s.reshape(bs, SEG, W_IN_COLS)[:, SEG - (M_CONV - 1):, OFF_XBC:OFF_DT]

    return (y_p.reshape(bp, seq, D_MODEL), y_s.reshape(bs, SEG, D_MODEL),
            hgs_p[None], hgs_s[None],
            ssm_p.reshape(1, bp, M_HEADS, M_HEADDIM, M_DSTATE), ssm_s.reshape(1, bs, M_HEADS, M_HEADDIM, M_DSTATE),
            conv_p[None], conv_s[None])
```

```python
import functools

import jax
import jax.numpy as jnp
from jax import lax
from jax.experimental import pallas as pl
from jax.experimental.pallas import tpu as pltpu

F32 = jnp.float32
BF16 = jnp.bfloat16

D_MODEL = 1024
DEPTH = 1
HG_HEADS = 8
HG_DK = 128
HG_DV = 128
HG_WIDTH = HG_HEADS * HG_DV
M_WIDTH = 1024
M_HEADDIM = 64
M_HEADS = M_WIDTH // M_HEADDIM
M_DSTATE = 128
M_GROUPS = 2
M_GROUP_WIDTH = M_WIDTH // M_GROUPS
M_CONV = 4
M_CONV_DIM = M_WIDTH + 2 * M_GROUPS * M_DSTATE
D_FF = 4 * D_MODEL
NORM_EPS = 1e-5

LANE = 128
SUBLANE = 8

OFF_Q, OFF_F, OFF_I, OFF_G = 0, 1024, 2048, 3072
HG_PROJ_WIDTH = 4096
OFF_XBC = HG_PROJ_WIDTH
OFF_DT = OFF_XBC + M_CONV_DIM
XD_WIDTH = 2048
OFF_Z = OFF_XBC + XD_WIDTH
W_IN_COLS = OFF_Z + M_WIDTH
PACK_TILE = 1024

VMEM_LIMIT = 56 * 1024 * 1024

NEG_BIG = -1e30


def _dot(a, b):
    return jnp.dot(a, b, preferred_element_type=F32)


def _dot_nt(a, b):
    return lax.dot_general(a, b, (((1,), (1,)), ((), ())), preferred_element_type=F32)


def _dot_tn(a, b):
    return lax.dot_general(a, b, (((0,), (0,)), ((), ())), preferred_element_type=F32)


def _silu(x):
    hx = 0.5 * x
    return hx + hx * jnp.tanh(hx)


def _forget_gate(fr, lb):
    half = 0.5 * (1.0 - lb)
    return (lb + half) + half * jnp.tanh(0.5 * fr)


def _softplus(x):
    return jnp.maximum(x, 0.0) + jnp.log(1.0 + jnp.exp(-jnp.abs(x)))


def _rms(x, gain):
    ms = jnp.mean(x * x, axis=-1, keepdims=True)
    return x * lax.rsqrt(ms + NORM_EPS) * gain


def _lower_bound(lbl, layer):
    rows = [lbl[i:i + 1, :] for i in range(DEPTH + 1)]
    m = functools.reduce(jnp.maximum, rows)
    es = [jnp.exp(r - m) for r in rows]
    return sum(es[:layer + 1]) / sum(es)


def _cumsum_rows(x, seg):
    t = lax.broadcasted_iota(jnp.int32, x.shape, 0) & (seg - 1)
    s = 1
    while s < seg:
        x = x + jnp.where(t >= s, pltpu.roll(x, s, 0), 0.0)
        s *= 2
    return x


def _split3(x):
    h = x.astype(BF16)
    r = x - h.astype(F32)
    m = r.astype(BF16)
    l = (r - m.astype(F32)).astype(BF16)
    return h, m, l


def _lane_bcast(x, lane):
    return jnp.broadcast_to(x[:, lane:lane + 1], x.shape)


def _pack_in_proj_kernel(wt_ref, o_ref, *, tn, n_hg, n_xd, src_cols):
    j = pl.program_id(0)
    col0 = _pack_src_tile(j, n_hg, n_xd) * tn
    col = col0 + lax.broadcasted_iota(jnp.int32, wt_ref.shape, 0)
    o_ref[...] = jnp.where(col < src_cols, wt_ref[...], 0.0).T.astype(BF16)


def _pack_src_tile(j, n_hg, n_xd):
    n_z = M_WIDTH // PACK_TILE
    return jnp.where(j < n_hg, j, jnp.where(j < n_hg + n_xd, j + n_z, j - n_xd))


def _pack_in_proj(wt):
    tn = PACK_TILE
    n_hg, n_xd = HG_PROJ_WIDTH // tn, XD_WIDTH // tn
    kern = functools.partial(_pack_in_proj_kernel, tn=tn, n_hg=n_hg, n_xd=n_xd, src_cols=wt.shape[0])
    return pl.pallas_call(
        kern,
        grid=(W_IN_COLS // tn,),
        in_specs=[pl.BlockSpec((tn, D_MODEL), lambda j: (_pack_src_tile(j, n_hg, n_xd), 0))],
        out_specs=pl.BlockSpec((D_MODEL, tn), lambda j: (0, j)),
        out_shape=jax.ShapeDtypeStruct((D_MODEL, W_IN_COLS), BF16),
        compiler_params=pltpu.CompilerParams(
            dimension_semantics=("arbitrary",), vmem_limit_bytes=VMEM_LIMIT),
        name="pack_in_proj",
    )(wt)


def _in_proj(x, ln1_ref, w_ref, bufs, tb):
    h_scr, hg_scr, z_scr, xbc_scr, dt_scr = bufs
    h_scr[...] = _rms(x, ln1_ref[...]).astype(BF16)
    for n0 in range(0, HG_PROJ_WIDTH, 1024):
        hg_scr[:, n0:n0 + 1024] = _dot(h_scr[...], w_ref[:, n0:n0 + 1024])
    xbc_scr[SUBLANE:SUBLANE + tb, :] = _dot(h_scr[...], w_ref[:, OFF_XBC:OFF_DT])
    dt_scr[...] = _dot(h_scr[...], w_ref[:, OFF_DT:OFF_DT + LANE])
    z_scr[...] = _dot(h_scr[...], w_ref[:, OFF_Z:OFF_Z + M_WIDTH])


def _recurrences(bufs, prm, mixh_ref, mixm_ref, work, state, tb, ch, cs):
    _, hg_scr, z_scr, xbc_scr, dt_scr = bufs
    lbl_ref, hgn_ref, cw_ref, cb_ref, dtb_ref, alog_ref, dsk_ref, mn_ref = prm
    xs_scr, bc_scr, y_scr = work
    sth_scr, stm_scr, hist_scr = state

    xbc_scr[0:SUBLANE, :] = hist_scr[...]

    lb = _lower_bound(lbl_ref[...], 0)
    tril_h = (lax.broadcasted_iota(jnp.int32, (ch, ch), 0) >= lax.broadcasted_iota(jnp.int32, (ch, ch), 1))
    for j in range(tb // ch):
        r = slice(j * ch, (j + 1) * ch)
        for hd in range(HG_HEADS):
            cl = slice(hd * LANE, (hd + 1) * LANE)
            q = hg_scr[r, OFF_Q + hd * LANE:OFF_Q + (hd + 1) * LANE]
            fr = hg_scr[r, OFF_F + hd * LANE:OFF_F + (hd + 1) * LANE]
            v = hg_scr[r, OFF_I + hd * LANE:OFF_I + (hd + 1) * LANE]
            g = hg_scr[r, OFF_G + hd * LANE:OFF_G + (hd + 1) * LANE]
            lbh = lb[:, cl]
            f = _forget_gate(fr, lbh)
            a = jnp.log(f)
            k = 1.0 - f
            cum = _cumsum_rows(a, ch)
            mid = cum[ch // 2 - 1:ch // 2, :]
            last = cum[ch - 1:ch, :]
            qd_mid = q * jnp.exp(cum - mid)
            kd_mid = k * jnp.exp(mid - cum)
            qd = (qd_mid * jnp.exp(mid)).astype(BF16)
            k_end = (kd_mid * jnp.exp(last - mid)).astype(BF16)
            dec = jnp.exp(last)
            vb = v.astype(BF16)
            att = _dot_nt(qd_mid.astype(BF16), kd_mid.astype(BF16))
            att = jnp.where(tril_h, att, 0.0).astype(BF16)
            st = sth_scr[hd]
            o = _dot(att, vb) + _dot_nt(qd, st.astype(BF16))
            sth_scr[hd] = dec * st + _dot_tn(vb, k_end)
            on = _rms(o, hgn_ref[:, cl]) * _silu(g)
            mixh_ref[0, r, cl] = on.astype(BF16)

    for cbk in range(M_CONV_DIM // LANE):
        cl = slice(cbk * LANE, (cbk + 1) * LANE)
        acc = cb_ref[:, cl] + cw_ref[M_CONV - 1:M_CONV, cl] * xbc_scr[SUBLANE:SUBLANE + tb, cl]
        for d in range(1, M_CONV):
            acc = acc + cw_ref[M_CONV - 1 - d:M_CONV - d, cl] * xbc_scr[SUBLANE - d:SUBLANE - d + tb, cl]
        act = _silu(acc)
        if cbk < M_WIDTH // LANE:
            xs_scr[:, cl] = act
        else:
            bc_scr[:, (cbk * LANE - M_WIDTH):(cbk * LANE - M_WIDTH) + LANE] = act.astype(BF16)
    hist_scr[...] = xbc_scr[tb:tb + SUBLANE, :]

    a_row = -jnp.exp(alog_ref[...])
    tril_m = (lax.broadcasted_iota(jnp.int32, (cs, cs), 0) >= lax.broadcasted_iota(jnp.int32, (cs, cs), 1))
    lane_lo = lax.broadcasted_iota(jnp.int32, (cs, LANE), 1) < M_HEADDIM
    for s in range(tb // cs):
        r = slice(s * cs, (s + 1) * cs)
        dtv = _softplus(dt_scr[r, :] + dtb_ref[...])
        cum = _cumsum_rows(dtv * a_row, cs)
        cum_t = cum.T
        cbs = {}
        for p in range(M_HEADS // 2):
            grp = p // (M_HEADS // 2 // M_GROUPS)
            cl = slice(p * LANE, (p + 1) * LANE)
            bm = bc_scr[r, grp * M_DSTATE:(grp + 1) * M_DSTATE]
            cm = bc_scr[r, (M_GROUPS + grp) * M_DSTATE:(M_GROUPS + grp + 1) * M_DSTATE]
            if grp not in cbs:
                cbs[grp] = _dot_nt(cm, bm)
            cb = cbs[grp]
            cum_h = [_lane_bcast(cum, hh) for hh in (2 * p, 2 * p + 1)]
            cum_x = jnp.where(lane_lo, cum_h[0], cum_h[1])
            dt_x = jnp.where(lane_lo, _lane_bcast(dtv, 2 * p), _lane_bcast(dtv, 2 * p + 1))
            last_x = cum_x[cs - 1:cs, :]
            xs = xs_scr[r, cl]
            xdt = xs * dt_x
            ms = []
            for i, hh in enumerate((2 * p, 2 * p + 1)):
                seg = cum_h[i] - cum_t[hh:hh + 1, :]
                ms.append((cb * jnp.exp(jnp.where(tril_m, seg, NEG_BIG))).astype(BF16))
            x_lo = jnp.where(lane_lo, xdt, 0.0).astype(BF16)
            x_hi = jnp.where(lane_lo, 0.0, xdt).astype(BF16)
            y = _dot(jnp.concatenate(ms, axis=1), jnp.concatenate([x_lo, x_hi], axis=0))
            st = stm_scr[:, cl]
            y = y + _dot(cm, st.astype(BF16)) * jnp.exp(cum_x)
            y = y + dsk_ref[:, cl] * xs
            to_end = (xdt * jnp.exp(last_x - cum_x)).astype(BF16)
            stm_scr[:, cl] = jnp.exp(last_x) * st + _dot_tn(bm, to_end)
            y_scr[r, cl] = y * _silu(z_scr[r, cl])
        for grp in range(M_GROUPS):
            gl = slice(grp * M_GROUP_WIDTH, (grp + 1) * M_GROUP_WIDTH)
            mixm_ref[0, r, gl] = _rms(y_scr[r, gl], mn_ref[:, gl]).astype(BF16)


def _prompt_mixer_kernel(x_ref, lbl_ref, ln1_ref, w_ref, hgn_ref, cw_ref, cb_ref, dtb_ref, alog_ref, dsk_ref, mn_ref,
                         mixh_ref, mixm_ref, hgs_ref, ssm_ref, conv_ref,
                         h_scr, hg_scr, z_scr, xbc_scr, dt_scr, xs_scr, bc_scr, y_scr, sth_scr, stm_scr, hist_scr,
                         *, tb, ch, cs):
    c = pl.program_id(1)
    last_c = pl.num_programs(1) - 1

    @pl.when(c == 0)
    def _():
        sth_scr[...] = jnp.zeros(sth_scr.shape, F32)
        stm_scr[...] = jnp.zeros(stm_scr.shape, F32)
        hist_scr[...] = jnp.zeros(hist_scr.shape, F32)

    bufs = (h_scr, hg_scr, z_scr, xbc_scr, dt_scr)
    _in_proj(x_ref[0], ln1_ref, w_ref, bufs, tb)
    prm = (lbl_ref, hgn_ref, cw_ref, cb_ref, dtb_ref, alog_ref, dsk_ref, mn_ref)
    _recurrences(bufs, prm, mixh_ref, mixm_ref, (xs_scr, bc_scr, y_scr), (sth_scr, stm_scr, hist_scr), tb, ch, cs)

    @pl.when(c == last_c)
    def _():
        for hd in range(HG_HEADS):
            hgs_ref[0, hd] = sth_scr[hd].T
        for p in range(M_WIDTH // LANE):
            ssm_ref[0, p * LANE:(p + 1) * LANE, :] = stm_scr[:, p * LANE:(p + 1) * LANE].T
        conv_ref[0] = hist_scr[SUBLANE - (M_CONV - 1):SUBLANE, :]


def _const_spec(shape):
    nd = len(shape)
    return pl.BlockSpec(shape, lambda *_: (0,) * nd, pipeline_mode=pl.Buffered(1))


def _prompt_mixer(x, lbl, ln1, w_all, hgn, cw, cb, dtb, alog, dsk, mn, *, tb=256, ch=64, cs=128):
    bsz, seq, _ = x.shape
    nc = seq // tb
    kern = functools.partial(_prompt_mixer_kernel, tb=tb, ch=ch, cs=cs)
    params = (lbl, ln1, w_all, hgn, cw, cb, dtb, alog, dsk, mn)
    return pl.pallas_call(
        kern,
        grid=(bsz, nc),
        in_specs=[pl.BlockSpec((1, tb, D_MODEL), lambda b, c: (b, c, 0))] + [_const_spec(p.shape) for p in params],
        out_specs=[
            pl.BlockSpec((1, tb, HG_WIDTH), lambda b, c: (b, c, 0)),
            pl.BlockSpec((1, tb, M_WIDTH), lambda b, c: (b, c, 0)),
            pl.BlockSpec((1, HG_HEADS, HG_DK, HG_DV), lambda b, c: (b, 0, 0, 0)),
            pl.BlockSpec((1, M_WIDTH, M_DSTATE), lambda b, c: (b, 0, 0)),
            pl.BlockSpec((1, M_CONV - 1, M_CONV_DIM), lambda b, c: (b, 0, 0)),
        ],
        out_shape=[
            jax.ShapeDtypeStruct((bsz, seq, HG_WIDTH), BF16),
            jax.ShapeDtypeStruct((bsz, seq, M_WIDTH), BF16),
            jax.ShapeDtypeStruct((bsz, HG_HEADS, HG_DK, HG_DV), F32),
            jax.ShapeDtypeStruct((bsz, M_WIDTH, M_DSTATE), F32),
            jax.ShapeDtypeStruct((bsz, M_CONV - 1, M_CONV_DIM), F32),
        ],
        scratch_shapes=[
            pltpu.VMEM((tb, D_MODEL), BF16),
            pltpu.VMEM((tb, HG_PROJ_WIDTH), F32),
            pltpu.VMEM((tb, M_WIDTH), F32),
            pltpu.VMEM((tb + SUBLANE, M_CONV_DIM), F32),
            pltpu.VMEM((tb, LANE), F32),
            pltpu.VMEM((tb, M_WIDTH), F32),
            pltpu.VMEM((tb, 2 * M_GROUPS * M_DSTATE), BF16),
            pltpu.VMEM((tb, M_WIDTH), F32),
            pltpu.VMEM((HG_HEADS, HG_DV, HG_DK), F32),
            pltpu.VMEM((M_DSTATE, M_WIDTH), F32),
            pltpu.VMEM((SUBLANE, M_CONV_DIM), F32),
        ],
        compiler_params=pltpu.CompilerParams(
            dimension_semantics=("arbitrary", "arbitrary"), vmem_limit_bytes=VMEM_LIMIT),
        name="prompt_mixer",
    )(x, *params)


def _sample_in_proj_kernel(x_ref, ln1_ref, w_ref, o_ref, h_scr):
    @pl.when(pl.program_id(0) == 0)
    def _():
        h_scr[...] = _rms(x_ref[...], ln1_ref[...]).astype(BF16)

    o_ref[...] = _dot(h_scr[...], w_ref[...])


def _sample_in_proj(x, ln1, w_all, *, tn=1024):
    rows = x.shape[0]
    return pl.pallas_call(
        _sample_in_proj_kernel,
        grid=(W_IN_COLS // tn,),
        in_specs=[_const_spec((rows, D_MODEL)), _const_spec(ln1.shape),
                  pl.BlockSpec((D_MODEL, tn), lambda n: (0, n))],
        out_specs=pl.BlockSpec((rows, tn), lambda n: (0, n)),
        out_shape=jax.ShapeDtypeStruct((rows, W_IN_COLS), F32),
        scratch_shapes=[pltpu.VMEM((rows, D_MODEL), BF16)],
        compiler_params=pltpu.CompilerParams(
            dimension_semantics=("arbitrary",), vmem_limit_bytes=VMEM_LIMIT),
        name="sample_in_proj",
    )(x, ln1, w_all)


SEG = 8


def _block_causal_mask(rows):
    ri = lax.broadcasted_iota(jnp.int32, (rows, rows), 0)
    ci = lax.broadcasted_iota(jnp.int32, (rows, rows), 1)
    return ((ri & -SEG) == (ci & -SEG)) & (ci <= ri)


def _seg_last(x):
    rows, width = x.shape
    x3 = x.reshape(rows // SEG, SEG, width)
    return jnp.broadcast_to(x3[:, SEG - 1:SEG, :], x3.shape).reshape(rows, width)


def _decay_tail(dec8):
    h, m, l = _split3(dec8)
    t = lax.broadcasted_iota(jnp.int32, dec8.shape, 0)
    return jnp.where(t == 0, h.astype(F32), jnp.where(t == 1, m.astype(F32), jnp.where(t == 2, l.astype(F32), 0.0)))


def _ones_tail():
    t = lax.broadcasted_iota(jnp.int32, (2 * SEG, LANE), 0)
    return jnp.where((t >= SEG) & (t < SEG + 3), 1.0, 0.0).astype(BF16)


def _sample_hgrn_kernel(p_ref, s0_ref, lbl_ref, hgn_ref, mixh_ref, s8_ref,
                        qd_scr, ke_scr, dec_scr, o_scr, *, nb):
    rows = nb * SEG
    lb = _lower_bound(lbl_ref[...], 0)
    mask = _block_causal_mask(rows)
    for hd in range(HG_HEADS):
        cl = slice(hd * LANE, (hd + 1) * LANE)
        q = p_ref[:, OFF_Q + hd * LANE:OFF_Q + (hd + 1) * LANE]
        fr = p_ref[:, OFF_F + hd * LANE:OFF_F + (hd + 1) * LANE]
        v = p_ref[:, OFF_I + hd * LANE:OFF_I + (hd + 1) * LANE]
        lbh = lb[:, cl]
        f = _forget_gate(fr, lbh)
        a = jnp.log(f)
        k = 1.0 - f
        cum = _cumsum_rows(a, SEG)
        last = _seg_last(cum)
        qd = (q * jnp.exp(cum)).astype(BF16)
        kd = (k * jnp.exp(-cum)).astype(BF16)
        att = jnp.where(mask, _dot_nt(qd, kd), 0.0).astype(BF16)
        o_scr[:, cl] = _dot(att, v.astype(BF16))
        qd_scr[:, cl] = qd
        ke_scr[:, cl] = k * jnp.exp(last - cum)
        dec_scr[:, cl] = jnp.exp(last)

    ones_tail = _ones_tail()
    zeros8 = jnp.zeros((SEG, LANE), F32)

    def pair_body(m, carry):
        r16 = pl.ds(pl.multiple_of(m * 2 * SEG, 2 * SEG), 2 * SEG)
        for par in range(2):
            i = 2 * m + par
            r8 = pl.ds(pl.multiple_of(i * SEG, SEG), SEG)
            for hd in range(HG_HEADS):
                cl = slice(hd * LANE, (hd + 1) * LANE)
                s0 = s0_ref[i, hd]
                oi = _dot(qd_scr[r16, cl], s0.astype(BF16))
                o_scr[r8, cl] = o_scr[r8, cl] + oi[par * SEG:(par + 1) * SEG, :]
                aug = jnp.concatenate([ke_scr[r8, cl], _decay_tail(dec_scr[r8, cl])], axis=0).astype(BF16)
                v8 = p_ref[r8, OFF_I + hd * LANE:OFF_I + (hd + 1) * LANE]
                rhs = jnp.concatenate([jnp.concatenate([v8, zeros8], axis=0).astype(BF16), ones_tail], axis=1)
                ud = _dot_tn(aug, rhs)
                s8_ref[i, hd] = ud[:, LANE:] * s0 + ud[:, :LANE]
        return carry

    lax.fori_loop(0, nb // 2, pair_body, 0, unroll=4)

    for hd in range(HG_HEADS):
        cl = slice(hd * LANE, (hd + 1) * LANE)
        g = p_ref[:, OFF_G + hd * LANE:OFF_G + (hd + 1) * LANE]
        mixh_ref[:, cl] = (_rms(o_scr[:, cl], hgn_ref[:, cl]) * _silu(g)).astype(BF16)


def _sample_hgrn(proj, s0, lbl, hgn, *, nb=16):
    nseq = s0.shape[0]
    rows = nb * SEG
    kern = functools.partial(_sample_hgrn_kernel, nb=nb)
    return pl.pallas_call(
        kern,
        grid=(nseq // nb,),
        in_specs=[pl.BlockSpec((rows, OFF_XBC), lambda i: (i, 0)),
                  pl.BlockSpec((nb, HG_HEADS, HG_DK, HG_DV), lambda i: (i, 0, 0, 0)),
                  _const_spec(lbl.shape), _const_spec(hgn.shape)],
        out_specs=[pl.BlockSpec((rows, HG_WIDTH), lambda i: (i, 0)),
                   pl.BlockSpec((nb, HG_HEADS, HG_DK, HG_DV), lambda i: (i, 0, 0, 0))],
        out_shape=[jax.ShapeDtypeStruct((nseq * SEG, HG_WIDTH), BF16),
                   jax.ShapeDtypeStruct(s0.shape, F32)],
        scratch_shapes=[pltpu.VMEM((rows, HG_WIDTH), BF16),
                        pltpu.VMEM((rows, HG_WIDTH), F32),
                        pltpu.VMEM((rows, HG_WIDTH), F32),
                        pltpu.VMEM((rows, HG_WIDTH), F32)],
        compiler_params=pltpu.CompilerParams(
            dimension_semantics=("arbitrary",), vmem_limit_bytes=VMEM_LIMIT),
        name="sample_hgrn",
    )(proj, s0, lbl, hgn)


def _sample_ssd_kernel(xd_ref, z_ref, c0_ref, s0_ref, cw_ref, cb_ref, dtb_ref, alog_ref, dsk_ref, mn_ref,
                       mixm_ref, s8_ref, c8_ref,
                       hist_scr, raw_scr, xs_scr, bm_scr, cm_scr, u_scr, cd_scr, fs_scr, y_scr, *, nb):
    rows = nb * SEG
    t_in_seq = lax.broadcasted_iota(jnp.int32, (rows, LANE), 0) & (SEG - 1)

    @pl.when(pl.program_id(0) == 0)
    def _():
        hist_scr[...] = jnp.zeros(hist_scr.shape, F32)

    for cbk in range(M_CONV_DIM // LANE):
        cl = slice(cbk * LANE, (cbk + 1) * LANE)
        x = xd_ref[:, cl]
        raw_scr[cbk] = x
        for d in range(M_CONV - 1):
            seq_rows = pl.ds(SEG - (M_CONV - 1) + d, nb, stride=SEG)
            hist_scr[cbk, seq_rows, :] = c0_ref[d, :, cl]
            c8_ref[d, :, cl] = raw_scr[cbk, seq_rows, :]
        hist = hist_scr[cbk]
        acc = cb_ref[:, cl] + cw_ref[M_CONV - 1:M_CONV, cl] * x
        for d in range(1, M_CONV):
            shifted = jnp.where(t_in_seq >= d, pltpu.roll(x, d, 0), pltpu.roll(hist, rows - SEG + d, 0))
            acc = acc + cw_ref[M_CONV - 1 - d:M_CONV - d, cl] * shifted
        act = _silu(acc)
        if cbk < M_WIDTH // LANE:
            xs_scr[:, cl] = act
        elif cbk < (M_WIDTH + M_GROUPS * M_DSTATE) // LANE:
            bm_scr[:, cbk * LANE - M_WIDTH:(cbk + 1) * LANE - M_WIDTH] = act
        else:
            o0 = cbk * LANE - M_WIDTH - M_GROUPS * M_DSTATE
            cm_scr[:, o0:o0 + LANE] = act.astype(BF16)

    a_row = -jnp.exp(alog_ref[...])
    dtv = _softplus(xd_ref[:, M_CONV_DIM:M_CONV_DIM + LANE] + dtb_ref[...])
    cum = _cumsum_rows(dtv * a_row, SEG)
    cum_t = cum.T
    mask = _block_causal_mask(rows)
    lane_lo = lax.broadcasted_iota(jnp.int32, (rows, LANE), 1) < M_HEADDIM
    cbs = {}
    for p in range(M_HEADS // 2):
        grp = p // (M_HEADS // 2 // M_GROUPS)
        cl = slice(p * LANE, (p + 1) * LANE)
        if grp not in cbs:
            bm = bm_scr[:, grp * M_DSTATE:(grp + 1) * M_DSTATE].astype(BF16)
            cbs[grp] = _dot_nt(cm_scr[:, grp * M_DSTATE:(grp + 1) * M_DSTATE], bm)
        cb = cbs[grp]
        cum_h = [_lane_bcast(cum, hh) for hh in (2 * p, 2 * p + 1)]
        cum_x = jnp.where(lane_lo, cum_h[0], cum_h[1])
        dt_x = jnp.where(lane_lo, _lane_bcast(dtv, 2 * p), _lane_bcast(dtv, 2 * p + 1))
        last_x = _seg_last(cum_x)
        xs = xs_scr[:, cl]
        xdt = xs * dt_x
        ms = []
        for i, hh in enumerate((2 * p, 2 * p + 1)):
            seg = cum_h[i] - cum_t[hh:hh + 1, :]
            ms.append((cb * jnp.exp(jnp.where(mask, seg, NEG_BIG))).astype(BF16))
        x_lo = jnp.where(lane_lo, xdt, 0.0).astype(BF16)
        x_hi = jnp.where(lane_lo, 0.0, xdt).astype(BF16)
        y = _dot(jnp.concatenate(ms, axis=1), jnp.concatenate([x_lo, x_hi], axis=0))
        y_scr[:, cl] = y + dsk_ref[:, cl] * xs
        u_scr[:, cl] = xdt * jnp.exp(last_x - cum_x)
        cd_scr[:, cl] = jnp.exp(last_x)
        fs_scr[:, cl] = jnp.exp(cum_x)

    ones_tail = _ones_tail()
    zeros8 = jnp.zeros((SEG, M_DSTATE), F32)

    def pair_body(m, carry):
        r16 = pl.ds(pl.multiple_of(m * 2 * SEG, 2 * SEG), 2 * SEG)
        for par in range(2):
            i = 2 * m + par
            r8 = pl.ds(pl.multiple_of(i * SEG, SEG), SEG)
            for grp in range(M_GROUPS):
                gl = slice(grp * M_GROUP_WIDTH, (grp + 1) * M_GROUP_WIDTH)
                nl = slice(grp * M_DSTATE, (grp + 1) * M_DSTATE)
                s0 = s0_ref[i, gl, :]
                yi = _dot_nt(cm_scr[r16, nl], s0.astype(BF16))
                y_scr[r8, gl] = y_scr[r8, gl] + yi[par * SEG:(par + 1) * SEG, :] * fs_scr[r8, gl]
                aug = jnp.concatenate([u_scr[r8, gl], _decay_tail(cd_scr[r8, gl])], axis=0).astype(BF16)
                rhs = jnp.concatenate(
                    [jnp.concatenate([bm_scr[r8, nl], zeros8], axis=0).astype(BF16), ones_tail], axis=1)
                ud = _dot_tn(aug, rhs)
                s8_ref[i, gl, :] = ud[:, M_DSTATE:] * s0 + ud[:, :M_DSTATE]
        return carry

    lax.fori_loop(0, nb // 2, pair_body, 0, unroll=4)

    for grp in range(M_GROUPS):
        gl = slice(grp * M_GROUP_WIDTH, (grp + 1) * M_GROUP_WIDTH)
        y = y_scr[:, gl] * _silu(z_ref[:, gl])
        mixm_ref[:, gl] = _rms(y, mn_ref[:, gl]).astype(BF16)


def _sample_ssd(proj, c0, s0, cw, cb, dtb, alog, dsk, mn, *, nb=16):
    nseq = s0.shape[0]
    rows = nb * SEG
    kern = functools.partial(_sample_ssd_kernel, nb=nb)
    params = (cw, cb, dtb, alog, dsk, mn)
    conv_spec = pl.BlockSpec((M_CONV - 1, nb, M_CONV_DIM), lambda i: (0, i, 0))
    return pl.pallas_call(
        kern,
        grid=(nseq // nb,),
        in_specs=[pl.BlockSpec((rows, XD_WIDTH), lambda i: (i, OFF_XBC // XD_WIDTH)),
                  pl.BlockSpec((rows, M_WIDTH), lambda i: (i, OFF_Z // M_WIDTH)),
                  conv_spec,
                  pl.BlockSpec((nb, M_WIDTH, M_DSTATE), lambda i: (i, 0, 0))]
                 + [_const_spec(p.shape) for p in params],
        out_specs=[pl.BlockSpec((rows, M_WIDTH), lambda i: (i, 0)),
                   pl.BlockSpec((nb, M_WIDTH, M_DSTATE), lambda i: (i, 0, 0)),
                   conv_spec],
        out_shape=[jax.ShapeDtypeStruct((nseq * SEG, M_WIDTH), BF16),
                   jax.ShapeDtypeStruct(s0.shape, F32),
                   jax.ShapeDtypeStruct(c0.shape, F32)],
        scratch_shapes=[pltpu.VMEM((M_CONV_DIM // LANE, rows, LANE), F32),
                        pltpu.VMEM((M_CONV_DIM // LANE, rows, LANE), F32),
                        pltpu.VMEM((rows, M_WIDTH), F32),
                        pltpu.VMEM((rows, M_GROUPS * M_DSTATE), F32),
                        pltpu.VMEM((rows, M_GROUPS * M_DSTATE), BF16),
                        pltpu.VMEM((rows, M_WIDTH), F32),
                        pltpu.VMEM((rows, M_WIDTH), F32),
                        pltpu.VMEM((rows, M_WIDTH), F32),
                        pltpu.VMEM((rows, M_WIDTH), F32)],
        compiler_params=pltpu.CompilerParams(
            dimension_semantics=("arbitrary",), vmem_limit_bytes=VMEM_LIMIT),
        name="sample_ssd",
    )(proj, proj, c0, s0, *params)


def _out_mlp_kernel(x_ref, mh_ref, mm_ref, wo_ref, ln2_ref, wu_ref, wd_ref, lnf_ref, o_ref, *, ff_tile):
    x1 = x_ref[...] + _dot(mh_ref[...], wo_ref[0:HG_WIDTH, :]) + _dot(mm_ref[...], wo_ref[HG_WIDTH:, :])
    hn = _rms(x1, ln2_ref[...]).astype(BF16)
    mlp = None
    for j in range(D_FF // ff_tile):
        u = jnp.maximum(_dot(hn, wu_ref[:, j * ff_tile:(j + 1) * ff_tile]), 0.0)
        d = _dot((u * u).astype(BF16), wd_ref[j * ff_tile:(j + 1) * ff_tile, :])
        mlp = d if mlp is None else mlp + d
    o_ref[...] = _rms(x1 + mlp, lnf_ref[...])


def _out_mlp(x, mix_h, mix_m, w_out, ln2, w_up, w_down, ln_f, *, tm=512, ff_tile=1024):
    rows = x.shape[0]
    kern = functools.partial(_out_mlp_kernel, ff_tile=ff_tile)
    row_spec = lambda w: pl.BlockSpec((tm, w), lambda i: (i, 0))
    return pl.pallas_call(
        kern,
        grid=(rows // tm,),
        in_specs=[row_spec(D_MODEL), row_spec(HG_WIDTH), row_spec(M_WIDTH),
                  _const_spec(w_out.shape), _const_spec(ln2.shape), _const_spec(w_up.shape),
                  _const_spec(w_down.shape), _const_spec(ln_f.shape)],
        out_specs=row_spec(D_MODEL),
        out_shape=jax.ShapeDtypeStruct((rows, D_MODEL), F32),
        compiler_params=pltpu.CompilerParams(
            dimension_semantics=("arbitrary",), vmem_limit_bytes=VMEM_LIMIT),
        name="out_mlp",
    )(x, mix_h, mix_m, w_out, ln2, w_up, w_down, ln_f)


def _pad_lanes(v):
    return jnp.pad(v, ((0, 0), (0, LANE - v.shape[1])))


def kernel(x_prompt, x_sample, state_hgrn, state_ssm, state_conv, hg_lb_logits, ln1, w_in, hg_norm, conv_w,
           conv_b, dt_bias, a_log, d_skip, m_norm, w_out, ln2, w_up, w_down, ln_f):
    l = 0
    bp, seq, _ = x_prompt.shape
    bs, dseq, _ = x_sample.shape
    assert DEPTH == 1 and dseq == SEG

    w_all = _pack_in_proj(jnp.transpose(w_in[l]))
    lbl = hg_lb_logits.astype(F32)
    ln1_r = ln1[l][None, :]
    hgn_r = hg_norm[l].reshape(1, HG_WIDTH)
    cw = conv_w[l]
    cb_r = conv_b[l][None, :]
    dtb_r = _pad_lanes(dt_bias[l][None, :])
    alog_r = _pad_lanes(a_log[l][None, :])
    dsk_r = jnp.repeat(d_skip[l], M_HEADDIM)[None, :]
    mn_r = m_norm[l][None, :]
    mlp_w = (w_out[l].astype(BF16), ln2[l][None, :], w_up[l].astype(BF16), w_down[l].astype(BF16), ln_f[None, :])

    mixh_p, mixm_p, hgs_p, ssm_p, conv_p = _prompt_mixer(
        x_prompt, lbl, ln1_r, w_all, hgn_r, cw, cb_r, dtb_r, alog_r, dsk_r, mn_r)
    y_p = _out_mlp(x_prompt.reshape(bp * seq, D_MODEL), mixh_p.reshape(bp * seq, HG_WIDTH),
                   mixm_p.reshape(bp * seq, M_WIDTH), *mlp_w)

    xs2 = x_sample.reshape(bs * SEG, D_MODEL)
    proj_s = _sample_in_proj(xs2, ln1_r, w_all)
    mixh_s, hgs_s = _sample_hgrn(proj_s, state_hgrn[l], lbl, hgn_r)
    conv0 = jnp.transpose(state_conv[l], (1, 0, 2))
    mixm_s, ssm_s, conv8 = _sample_ssd(proj_s, conv0, state_ssm[l].reshape(bs, M_WIDTH, M_DSTATE),
                                       cw, cb_r, dtb_r, alog_r, dsk_r, mn_r)
    y_s = _out_mlp(xs2, mixh_s, mixm_s, *mlp_w)
    conv_s = jnp.transpose(conv8, (1, 0, 2))

    return (y_p.reshape(bp, seq, D_MODEL), y_s.reshape(bs, SEG, D_MODEL),
            hgs_p[None], hgs_s[None],
            ssm_p.reshape(1, bp, M_HEADS, M_HEADDIM, M_DSTATE), ssm_s.reshape(1, bs, M_HEADS, M_HEADDIM, M_DSTATE),
            conv_p[None], conv_s[None])
```

```python
import functools

import jax
import jax.numpy as jnp
from jax import lax
from jax.experimental import pallas as pl
from jax.experimental.pallas import tpu as pltpu

F32 = jnp.float32
BF16 = jnp.bfloat16

D_MODEL = 1024
DEPTH = 1
HG_HEADS = 8
HG_DK = 128
HG_DV = 128
HG_WIDTH = HG_HEADS * HG_DV
M_WIDTH = 1024
M_HEADDIM = 64
M_HEADS = M_WIDTH // M_HEADDIM
M_DSTATE = 128
M_GROUPS = 2
M_GROUP_WIDTH = M_WIDTH // M_GROUPS
M_CONV = 4
M_CONV_DIM = M_WIDTH + 2 * M_GROUPS * M_DSTATE
D_FF = 4 * D_MODEL
NORM_EPS = 1e-5

LANE = 128
SUBLANE = 8

OFF_Q, OFF_F, OFF_I, OFF_G = 0, 1024, 2048, 3072
HG_PROJ_WIDTH = 4096
OFF_XBC = HG_PROJ_WIDTH
OFF_DT = OFF_XBC + M_CONV_DIM
XD_WIDTH = 2048
OFF_Z = OFF_XBC + XD_WIDTH
W_IN_COLS = OFF_Z + M_WIDTH
PACK_TILE = 512

VMEM_LIMIT = 56 * 1024 * 1024

NEG_BIG = -1e30


def _dot(a, b):
    return jnp.dot(a, b, preferred_element_type=F32)


def _dot_nt(a, b):
    return lax.dot_general(a, b, (((1,), (1,)), ((), ())), preferred_element_type=F32)


def _dot_tn(a, b):
    return lax.dot_general(a, b, (((0,), (0,)), ((), ())), preferred_element_type=F32)


def _silu(x):
    hx = 0.5 * x
    return hx + hx * jnp.tanh(hx)


def _forget_gate(fr, lb):
    half = 0.5 * (1.0 - lb)
    return (lb + half) + half * jnp.tanh(0.5 * fr)


def _softplus(x):
    return jnp.maximum(x, 0.0) + jnp.log(1.0 + jnp.exp(-jnp.abs(x)))


def _rms(x, gain):
    ms = jnp.mean(x * x, axis=-1, keepdims=True)
    return x * lax.rsqrt(ms + NORM_EPS) * gain


def _lower_bound(lbl, layer):
    rows = [lbl[i:i + 1, :] for i in range(DEPTH + 1)]
    m = functools.reduce(jnp.maximum, rows)
    es = [jnp.exp(r - m) for r in rows]
    return sum(es[:layer + 1]) / sum(es)


def _cumsum_rows(x, seg):
    t = lax.broadcasted_iota(jnp.int32, x.shape, 0) & (seg - 1)
    s = 1
    while s < seg:
        x = x + jnp.where(t >= s, pltpu.roll(x, s, 0), 0.0)
        s *= 2
    return x


def _split3(x):
    h = x.astype(BF16)
    r = x - h.astype(F32)
    m = r.astype(BF16)
    l = (r - m.astype(F32)).astype(BF16)
    return h, m, l


def _lane_bcast(x, lane):
    return jnp.broadcast_to(x[:, lane:lane + 1], x.shape)


def _pack_in_proj_kernel(wt_ref, o_ref, *, tn, n_hg, n_xd, src_cols):
    j = pl.program_id(0)
    col0 = _pack_src_tile(j, n_hg, n_xd) * tn
    col = col0 + lax.broadcasted_iota(jnp.int32, wt_ref.shape, 0)
    o_ref[...] = jnp.where(col < src_cols, wt_ref[...], 0.0).T.astype(BF16)


def _pack_src_tile(j, n_hg, n_xd):
    n_z = M_WIDTH // PACK_TILE
    return jnp.where(j < n_hg, j, jnp.where(j < n_hg + n_xd, j + n_z, j - n_xd))


def _pack_in_proj(wt):
    tn = PACK_TILE
    n_hg, n_xd = HG_PROJ_WIDTH // tn, XD_WIDTH // tn
    kern = functools.partial(_pack_in_proj_kernel, tn=tn, n_hg=n_hg, n_xd=n_xd, src_cols=wt.shape[0])
    return pl.pallas_call(
        kern,
        grid=(W_IN_COLS // tn,),
        in_specs=[pl.BlockSpec((tn, D_MODEL), lambda j: (_pack_src_tile(j, n_hg, n_xd), 0))],
        out_specs=pl.BlockSpec((D_MODEL, tn), lambda j: (0, j)),
        out_shape=jax.ShapeDtypeStruct((D_MODEL, W_IN_COLS), BF16),
        compiler_params=pltpu.CompilerParams(
            dimension_semantics=("arbitrary",), vmem_limit_bytes=VMEM_LIMIT),
        name="pack_in_proj",
    )(wt)


def _in_proj(x, ln1_ref, w_ref, bufs, tb):
    h_scr, hg_scr, z_scr, xbc_scr, dt_scr = bufs
    h_scr[...] = _rms(x, ln1_ref[...]).astype(BF16)
    for n0 in range(0, HG_PROJ_WIDTH, 1024):
        hg_scr[:, n0:n0 + 1024] = _dot(h_scr[...], w_ref[:, n0:n0 + 1024])
    xbc_scr[SUBLANE:SUBLANE + tb, :] = _dot(h_scr[...], w_ref[:, OFF_XBC:OFF_DT])
    dt_scr[...] = _dot(h_scr[...], w_ref[:, OFF_DT:OFF_DT + LANE])
    z_scr[...] = _dot(h_scr[...], w_ref[:, OFF_Z:OFF_Z + M_WIDTH])


def _recurrences(bufs, prm, mixh_ref, mixm_ref, work, state, tb, ch, cs):
    _, hg_scr, z_scr, xbc_scr, dt_scr = bufs
    lbl_ref, hgn_ref, cw_ref, cb_ref, dtb_ref, alog_ref, dsk_ref, mn_ref = prm
    xs_scr, bc_scr, y_scr = work
    sth_scr, stm_scr, hist_scr = state

    xbc_scr[0:SUBLANE, :] = hist_scr[...]

    lb = _lower_bound(lbl_ref[...], 0)
    tril_h = (lax.broadcasted_iota(jnp.int32, (ch, ch), 0) >= lax.broadcasted_iota(jnp.int32, (ch, ch), 1))
    for j in range(tb // ch):
        r = slice(j * ch, (j + 1) * ch)
        for hd in range(HG_HEADS):
            cl = slice(hd * LANE, (hd + 1) * LANE)
            q = hg_scr[r, OFF_Q + hd * LANE:OFF_Q + (hd + 1) * LANE]
            fr = hg_scr[r, OFF_F + hd * LANE:OFF_F + (hd + 1) * LANE]
            v = hg_scr[r, OFF_I + hd * LANE:OFF_I + (hd + 1) * LANE]
            g = hg_scr[r, OFF_G + hd * LANE:OFF_G + (hd + 1) * LANE]
            lbh = lb[:, cl]
            f = _forget_gate(fr, lbh)
            a = jnp.log(f)
            k = 1.0 - f
            cum = _cumsum_rows(a, ch)
            mid = cum[ch // 2 - 1:ch // 2, :]
            last = cum[ch - 1:ch, :]
            qd_mid = q * jnp.exp(cum - mid)
            kd_mid = k * jnp.exp(mid - cum)
            qd = (qd_mid * jnp.exp(mid)).astype(BF16)
            k_end = (kd_mid * jnp.exp(last - mid)).astype(BF16)
            dec = jnp.exp(last)
            vb = v.astype(BF16)
            att = _dot_nt(qd_mid.astype(BF16), kd_mid.astype(BF16))
            att = jnp.where(tril_h, att, 0.0).astype(BF16)
            st = sth_scr[hd]
            o = _dot(att, vb) + _dot_nt(qd, st.astype(BF16))
            sth_scr[hd] = dec * st + _dot_tn(vb, k_end)
            on = _rms(o, hgn_ref[:, cl]) * _silu(g)
            mixh_ref[0, r, cl] = on.astype(BF16)

    for cbk in range(M_CONV_DIM // LANE):
        cl = slice(cbk * LANE, (cbk + 1) * LANE)
        acc = cb_ref[:, cl] + cw_ref[M_CONV - 1:M_CONV, cl] * xbc_scr[SUBLANE:SUBLANE + tb, cl]
        for d in range(1, M_CONV):
            acc = acc + cw_ref[M_CONV - 1 - d:M_CONV - d, cl] * xbc_scr[SUBLANE - d:SUBLANE - d + tb, cl]
        act = _silu(acc)
        if cbk < M_WIDTH // LANE:
            xs_scr[:, cl] = act
        else:
            bc_scr[:, (cbk * LANE - M_WIDTH):(cbk * LANE - M_WIDTH) + LANE] = act.astype(BF16)
    hist_scr[...] = xbc_scr[tb:tb + SUBLANE, :]

    a_row = -jnp.exp(alog_ref[...])
    tril_m = (lax.broadcasted_iota(jnp.int32, (cs, cs), 0) >= lax.broadcasted_iota(jnp.int32, (cs, cs), 1))
    lane_lo = lax.broadcasted_iota(jnp.int32, (cs, LANE), 1) < M_HEADDIM
    for s in range(tb // cs):
        r = slice(s * cs, (s + 1) * cs)
        dtv = _softplus(dt_scr[r, :] + dtb_ref[...])
        cum = _cumsum_rows(dtv * a_row, cs)
        cum_t = cum.T
        cbs = {}
        for p in range(M_HEADS // 2):
            grp = p // (M_HEADS // 2 // M_GROUPS)
            cl = slice(p * LANE, (p + 1) * LANE)
            bm = bc_scr[r, grp * M_DSTATE:(grp + 1) * M_DSTATE]
            cm = bc_scr[r, (M_GROUPS + grp) * M_DSTATE:(M_GROUPS + grp + 1) * M_DSTATE]
            if grp not in cbs:
                cbs[grp] = _dot_nt(cm, bm)
            cb = cbs[grp]
            cum_h = [_lane_bcast(cum, hh) for hh in (2 * p, 2 * p + 1)]
            cum_x = jnp.where(lane_lo, cum_h[0], cum_h[1])
            dt_x = jnp.where(lane_lo, _lane_bcast(dtv, 2 * p), _lane_bcast(dtv, 2 * p + 1))
            last_x = cum_x[cs - 1:cs, :]
            xs = xs_scr[r, cl]
            xdt = xs * dt_x
            ms = []
            for i, hh in enumerate((2 * p, 2 * p + 1)):
                seg = cum_h[i] - cum_t[hh:hh + 1, :]
                ms.append((cb * jnp.exp(jnp.where(tril_m, seg, NEG_BIG))).astype(BF16))
            x_lo = jnp.where(lane_lo, xdt, 0.0).astype(BF16)
            x_hi = jnp.where(lane_lo, 0.0, xdt).astype(BF16)
            y = _dot(jnp.concatenate(ms, axis=1), jnp.concatenate([x_lo, x_hi], axis=0))
            st = stm_scr[:, cl]
            y = y + _dot(cm, st.astype(BF16)) * jnp.exp(cum_x)
            y = y + dsk_ref[:, cl] * xs
            to_end = (xdt * jnp.exp(last_x - cum_x)).astype(BF16)
            stm_scr[:, cl] = jnp.exp(last_x) * st + _dot_tn(bm, to_end)
            y_scr[r, cl] = y * _silu(z_scr[r, cl])
        for grp in range(M_GROUPS):
            gl = slice(grp * M_GROUP_WIDTH, (grp + 1) * M_GROUP_WIDTH)
            mixm_ref[0, r, gl] = _rms(y_scr[r, gl], mn_ref[:, gl]).astype(BF16)


def _prompt_mixer_kernel(x_ref, lbl_ref, ln1_ref, w_ref, hgn_ref, cw_ref, cb_ref, dtb_ref, alog_ref, dsk_ref, mn_ref,
                         mixh_ref, mixm_ref, hgs_ref, ssm_ref, conv_ref,
                         h_scr, hg_scr, z_scr, xbc_scr, dt_scr, xs_scr, bc_scr, y_scr, sth_scr, stm_scr, hist_scr,
                         *, tb, ch, cs):
    c = pl.program_id(1)
    last_c = pl.num_programs(1) - 1

    @pl.when(c == 0)
    def _():
        sth_scr[...] = jnp.zeros(sth_scr.shape, F32)
        stm_scr[...] = jnp.zeros(stm_scr.shape, F32)
        hist_scr[...] = jnp.zeros(hist_scr.shape, F32)

    bufs = (h_scr, hg_scr, z_scr, xbc_scr, dt_scr)
    _in_proj(x_ref[0], ln1_ref, w_ref, bufs, tb)
    prm = (lbl_ref, hgn_ref, cw_ref, cb_ref, dtb_ref, alog_ref, dsk_ref, mn_ref)
    _recurrences(bufs, prm, mixh_ref, mixm_ref, (xs_scr, bc_scr, y_scr), (sth_scr, stm_scr, hist_scr), tb, ch, cs)

    @pl.when(c == last_c)
    def _():
        for hd in range(HG_HEADS):
            hgs_ref[0, hd] = sth_scr[hd].T
        for p in range(M_WIDTH // LANE):
            ssm_ref[0, p * LANE:(p + 1) * LANE, :] = stm_scr[:, p * LANE:(p + 1) * LANE].T
        conv_ref[0] = hist_scr[SUBLANE - (M_CONV - 1):SUBLANE, :]


def _const_spec(shape):
    nd = len(shape)
    return pl.BlockSpec(shape, lambda *_: (0,) * nd, pipeline_mode=pl.Buffered(1))


def _prompt_mixer(x, lbl, ln1, w_all, hgn, cw, cb, dtb, alog, dsk, mn, *, tb=256, ch=64, cs=128):
    bsz, seq, _ = x.shape
    nc = seq // tb
    kern = functools.partial(_prompt_mixer_kernel, tb=tb, ch=ch, cs=cs)
    params = (lbl, ln1, w_all, hgn, cw, cb, dtb, alog, dsk, mn)
    return pl.pallas_call(
        kern,
        grid=(bsz, nc),
        in_specs=[pl.BlockSpec((1, tb, D_MODEL), lambda b, c: (b, c, 0))] + [_const_spec(p.shape) for p in params],
        out_specs=[
            pl.BlockSpec((1, tb, HG_WIDTH), lambda b, c: (b, c, 0)),
            pl.BlockSpec((1, tb, M_WIDTH), lambda b, c: (b, c, 0)),
            pl.BlockSpec((1, HG_HEADS, HG_DK, HG_DV), lambda b, c: (b, 0, 0, 0)),
            pl.BlockSpec((1, M_WIDTH, M_DSTATE), lambda b, c: (b, 0, 0)),
            pl.BlockSpec((1, M_CONV - 1, M_CONV_DIM), lambda b, c: (b, 0, 0)),
        ],
        out_shape=[
            jax.ShapeDtypeStruct((bsz, seq, HG_WIDTH), BF16),
            jax.ShapeDtypeStruct((bsz, seq, M_WIDTH), BF16),
            jax.ShapeDtypeStruct((bsz, HG_HEADS, HG_DK, HG_DV), F32),
            jax.ShapeDtypeStruct((bsz, M_WIDTH, M_DSTATE), F32),
            jax.ShapeDtypeStruct((bsz, M_CONV - 1, M_CONV_DIM), F32),
        ],
        scratch_shapes=[
            pltpu.VMEM((tb, D_MODEL), BF16),
            pltpu.VMEM((tb, HG_PROJ_WIDTH), F32),
            pltpu.VMEM((tb, M_WIDTH), F32),
            pltpu.VMEM((tb + SUBLANE, M_CONV_DIM), F32),
            pltpu.VMEM((tb, LANE), F32),
            pltpu.VMEM((tb, M_WIDTH), F32),
            pltpu.VMEM((tb, 2 * M_GROUPS * M_DSTATE), BF16),
            pltpu.VMEM((tb, M_WIDTH), F32),
            pltpu.VMEM((HG_HEADS, HG_DV, HG_DK), F32),
            pltpu.VMEM((M_DSTATE, M_WIDTH), F32),
            pltpu.VMEM((SUBLANE, M_CONV_DIM), F32),
        ],
        compiler_params=pltpu.CompilerParams(
            dimension_semantics=("arbitrary", "arbitrary"), vmem_limit_bytes=VMEM_LIMIT),
        name="prompt_mixer",
    )(x, *params)


def _sample_in_proj_kernel(x_ref, ln1_ref, w_ref, o_ref, h_scr):
    @pl.when(pl.program_id(0) == 0)
    def _():
        h_scr[...] = _rms(x_ref[...], ln1_ref[...]).astype(BF16)

    o_ref[...] = _dot(h_scr[...], w_ref[...])


def _sample_in_proj(x, ln1, w_all, *, tn=512):
    rows = x.shape[0]
    return pl.pallas_call(
        _sample_in_proj_kernel,
        grid=(W_IN_COLS // tn,),
        in_specs=[_const_spec((rows, D_MODEL)), _const_spec(ln1.shape),
                  pl.BlockSpec((D_MODEL, tn), lambda n: (0, n))],
        out_specs=pl.BlockSpec((rows, tn), lambda n: (0, n)),
        out_shape=jax.ShapeDtypeStruct((rows, W_IN_COLS), F32),
        scratch_shapes=[pltpu.VMEM((rows, D_MODEL), BF16)],
        compiler_params=pltpu.CompilerParams(
            dimension_semantics=("arbitrary",), vmem_limit_bytes=VMEM_LIMIT),
        name="sample_in_proj",
    )(x, ln1, w_all)


SEG = 8


def _block_causal_mask(rows):
    ri = lax.broadcasted_iota(jnp.int32, (rows, rows), 0)
    ci = lax.broadcasted_iota(jnp.int32, (rows, rows), 1)
    return ((ri & -SEG) == (ci & -SEG)) & (ci <= ri)


def _seg_last(x):
    rows, width = x.shape
    x3 = x.reshape(rows // SEG, SEG, width)
    return jnp.broadcast_to(x3[:, SEG - 1:SEG, :], x3.shape).reshape(rows, width)


def _decay_tail(dec8):
    h, m, l = _split3(dec8)
    t = lax.broadcasted_iota(jnp.int32, dec8.shape, 0)
    return jnp.where(t == 0, h.astype(F32), jnp.where(t == 1, m.astype(F32), jnp.where(t == 2, l.astype(F32), 0.0)))


def _ones_tail():
    t = lax.broadcasted_iota(jnp.int32, (2 * SEG, LANE), 0)
    return jnp.where((t >= SEG) & (t < SEG + 3), 1.0, 0.0).astype(BF16)


def _sample_hgrn_kernel(p_ref, s0_ref, lbl_ref, hgn_ref, mixh_ref, s8_ref,
                        qd_scr, ke_scr, dec_scr, o_scr, *, nb):
    rows = nb * SEG
    lb = _lower_bound(lbl_ref[...], 0)
    mask = _block_causal_mask(rows)
    for hd in range(HG_HEADS):
        cl = slice(hd * LANE, (hd + 1) * LANE)
        q = p_ref[:, OFF_Q + hd * LANE:OFF_Q + (hd + 1) * LANE]
        fr = p_ref[:, OFF_F + hd * LANE:OFF_F + (hd + 1) * LANE]
        v = p_ref[:, OFF_I + hd * LANE:OFF_I + (hd + 1) * LANE]
        lbh = lb[:, cl]
        f = _forget_gate(fr, lbh)
        a = jnp.log(f)
        k = 1.0 - f
        cum = _cumsum_rows(a, SEG)
        last = _seg_last(cum)
        qd = (q * jnp.exp(cum)).astype(BF16)
        kd = (k * jnp.exp(-cum)).astype(BF16)
        att = jnp.where(mask, _dot_nt(qd, kd), 0.0).astype(BF16)
        o_scr[:, cl] = _dot(att, v.astype(BF16))
        qd_scr[:, cl] = qd
        ke_scr[:, cl] = k * jnp.exp(last - cum)
        dec_scr[:, cl] = jnp.exp(last)

    ones_tail = _ones_tail()
    zeros8 = jnp.zeros((SEG, LANE), F32)

    def pair_body(m, carry):
        r16 = pl.ds(pl.multiple_of(m * 2 * SEG, 2 * SEG), 2 * SEG)
        for par in range(2):
            i = 2 * m + par
            r8 = pl.ds(pl.multiple_of(i * SEG, SEG), SEG)
            for hd in range(HG_HEADS):
                cl = slice(hd * LANE, (hd + 1) * LANE)
                s0 = s0_ref[i, hd]
                oi = _dot(qd_scr[r16, cl], s0.astype(BF16))
                o_scr[r8, cl] = o_scr[r8, cl] + oi[par * SEG:(par + 1) * SEG, :]
                aug = jnp.concatenate([ke_scr[r8, cl], _decay_tail(dec_scr[r8, cl])], axis=0).astype(BF16)
                v8 = p_ref[r8, OFF_I + hd * LANE:OFF_I + (hd + 1) * LANE]
                rhs = jnp.concatenate([jnp.concatenate([v8, zeros8], axis=0).astype(BF16), ones_tail], axis=1)
                ud = _dot_tn(aug, rhs)
                s8_ref[i, hd] = ud[:, LANE:] * s0 + ud[:, :LANE]
        return carry

    lax.fori_loop(0, nb // 2, pair_body, 0, unroll=True)

    for hd in range(HG_HEADS):
        cl = slice(hd * LANE, (hd + 1) * LANE)
        g = p_ref[:, OFF_G + hd * LANE:OFF_G + (hd + 1) * LANE]
        mixh_ref[:, cl] = (_rms(o_scr[:, cl], hgn_ref[:, cl]) * _silu(g)).astype(BF16)


def _sample_hgrn(proj, s0, lbl, hgn, *, nb=16):
    nseq = s0.shape[0]
    rows = nb * SEG
    kern = functools.partial(_sample_hgrn_kernel, nb=nb)
    return pl.pallas_call(
        kern,
        grid=(nseq // nb,),
        in_specs=[pl.BlockSpec((rows, OFF_XBC), lambda i: (i, 0)),
                  pl.BlockSpec((nb, HG_HEADS, HG_DK, HG_DV), lambda i: (i, 0, 0, 0)),
                  _const_spec(lbl.shape), _const_spec(hgn.shape)],
        out_specs=[pl.BlockSpec((rows, HG_WIDTH), lambda i: (i, 0)),
                   pl.BlockSpec((nb, HG_HEADS, HG_DK, HG_DV), lambda i: (i, 0, 0, 0))],
        out_shape=[jax.ShapeDtypeStruct((nseq * SEG, HG_WIDTH), BF16),
                   jax.ShapeDtypeStruct(s0.shape, F32)],
        scratch_shapes=[pltpu.VMEM((rows, HG_WIDTH), BF16),
                        pltpu.VMEM((rows, HG_WIDTH), F32),
                        pltpu.VMEM((rows, HG_WIDTH), F32),
                        pltpu.VMEM((rows, HG_WIDTH), F32)],
        compiler_params=pltpu.CompilerParams(
            dimension_semantics=("arbitrary",), vmem_limit_bytes=VMEM_LIMIT),
        name="sample_hgrn",
    )(proj, s0, lbl, hgn)


def _sample_ssd_kernel(xd_ref, z_ref, c0_ref, s0_ref, cw_ref, cb_ref, dtb_ref, alog_ref, dsk_ref, mn_ref,
                       mixm_ref, s8_ref, c8_ref,
                       hist_scr, raw_scr, xs_scr, bm_scr, cm_scr, u_scr, cd_scr, fs_scr, y_scr, *, nb):
    rows = nb * SEG
    t_in_seq = lax.broadcasted_iota(jnp.int32, (rows, LANE), 0) & (SEG - 1)

    @pl.when(pl.program_id(0) == 0)
    def _():
        hist_scr[...] = jnp.zeros(hist_scr.shape, F32)

    for cbk in range(M_CONV_DIM // LANE):
        cl = slice(cbk * LANE, (cbk + 1) * LANE)
        x = xd_ref[:, cl]
        raw_scr[cbk] = x
        for d in range(M_CONV - 1):
            seq_rows = pl.ds(SEG - (M_CONV - 1) + d, nb, stride=SEG)
            hist_scr[cbk, seq_rows, :] = c0_ref[d, :, cl]
            c8_ref[d, :, cl] = raw_scr[cbk, seq_rows, :]
        hist = hist_scr[cbk]
        acc = cb_ref[:, cl] + cw_ref[M_CONV - 1:M_CONV, cl] * x
        for d in range(1, M_CONV):
            shifted = jnp.where(t_in_seq >= d, pltpu.roll(x, d, 0), pltpu.roll(hist, rows - SEG + d, 0))
            acc = acc + cw_ref[M_CONV - 1 - d:M_CONV - d, cl] * shifted
        act = _silu(acc)
        if cbk < M_WIDTH // LANE:
            xs_scr[:, cl] = act
        elif cbk < (M_WIDTH + M_GROUPS * M_DSTATE) // LANE:
            bm_scr[:, cbk * LANE - M_WIDTH:(cbk + 1) * LANE - M_WIDTH] = act
        else:
            o0 = cbk * LANE - M_WIDTH - M_GROUPS * M_DSTATE
            cm_scr[:, o0:o0 + LANE] = act.astype(BF16)

    a_row = -jnp.exp(alog_ref[...])
    dtv = _softplus(xd_ref[:, M_CONV_DIM:M_CONV_DIM + LANE] + dtb_ref[...])
    cum = _cumsum_rows(dtv * a_row, SEG)
    cum_t = cum.T
    mask = _block_causal_mask(rows)
    lane_lo = lax.broadcasted_iota(jnp.int32, (rows, LANE), 1) < M_HEADDIM
    cbs = {}
    for p in range(M_HEADS // 2):
        grp = p // (M_HEADS // 2 // M_GROUPS)
        cl = slice(p * LANE, (p + 1) * LANE)
        if grp not in cbs:
            bm = bm_scr[:, grp * M_DSTATE:(grp + 1) * M_DSTATE].astype(BF16)
            cbs[grp] = _dot_nt(cm_scr[:, grp * M_DSTATE:(grp + 1) * M_DSTATE], bm)
        cb = cbs[grp]
        cum_h = [_lane_bcast(cum, hh) for hh in (2 * p, 2 * p + 1)]
        cum_x = jnp.where(lane_lo, cum_h[0], cum_h[1])
        dt_x = jnp.where(lane_lo, _lane_bcast(dtv, 2 * p), _lane_bcast(dtv, 2 * p + 1))
        last_x = _seg_last(cum_x)
        xs = xs_scr[:, cl]
        xdt = xs * dt_x
        ms = []
        for i, hh in enumerate((2 * p, 2 * p + 1)):
            seg = cum_h[i] - cum_t[hh:hh + 1, :]
            ms.append((cb * jnp.exp(jnp.where(mask, seg, NEG_BIG))).astype(BF16))
        x_lo = jnp.where(lane_lo, xdt, 0.0).astype(BF16)
        x_hi = jnp.where(lane_lo, 0.0, xdt).astype(BF16)
        y = _dot(jnp.concatenate(ms, axis=1), jnp.concatenate([x_lo, x_hi], axis=0))
        y_scr[:, cl] = y + dsk_ref[:, cl] * xs
        u_scr[:, cl] = xdt * jnp.exp(last_x - cum_x)
        cd_scr[:, cl] = jnp.exp(last_x)
        fs_scr[:, cl] = jnp.exp(cum_x)

    ones_tail = _ones_tail()
    zeros8 = jnp.zeros((SEG, M_DSTATE), F32)

    def pair_body(m, carry):
        r16 = pl.ds(pl.multiple_of(m * 2 * SEG, 2 * SEG), 2 * SEG)
        for par in range(2):
            i = 2 * m + par
            r8 = pl.ds(pl.multiple_of(i * SEG, SEG), SEG)
            for grp in range(M_GROUPS):
                gl = slice(grp * M_GROUP_WIDTH, (grp + 1) * M_GROUP_WIDTH)
                nl = slice(grp * M_DSTATE, (grp + 1) * M_DSTATE)
                s0 = s0_ref[i, gl, :]
                yi = _dot_nt(cm_scr[r16, nl], s0.astype(BF16))
                y_scr[r8, gl] = y_scr[r8, gl] + yi[par * SEG:(par + 1) * SEG, :] * fs_scr[r8, gl]
                aug = jnp.concatenate([u_scr[r8, gl], _decay_tail(cd_scr[r8, gl])], axis=0).astype(BF16)
                rhs = jnp.concatenate(
                    [jnp.concatenate([bm_scr[r8, nl], zeros8], axis=0).astype(BF16), ones_tail], axis=1)
                ud = _dot_tn(aug, rhs)
                s8_ref[i, gl, :] = ud[:, M_DSTATE:] * s0 + ud[:, :M_DSTATE]
        return carry

    lax.fori_loop(0, nb // 2, pair_body, 0, unroll=True)

    for grp in range(M_GROUPS):
        gl = slice(grp * M_GROUP_WIDTH, (grp + 1) * M_GROUP_WIDTH)
        y = y_scr[:, gl] * _silu(z_ref[:, gl])
        mixm_ref[:, gl] = _rms(y, mn_ref[:, gl]).astype(BF16)


def _sample_ssd(proj, c0, s0, cw, cb, dtb, alog, dsk, mn, *, nb=16):
    nseq = s0.shape[0]
    rows = nb * SEG
    kern = functools.partial(_sample_ssd_kernel, nb=nb)
    params = (cw, cb, dtb, alog, dsk, mn)
    conv_spec = pl.BlockSpec((M_CONV - 1, nb, M_CONV_DIM), lambda i: (0, i, 0))
    return pl.pallas_call(
        kern,
        grid=(nseq // nb,),
        in_specs=[pl.BlockSpec((rows, XD_WIDTH), lambda i: (i, OFF_XBC // XD_WIDTH)),
                  pl.BlockSpec((rows, M_WIDTH), lambda i: (i, OFF_Z // M_WIDTH)),
                  conv_spec,
                  pl.BlockSpec((nb, M_WIDTH, M_DSTATE), lambda i: (i, 0, 0))]
                 + [_const_spec(p.shape) for p in params],
        out_specs=[pl.BlockSpec((rows, M_WIDTH), lambda i: (i, 0)),
                   pl.BlockSpec((nb, M_WIDTH, M_DSTATE), lambda i: (i, 0, 0)),
                   conv_spec],
        out_shape=[jax.ShapeDtypeStruct((nseq * SEG, M_WIDTH), BF16),
                   jax.ShapeDtypeStruct(s0.shape, F32),
                   jax.ShapeDtypeStruct(c0.shape, F32)],
        scratch_shapes=[pltpu.VMEM((M_CONV_DIM // LANE, rows, LANE), F32),
                        pltpu.VMEM((M_CONV_DIM // LANE, rows, LANE), F32),
                        pltpu.VMEM((rows, M_WIDTH), F32),
                        pltpu.VMEM((rows, M_GROUPS * M_DSTATE), F32),
                        pltpu.VMEM((rows, M_GROUPS * M_DSTATE), BF16),
                        pltpu.VMEM((rows, M_WIDTH), F32),
                        pltpu.VMEM((rows, M_WIDTH), F32),
                        pltpu.VMEM((rows, M_WIDTH), F32),
                        pltpu.VMEM((rows, M_WIDTH), F32)],
        compiler_params=pltpu.CompilerParams(
            dimension_semantics=("arbitrary",), vmem_limit_bytes=VMEM_LIMIT),
        name="sample_ssd",
    )(proj, proj, c0, s0, *params)


def _out_mlp_kernel(x_ref, mh_ref, mm_ref, wo_ref, ln2_ref, wu_ref, wd_ref, lnf_ref, o_ref, *, ff_tile):
    x1 = x_ref[...] + _dot(mh_ref[...], wo_ref[0:HG_WIDTH, :]) + _dot(mm_ref[...], wo_ref[HG_WIDTH:, :])
    hn = _rms(x1, ln2_ref[...]).astype(BF16)
    mlp = None
    for j in range(D_FF // ff_tile):
        u = jnp.maximum(_dot(hn, wu_ref[:, j * ff_tile:(j + 1) * ff_tile]), 0.0)
        d = _dot((u * u).astype(BF16), wd_ref[j * ff_tile:(j + 1) * ff_tile, :])
        mlp = d if mlp is None else mlp + d
    o_ref[...] = _rms(x1 + mlp, lnf_ref[...])


def _out_mlp(x, mix_h, mix_m, w_out, ln2, w_up, w_down, ln_f, *, tm=512, ff_tile=1024):
    rows = x.shape[0]
    kern = functools.partial(_out_mlp_kernel, ff_tile=ff_tile)
    row_spec = lambda w: pl.BlockSpec((tm, w), lambda i: (i, 0))
    return pl.pallas_call(
        kern,
        grid=(rows // tm,),
        in_specs=[row_spec(D_MODEL), row_spec(HG_WIDTH), row_spec(M_WIDTH),
                  _const_spec(w_out.shape), _const_spec(ln2.shape), _const_spec(w_up.shape),
                  _const_spec(w_down.shape), _const_spec(ln_f.shape)],
        out_specs=row_spec(D_MODEL),
        out_shape=jax.ShapeDtypeStruct((rows, D_MODEL), F32),
        compiler_params=pltpu.CompilerParams(
            dimension_semantics=("arbitrary",), vmem_limit_bytes=VMEM_LIMIT),
        name="out_mlp",
    )(x, mix_h, mix_m, w_out, ln2, w_up, w_down, ln_f)


def _pad_lanes(v):
    return jnp.pad(v, ((0, 0), (0, LANE - v.shape[1])))


def kernel(x_prompt, x_sample, state_hgrn, state_ssm, state_conv, hg_lb_logits, ln1, w_in, hg_norm, conv_w,
           conv_b, dt_bias, a_log, d_skip, m_norm, w_out, ln2, w_up, w_down, ln_f):
    l = 0
    bp, seq, _ = x_prompt.shape
    bs, dseq, _ = x_sample.shape
    assert DEPTH == 1 and dseq == SEG

    w_all = _pack_in_proj(jnp.transpose(w_in[l]))
    lbl = hg_lb_logits.astype(F32)
    ln1_r = ln1[l][None, :]
    hgn_r = hg_norm[l].reshape(1, HG_WIDTH)
    cw = conv_w[l]
    cb_r = conv_b[l][None, :]
    dtb_r = _pad_lanes(dt_bias[l][None, :])
    alog_r = _pad_lanes(a_log[l][None, :])
    dsk_r = jnp.repeat(d_skip[l], M_HEADDIM)[None, :]
    mn_r = m_norm[l][None, :]
    mlp_w = (w_out[l].astype(BF16), ln2[l][None, :], w_up[l].astype(BF16), w_down[l].astype(BF16), ln_f[None, :])

    mixh_p, mixm_p, hgs_p, ssm_p, conv_p = _prompt_mixer(
        x_prompt, lbl, ln1_r, w_all, hgn_r, cw, cb_r, dtb_r, alog_r, dsk_r, mn_r)
    y_p = _out_mlp(x_prompt.reshape(bp * seq, D_MODEL), mixh_p.reshape(bp * seq, HG_WIDTH),
                   mixm_p.reshape(bp * seq, M_WIDTH), *mlp_w)

    xs2 = x_sample.reshape(bs * SEG, D_MODEL)
    proj_s = _sample_in_proj(xs2, ln1_r, w_all)
    mixh_s, hgs_s = _sample_hgrn(proj_s, state_hgrn[l], lbl, hgn_r)
    conv0 = jnp.transpose(state_conv[l], (1, 0, 2))
    mixm_s, ssm_s, conv8 = _sample_ssd(proj_s, conv0, state_ssm[l].reshape(bs, M_WIDTH, M_DSTATE),
                                       cw, cb_r, dtb_r, alog_r, dsk_r, mn_r)
    y_s = _out_mlp(xs2, mixh_s, mixm_s, *mlp_w)
    conv_s = jnp.transpose(conv8, (1, 0, 2))

    return (y_p.reshape(bp, seq, D_MODEL), y_s.reshape(bs, SEG, D_MODEL),
            hgs_p[None], hgs_s[None],
            ssm_p.reshape(1, bp, M_HEADS, M_HEADDIM, M_DSTATE), ssm_s.reshape(1, bs, M_HEADS, M_HEADDIM, M_DSTATE),
            conv_p[None], conv_s[None])
```

```python
import functools

import jax
import jax.numpy as jnp
from jax import lax
from jax.experimental import pallas as pl
from jax.experimental.pallas import tpu as pltpu

F32 = jnp.float32
BF16 = jnp.bfloat16

D_MODEL = 1024
DEPTH = 1
HG_HEADS = 8
HG_DK = 128
HG_DV = 128
HG_WIDTH = HG_HEADS * HG_DV
M_WIDTH = 1024
M_HEADDIM = 64
M_HEADS = M_WIDTH // M_HEADDIM
M_DSTATE = 128
M_GROUPS = 2
M_GROUP_WIDTH = M_WIDTH // M_GROUPS
M_CONV = 4
M_CONV_DIM = M_WIDTH + 2 * M_GROUPS * M_DSTATE
D_FF = 4 * D_MODEL
NORM_EPS = 1e-5

LANE = 128
SUBLANE = 8

OFF_Q, OFF_F, OFF_I, OFF_G = 0, 1024, 2048, 3072
HG_PROJ_WIDTH = 4096
OFF_XBC = HG_PROJ_WIDTH
OFF_DT = OFF_XBC + M_CONV_DIM
XD_WIDTH = 2048
OFF_Z = OFF_XBC + XD_WIDTH
W_IN_COLS = OFF_Z + M_WIDTH
PACK_TILE = 1024

VMEM_LIMIT = 56 * 1024 * 1024

NEG_BIG = -1e30


def _dot(a, b):
    return jnp.dot(a, b, preferred_element_type=F32)


def _dot_nt(a, b):
    return lax.dot_general(a, b, (((1,), (1,)), ((), ())), preferred_element_type=F32)


def _dot_tn(a, b):
    return lax.dot_general(a, b, (((0,), (0,)), ((), ())), preferred_element_type=F32)


def _silu(x):
    hx = 0.5 * x
    return hx + hx * jnp.tanh(hx)


def _forget_gate(fr, lb):
    half = 0.5 * (1.0 - lb)
    return (lb + half) + half * jnp.tanh(0.5 * fr)


def _softplus(x):
    return jnp.maximum(x, 0.0) + jnp.log(1.0 + jnp.exp(-jnp.abs(x)))


def _rms(x, gain):
    ms = jnp.mean(x * x, axis=-1, keepdims=True)
    return x * lax.rsqrt(ms + NORM_EPS) * gain


def _lower_bound(lbl, layer):
    rows = [lbl[i:i + 1, :] for i in range(DEPTH + 1)]
    m = functools.reduce(jnp.maximum, rows)
    es = [jnp.exp(r - m) for r in rows]
    return sum(es[:layer + 1]) / sum(es)


def _cumsum_rows(x, seg):
    t = lax.broadcasted_iota(jnp.int32, x.shape, 0) & (seg - 1)
    s = 1
    while s < seg:
        x = x + jnp.where(t >= s, pltpu.roll(x, s, 0), 0.0)
        s *= 2
    return x


def _split3(x):
    h = x.astype(BF16)
    r = x - h.astype(F32)
    m = r.astype(BF16)
    l = (r - m.astype(F32)).astype(BF16)
    return h, m, l


def _lane_bcast(x, lane):
    return jnp.broadcast_to(x[:, lane:lane + 1], x.shape)


def _pack_in_proj_kernel(wt_ref, o_ref, *, tn, n_hg, n_xd, src_cols):
    j = pl.program_id(0)
    col0 = _pack_src_tile(j, n_hg, n_xd) * tn
    col = col0 + lax.broadcasted_iota(jnp.int32, wt_ref.shape, 0)
    o_ref[...] = jnp.where(col < src_cols, wt_ref[...], 0.0).T.astype(BF16)


def _pack_src_tile(j, n_hg, n_xd):
    n_z = M_WIDTH // PACK_TILE
    return jnp.where(j < n_hg, j, jnp.where(j < n_hg + n_xd, j + n_z, j - n_xd))


def _pack_in_proj(wt):
    tn = PACK_TILE
    n_hg, n_xd = HG_PROJ_WIDTH // tn, XD_WIDTH // tn
    kern = functools.partial(_pack_in_proj_kernel, tn=tn, n_hg=n_hg, n_xd=n_xd, src_cols=wt.shape[0])
    return pl.pallas_call(
        kern,
        grid=(W_IN_COLS // tn,),
        in_specs=[pl.BlockSpec((tn, D_MODEL), lambda j: (_pack_src_tile(j, n_hg, n_xd), 0))],
        out_specs=pl.BlockSpec((D_MODEL, tn), lambda j: (0, j)),
        out_shape=jax.ShapeDtypeStruct((D_MODEL, W_IN_COLS), BF16),
        compiler_params=pltpu.CompilerParams(
            dimension_semantics=("arbitrary",), vmem_limit_bytes=VMEM_LIMIT),
        name="pack_in_proj",
    )(wt)


def _in_proj(x, ln1_ref, w_ref, bufs, tb):
    h_scr, hg_scr, z_scr, xbc_scr, dt_scr = bufs
    h_scr[...] = _rms(x, ln1_ref[...]).astype(BF16)
    for n0 in range(0, HG_PROJ_WIDTH, 1024):
        hg_scr[:, n0:n0 + 1024] = _dot(h_scr[...], w_ref[:, n0:n0 + 1024])
    xbc_scr[SUBLANE:SUBLANE + tb, :] = _dot(h_scr[...], w_ref[:, OFF_XBC:OFF_DT])
    dt_scr[...] = _dot(h_scr[...], w_ref[:, OFF_DT:OFF_DT + LANE])
    z_scr[...] = _dot(h_scr[...], w_ref[:, OFF_Z:OFF_Z + M_WIDTH])


def _recurrences(bufs, prm, mixh_ref, mixm_ref, work, state, tb, ch, cs):
    _, hg_scr, z_scr, xbc_scr, dt_scr = bufs
    lbl_ref, hgn_ref, cw_ref, cb_ref, dtb_ref, alog_ref, dsk_ref, mn_ref = prm
    xs_scr, bc_scr, y_scr = work
    sth_scr, stm_scr, hist_scr = state

    xbc_scr[0:SUBLANE, :] = hist_scr[...]

    lb = _lower_bound(lbl_ref[...], 0)
    tril_h = (lax.broadcasted_iota(jnp.int32, (ch, ch), 0) >= lax.broadcasted_iota(jnp.int32, (ch, ch), 1))
    for j in range(tb // ch):
        r = slice(j * ch, (j + 1) * ch)
        for hd in range(HG_HEADS):
            cl = slice(hd * LANE, (hd + 1) * LANE)
            q = hg_scr[r, OFF_Q + hd * LANE:OFF_Q + (hd + 1) * LANE]
            fr = hg_scr[r, OFF_F + hd * LANE:OFF_F + (hd + 1) * LANE]
            v = hg_scr[r, OFF_I + hd * LANE:OFF_I + (hd + 1) * LANE]
            g = hg_scr[r, OFF_G + hd * LANE:OFF_G + (hd + 1) * LANE]
            lbh = lb[:, cl]
            f = _forget_gate(fr, lbh)
            a = jnp.log(f)
            k = 1.0 - f
            cum = _cumsum_rows(a, ch)
            mid = cum[ch // 2 - 1:ch // 2, :]
            last = cum[ch - 1:ch, :]
            qd_mid = q * jnp.exp(cum - mid)
            kd_mid = k * jnp.exp(mid - cum)
            qd = (qd_mid * jnp.exp(mid)).astype(BF16)
            k_end = (kd_mid * jnp.exp(last - mid)).astype(BF16)
            dec = jnp.exp(last)
            vb = v.astype(BF16)
            att = _dot_nt(qd_mid.astype(BF16), kd_mid.astype(BF16))
            att = jnp.where(tril_h, att, 0.0).astype(BF16)
            st = sth_scr[hd]
            o = _dot(att, vb) + _dot_nt(qd, st.astype(BF16))
            sth_scr[hd] = dec * st + _dot_tn(vb, k_end)
            on = _rms(o, hgn_ref[:, cl]) * _silu(g)
            mixh_ref[0, r, cl] = on.astype(BF16)

    for cbk in range(M_CONV_DIM // LANE):
        cl = slice(cbk * LANE, (cbk + 1) * LANE)
        acc = cb_ref[:, cl] + cw_ref[M_CONV - 1:M_CONV, cl] * xbc_scr[SUBLANE:SUBLANE + tb, cl]
        for d in range(1, M_CONV):
            acc = acc + cw_ref[M_CONV - 1 - d:M_CONV - d, cl] * xbc_scr[SUBLANE - d:SUBLANE - d + tb, cl]
        act = _silu(acc)
        if cbk < M_WIDTH // LANE:
            xs_scr[:, cl] = act
        else:
            bc_scr[:, (cbk * LANE - M_WIDTH):(cbk * LANE - M_WIDTH) + LANE] = act.astype(BF16)
    hist_scr[...] = xbc_scr[tb:tb + SUBLANE, :]

    a_row = -jnp.exp(alog_ref[...])
    tril_m = (lax.broadcasted_iota(jnp.int32, (cs, cs), 0) >= lax.broadcasted_iota(jnp.int32, (cs, cs), 1))
    lane_lo = lax.broadcasted_iota(jnp.int32, (cs, LANE), 1) < M_HEADDIM
    for s in range(tb // cs):
        r = slice(s * cs, (s + 1) * cs)
        dtv = _softplus(dt_scr[r, :] + dtb_ref[...])
        cum = _cumsum_rows(dtv * a_row, cs)
        cum_t = cum.T
        cbs = {}
        for p in range(M_HEADS // 2):
            grp = p // (M_HEADS // 2 // M_GROUPS)
            cl = slice(p * LANE, (p + 1) * LANE)
            bm = bc_scr[r, grp * M_DSTATE:(grp + 1) * M_DSTATE]
            cm = bc_scr[r, (M_GROUPS + grp) * M_DSTATE:(M_GROUPS + grp + 1) * M_DSTATE]
            if grp not in cbs:
                cbs[grp] = _dot_nt(cm, bm)
            cb = cbs[grp]
            cum_h = [_lane_bcast(cum, hh) for hh in (2 * p, 2 * p + 1)]
            cum_x = jnp.where(lane_lo, cum_h[0], cum_h[1])
            dt_x = jnp.where(lane_lo, _lane_bcast(dtv, 2 * p), _lane_bcast(dtv, 2 * p + 1))
            last_x = cum_x[cs - 1:cs, :]
            xs = xs_scr[r, cl]
            xdt = xs * dt_x
            ms = []
            for i, hh in enumerate((2 * p, 2 * p + 1)):
                seg = cum_h[i] - cum_t[hh:hh + 1, :]
                ms.append((cb * jnp.exp(jnp.where(tril_m, seg, NEG_BIG))).astype(BF16))
            x_lo = jnp.where(lane_lo, xdt, 0.0).astype(BF16)
            x_hi = jnp.where(lane_lo, 0.0, xdt).astype(BF16)
            y = _dot(jnp.concatenate(ms, axis=1), jnp.concatenate([x_lo, x_hi], axis=0))
            st = stm_scr[:, cl]
            y = y + _dot(cm, st.astype(BF16)) * jnp.exp(cum_x)
            y = y + dsk_ref[:, cl] * xs
            to_end = (xdt * jnp.exp(last_x - cum_x)).astype(BF16)
            stm_scr[:, cl] = jnp.exp(last_x) * st + _dot_tn(bm, to_end)
            y_scr[r, cl] = y * _silu(z_scr[r, cl])
        for grp in range(M_GROUPS):
            gl = slice(grp * M_GROUP_WIDTH, (grp + 1) * M_GROUP_WIDTH)
            mixm_ref[0, r, gl] = _rms(y_scr[r, gl], mn_ref[:, gl]).astype(BF16)


def _prompt_mixer_kernel(x_ref, lbl_ref, ln1_ref, w_ref, hgn_ref, cw_ref, cb_ref, dtb_ref, alog_ref, dsk_ref, mn_ref,
                         mixh_ref, mixm_ref, hgs_ref, ssm_ref, conv_ref,
                         h_scr, hg_scr, z_scr, xbc_scr, dt_scr, xs_scr, bc_scr, y_scr, sth_scr, stm_scr, hist_scr,
                         *, tb, ch, cs):
    c = pl.program_id(1)
    last_c = pl.num_programs(1) - 1

    @pl.when(c == 0)
    def _():
        sth_scr[...] = jnp.zeros(sth_scr.shape, F32)
        stm_scr[...] = jnp.zeros(stm_scr.shape, F32)
        hist_scr[...] = jnp.zeros(hist_scr.shape, F32)

    bufs = (h_scr, hg_scr, z_scr, xbc_scr, dt_scr)
    _in_proj(x_ref[0], ln1_ref, w_ref, bufs, tb)
    prm = (lbl_ref, hgn_ref, cw_ref, cb_ref, dtb_ref, alog_ref, dsk_ref, mn_ref)
    _recurrences(bufs, prm, mixh_ref, mixm_ref, (xs_scr, bc_scr, y_scr), (sth_scr, stm_scr, hist_scr), tb, ch, cs)

    @pl.when(c == last_c)
    def _():
        for hd in range(HG_HEADS):
            hgs_ref[0, hd] = sth_scr[hd].T
        for p in range(M_WIDTH // LANE):
            ssm_ref[0, p * LANE:(p + 1) * LANE, :] = stm_scr[:, p * LANE:(p + 1) * LANE].T
        conv_ref[0] = hist_scr[SUBLANE - (M_CONV - 1):SUBLANE, :]


def _const_spec(shape):
    nd = len(shape)
    return pl.BlockSpec(shape, lambda *_: (0,) * nd, pipeline_mode=pl.Buffered(1))


def _prompt_mixer(x, lbl, ln1, w_all, hgn, cw, cb, dtb, alog, dsk, mn, *, tb=256, ch=64, cs=128):
    bsz, seq, _ = x.shape
    nc = seq // tb
    kern = functools.partial(_prompt_mixer_kernel, tb=tb, ch=ch, cs=cs)
    params = (lbl, ln1, w_all, hgn, cw, cb, dtb, alog, dsk, mn)
    return pl.pallas_call(
        kern,
        grid=(bsz, nc),
        in_specs=[pl.BlockSpec((1, tb, D_MODEL), lambda b, c: (b, c, 0))] + [_const_spec(p.shape) for p in params],
        out_specs=[
            pl.BlockSpec((1, tb, HG_WIDTH), lambda b, c: (b, c, 0)),
            pl.BlockSpec((1, tb, M_WIDTH), lambda b, c: (b, c, 0)),
            pl.BlockSpec((1, HG_HEADS, HG_DK, HG_DV), lambda b, c: (b, 0, 0, 0)),
            pl.BlockSpec((1, M_WIDTH, M_DSTATE), lambda b, c: (b, 0, 0)),
            pl.BlockSpec((1, M_CONV - 1, M_CONV_DIM), lambda b, c: (b, 0, 0)),
        ],
        out_shape=[
            jax.ShapeDtypeStruct((bsz, seq, HG_WIDTH), BF16),
            jax.ShapeDtypeStruct((bsz, seq, M_WIDTH), BF16),
            jax.ShapeDtypeStruct((bsz, HG_HEADS, HG_DK, HG_DV), F32),
            jax.ShapeDtypeStruct((bsz, M_WIDTH, M_DSTATE), F32),
            jax.ShapeDtypeStruct((bsz, M_CONV - 1, M_CONV_DIM), F32),
        ],
        scratch_shapes=[
            pltpu.VMEM((tb, D_MODEL), BF16),
            pltpu.VMEM((tb, HG_PROJ_WIDTH), F32),
            pltpu.VMEM((tb, M_WIDTH), F32),
            pltpu.VMEM((tb + SUBLANE, M_CONV_DIM), F32),
            pltpu.VMEM((tb, LANE), F32),
            pltpu.VMEM((tb, M_WIDTH), F32),
            pltpu.VMEM((tb, 2 * M_GROUPS * M_DSTATE), BF16),
            pltpu.VMEM((tb, M_WIDTH), F32),
            pltpu.VMEM((HG_HEADS, HG_DV, HG_DK), F32),
            pltpu.VMEM((M_DSTATE, M_WIDTH), F32),
            pltpu.VMEM((SUBLANE, M_CONV_DIM), F32),
        ],
        compiler_params=pltpu.CompilerParams(
            dimension_semantics=("arbitrary", "arbitrary"), vmem_limit_bytes=VMEM_LIMIT),
        name="prompt_mixer",
    )(x, *params)


def _sample_in_proj_kernel(x_ref, ln1_ref, w_ref, o_ref, h_scr):
    @pl.when(pl.program_id(0) == 0)
    def _():
        h_scr[...] = _rms(x_ref[...], ln1_ref[...]).astype(BF16)

    o_ref[...] = _dot(h_scr[...], w_ref[...])


def _sample_in_proj(x, ln1, w_all, *, tn=1024):
    rows = x.shape[0]
    return pl.pallas_call(
        _sample_in_proj_kernel,
        grid=(W_IN_COLS // tn,),
        in_specs=[_const_spec((rows, D_MODEL)), _const_spec(ln1.shape),
                  pl.BlockSpec((D_MODEL, tn), lambda n: (0, n))],
        out_specs=pl.BlockSpec((rows, tn), lambda n: (0, n)),
        out_shape=jax.ShapeDtypeStruct((rows, W_IN_COLS), F32),
        scratch_shapes=[pltpu.VMEM((rows, D_MODEL), BF16)],
        compiler_params=pltpu.CompilerParams(
            dimension_semantics=("arbitrary",), vmem_limit_bytes=VMEM_LIMIT),
        name="sample_in_proj",
    )(x, ln1, w_all)


SEG = 8


def _block_causal_mask(rows):
    ri = lax.broadcasted_iota(jnp.int32, (rows, rows), 0)
    ci = lax.broadcasted_iota(jnp.int32, (rows, rows), 1)
    return ((ri & -SEG) == (ci & -SEG)) & (ci <= ri)


def _seg_last(x):
    rows, width = x.shape
    x3 = x.reshape(rows // SEG, SEG, width)
    return jnp.broadcast_to(x3[:, SEG - 1:SEG, :], x3.shape).reshape(rows, width)


def _decay_tail(dec8):
    h, m, l = _split3(dec8)
    t = lax.broadcasted_iota(jnp.int32, dec8.shape, 0)
    return jnp.where(t == 0, h.astype(F32), jnp.where(t == 1, m.astype(F32), jnp.where(t == 2, l.astype(F32), 0.0)))


def _ones_tail():
    t = lax.broadcasted_iota(jnp.int32, (2 * SEG, LANE), 0)
    return jnp.where((t >= SEG) & (t < SEG + 3), 1.0, 0.0).astype(BF16)


def _sample_hgrn_kernel(p_ref, s0_ref, lbl_ref, hgn_ref, mixh_ref, s8_ref,
                        qd_scr, ke_scr, dec_scr, o_scr, *, nb):
    rows = nb * SEG
    lb = _lower_bound(lbl_ref[...], 0)
    mask = _block_causal_mask(rows)
    for hd in range(HG_HEADS):
        cl = slice(hd * LANE, (hd + 1) * LANE)
        q = p_ref[:, OFF_Q + hd * LANE:OFF_Q + (hd + 1) * LANE]
        fr = p_ref[:, OFF_F + hd * LANE:OFF_F + (hd + 1) * LANE]
        v = p_ref[:, OFF_I + hd * LANE:OFF_I + (hd + 1) * LANE]
        lbh = lb[:, cl]
        f = _forget_gate(fr, lbh)
        a = jnp.log(f)
        k = 1.0 - f
        cum = _cumsum_rows(a, SEG)
        last = _seg_last(cum)
        qd = (q * jnp.exp(cum)).astype(BF16)
        kd = (k * jnp.exp(-cum)).astype(BF16)
        att = jnp.where(mask, _dot_nt(qd, kd), 0.0).astype(BF16)
        o_scr[:, cl] = _dot(att, v.astype(BF16))
        qd_scr[:, cl] = qd
        ke_scr[:, cl] = k * jnp.exp(last - cum)
        dec_scr[:, cl] = jnp.exp(last)

    ones_tail = _ones_tail()
    zeros8 = jnp.zeros((SEG, LANE), F32)

    def pair_body(m, carry):
        r16 = pl.ds(pl.multiple_of(m * 2 * SEG, 2 * SEG), 2 * SEG)
        for par in range(2):
            i = 2 * m + par
            r8 = pl.ds(pl.multiple_of(i * SEG, SEG), SEG)
            for hd in range(HG_HEADS):
                cl = slice(hd * LANE, (hd + 1) * LANE)
                s0 = s0_ref[i, hd]
                oi = _dot(qd_scr[r16, cl], s0.astype(BF16))
                o_scr[r8, cl] = o_scr[r8, cl] + oi[par * SEG:(par + 1) * SEG, :]
                aug = jnp.concatenate([ke_scr[r8, cl], _decay_tail(dec_scr[r8, cl])], axis=0).astype(BF16)
                v8 = p_ref[r8, OFF_I + hd * LANE:OFF_I + (hd + 1) * LANE]
                rhs = jnp.concatenate([jnp.concatenate([v8, zeros8], axis=0).astype(BF16), ones_tail], axis=1)
                ud = _dot_tn(aug, rhs)
                s8_ref[i, hd] = ud[:, LANE:] * s0 + ud[:, :LANE]
        return carry

    lax.fori_loop(0, nb // 2, pair_body, 0, unroll=True)

    for hd in range(HG_HEADS):
        cl = slice(hd * LANE, (hd + 1) * LANE)
        g = p_ref[:, OFF_G + hd * LANE:OFF_G + (hd + 1) * LANE]
        mixh_ref[:, cl] = (_rms(o_scr[:, cl], hgn_ref[:, cl]) * _silu(g)).astype(BF16)


def _sample_hgrn(proj, s0, lbl, hgn, *, nb=16):
    nseq = s0.shape[0]
    rows = nb * SEG
    kern = functools.partial(_sample_hgrn_kernel, nb=nb)
    return pl.pallas_call(
        kern,
        grid=(nseq // nb,),
        in_specs=[pl.BlockSpec((rows, OFF_XBC), lambda i: (i, 0)),
                  pl.BlockSpec((nb, HG_HEADS, HG_DK, HG_DV), lambda i: (i, 0, 0, 0)),
                  _const_spec(lbl.shape), _const_spec(hgn.shape)],
        out_specs=[pl.BlockSpec((rows, HG_WIDTH), lambda i: (i, 0)),
                   pl.BlockSpec((nb, HG_HEADS, HG_DK, HG_DV), lambda i: (i, 0, 0, 0))],
        out_shape=[jax.ShapeDtypeStruct((nseq * SEG, HG_WIDTH), BF16),
                   jax.ShapeDtypeStruct(s0.shape, F32)],
        scratch_shapes=[pltpu.VMEM((rows, HG_WIDTH), BF16),
                        pltpu.VMEM((rows, HG_WIDTH), F32),
                        pltpu.VMEM((rows, HG_WIDTH), F32),
                        pltpu.VMEM((rows, HG_WIDTH), F32)],
        compiler_params=pltpu.CompilerParams(
            dimension_semantics=("arbitrary",), vmem_limit_bytes=VMEM_LIMIT),
        name="sample_hgrn",
    )(proj, s0, lbl, hgn)


def _sample_ssd_kernel(xd_ref, z_ref, c0_ref, s0_ref, cw_ref, cb_ref, dtb_ref, alog_ref, dsk_ref, mn_ref,
                       mixm_ref, s8_ref, c8_ref,
                       hist_scr, raw_scr, xs_scr, bm_scr, cm_scr, u_scr, cd_scr, fs_scr, y_scr, *, nb):
    rows = nb * SEG
    t_in_seq = lax.broadcasted_iota(jnp.int32, (rows, LANE), 0) & (SEG - 1)

    @pl.when(pl.program_id(0) == 0)
    def _():
        hist_scr[...] = jnp.zeros(hist_scr.shape, F32)

    for cbk in range(M_CONV_DIM // LANE):
        cl = slice(cbk * LANE, (cbk + 1) * LANE)
        x = xd_ref[:, cl]
        raw_scr[cbk] = x
        for d in range(M_CONV - 1):
            seq_rows = pl.ds(SEG - (M_CONV - 1) + d, nb, stride=SEG)
            hist_scr[cbk, seq_rows, :] = c0_ref[d, :, cl]
            c8_ref[d, :, cl] = raw_scr[cbk, seq_rows, :]
        hist = hist_scr[cbk]
        acc = cb_ref[:, cl] + cw_ref[M_CONV - 1:M_CONV, cl] * x
        for d in range(1, M_CONV):
            shifted = jnp.where(t_in_seq >= d, pltpu.roll(x, d, 0), pltpu.roll(hist, rows - SEG + d, 0))
            acc = acc + cw_ref[M_CONV - 1 - d:M_CONV - d, cl] * shifted
        act = _silu(acc)
        if cbk < M_WIDTH // LANE:
            xs_scr[:, cl] = act
        elif cbk < (M_WIDTH + M_GROUPS * M_DSTATE) // LANE:
            bm_scr[:, cbk * LANE - M_WIDTH:(cbk + 1) * LANE - M_WIDTH] = act
        else:
            o0 = cbk * LANE - M_WIDTH - M_GROUPS * M_DSTATE
            cm_scr[:, o0:o0 + LANE] = act.astype(BF16)

    a_row = -jnp.exp(alog_ref[...])
    dtv = _softplus(xd_ref[:, M_CONV_DIM:M_CONV_DIM + LANE] + dtb_ref[...])
    cum = _cumsum_rows(dtv * a_row, SEG)
    cum_t = cum.T
    mask = _block_causal_mask(rows)
    lane_lo = lax.broadcasted_iota(jnp.int32, (rows, LANE), 1) < M_HEADDIM
    cbs = {}
    for p in range(M_HEADS // 2):
        grp = p // (M_HEADS // 2 // M_GROUPS)
        cl = slice(p * LANE, (p + 1) * LANE)
        if grp not in cbs:
            bm = bm_scr[:, grp * M_DSTATE:(grp + 1) * M_DSTATE].astype(BF16)
            cbs[grp] = _dot_nt(cm_scr[:, grp * M_DSTATE:(grp + 1) * M_DSTATE], bm)
        cb = cbs[grp]
        cum_h = [_lane_bcast(cum, hh) for hh in (2 * p, 2 * p + 1)]
        cum_x = jnp.where(lane_lo, cum_h[0], cum_h[1])
        dt_x = jnp.where(lane_lo, _lane_bcast(dtv, 2 * p), _lane_bcast(dtv, 2 * p + 1))
        last_x = _seg_last(cum_x)
        xs = xs_scr[:, cl]
        xdt = xs * dt_x
        ms = []
        for i, hh in enumerate((2 * p, 2 * p + 1)):
            seg = cum_h[i] - cum_t[hh:hh + 1, :]
            ms.append((cb * jnp.exp(jnp.where(mask, seg, NEG_BIG))).astype(BF16))
        x_lo = jnp.where(lane_lo, xdt, 0.0).astype(BF16)
        x_hi = jnp.where(lane_lo, 0.0, xdt).astype(BF16)
        y = _dot(jnp.concatenate(ms, axis=1), jnp.concatenate([x_lo, x_hi], axis=0))
        y_scr[:, cl] = y + dsk_ref[:, cl] * xs
        u_scr[:, cl] = xdt * jnp.exp(last_x - cum_x)
        cd_scr[:, cl] = jnp.exp(last_x)
        fs_scr[:, cl] = jnp.exp(cum_x)

    ones_tail = _ones_tail()
    zeros8 = jnp.zeros((SEG, M_DSTATE), F32)

    def pair_body(m, carry):
        r16 = pl.ds(pl.multiple_of(m * 2 * SEG, 2 * SEG), 2 * SEG)
        for par in range(2):
            i = 2 * m + par
            r8 = pl.ds(pl.multiple_of(i * SEG, SEG), SEG)
            for grp in range(M_GROUPS):
                gl = slice(grp * M_GROUP_WIDTH, (grp + 1) * M_GROUP_WIDTH)
                nl = slice(grp * M_DSTATE, (grp + 1) * M_DSTATE)
                s0 = s0_ref[i, gl, :]
                yi = _dot_nt(cm_scr[r16, nl], s0.astype(BF16))
                y_scr[r8, gl] = y_scr[r8, gl] + yi[par * SEG:(par + 1) * SEG, :] * fs_scr[r8, gl]
                aug = jnp.concatenate([u_scr[r8, gl], _decay_tail(cd_scr[r8, gl])], axis=0).astype(BF16)
                rhs = jnp.concatenate(
                    [jnp.concatenate([bm_scr[r8, nl], zeros8], axis=0).astype(BF16), ones_tail], axis=1)
                ud = _dot_tn(aug, rhs)
                s8_ref[i, gl, :] = ud[:, M_DSTATE:] * s0 + ud[:, :M_DSTATE]
        return carry

    lax.fori_loop(0, nb // 2, pair_body, 0, unroll=True)

    for grp in range(M_GROUPS):
        gl = slice(grp * M_GROUP_WIDTH, (grp + 1) * M_GROUP_WIDTH)
        y = y_scr[:, gl] * _silu(z_ref[:, gl])
        mixm_ref[:, gl] = _rms(y, mn_ref[:, gl]).astype(BF16)


def _sample_ssd(proj, c0, s0, cw, cb, dtb, alog, dsk, mn, *, nb=16):
    nseq = s0.shape[0]
    rows = nb * SEG
    kern = functools.partial(_sample_ssd_kernel, nb=nb)
    params = (cw, cb, dtb, alog, dsk, mn)
    conv_spec = pl.BlockSpec((M_CONV - 1, nb, M_CONV_DIM), lambda i: (0, i, 0))
    return pl.pallas_call(
        kern,
        grid=(nseq // nb,),
        in_specs=[pl.BlockSpec((rows, XD_WIDTH), lambda i: (i, OFF_XBC // XD_WIDTH)),
                  pl.BlockSpec((rows, M_WIDTH), lambda i: (i, OFF_Z // M_WIDTH)),
                  conv_spec,
                  pl.BlockSpec((nb, M_WIDTH, M_DSTATE), lambda i: (i, 0, 0))]
                 + [_const_spec(p.shape) for p in params],
        out_specs=[pl.BlockSpec((rows, M_WIDTH), lambda i: (i, 0)),
                   pl.BlockSpec((nb, M_WIDTH, M_DSTATE), lambda i: (i, 0, 0)),
                   conv_spec],
        out_shape=[jax.ShapeDtypeStruct((nseq * SEG, M_WIDTH), BF16),
                   jax.ShapeDtypeStruct(s0.shape, F32),
                   jax.ShapeDtypeStruct(c0.shape, F32)],
        scratch_shapes=[pltpu.VMEM((M_CONV_DIM // LANE, rows, LANE), F32),
                        pltpu.VMEM((M_CONV_DIM // LANE, rows, LANE), F32),
                        pltpu.VMEM((rows, M_WIDTH), F32),
                        pltpu.VMEM((rows, M_GROUPS * M_DSTATE), F32),
                        pltpu.VMEM((rows, M_GROUPS * M_DSTATE), BF16),
                        pltpu.VMEM((rows, M_WIDTH), F32),
                        pltpu.VMEM((rows, M_WIDTH), F32),
                        pltpu.VMEM((rows, M_WIDTH), F32),
                        pltpu.VMEM((rows, M_WIDTH), F32)],
        compiler_params=pltpu.CompilerParams(
            dimension_semantics=("arbitrary",), vmem_limit_bytes=VMEM_LIMIT),
        name="sample_ssd",
    )(proj, proj, c0, s0, *params)


def _out_mlp_kernel(x_ref, mh_ref, mm_ref, wo_ref, ln2_ref, wu_ref, wd_ref, lnf_ref, o_ref, *, ff_tile):
    x1 = x_ref[...] + _dot(mh_ref[...], wo_ref[0:HG_WIDTH, :]) + _dot(mm_ref[...], wo_ref[HG_WIDTH:, :])
    hn = _rms(x1, ln2_ref[...]).astype(BF16)
    mlp = None
    for j in range(D_FF // ff_tile):
        u = jnp.maximum(_dot(hn, wu_ref[:, j * ff_tile:(j + 1) * ff_tile]), 0.0)
        d = _dot((u * u).astype(BF16), wd_ref[j * ff_tile:(j + 1) * ff_tile, :])
        mlp = d if mlp is None else mlp + d
    o_ref[...] = _rms(x1 + mlp, lnf_ref[...])


def _out_mlp(x, mix_h, mix_m, w_out, ln2, w_up, w_down, ln_f, *, tm=512, ff_tile=1024):
    rows = x.shape[0]
    kern = functools.partial(_out_mlp_kernel, ff_tile=ff_tile)
    row_spec = lambda w: pl.BlockSpec((tm, w), lambda i: (i, 0))
    return pl.pallas_call(
        kern,
        grid=(rows // tm,),
        in_specs=[row_spec(D_MODEL), row_spec(HG_WIDTH), row_spec(M_WIDTH),
                  _const_spec(w_out.shape), _const_spec(ln2.shape), _const_spec(w_up.shape),
                  _const_spec(w_down.shape), _const_spec(ln_f.shape)],
        out_specs=row_spec(D_MODEL),
        out_shape=jax.ShapeDtypeStruct((rows, D_MODEL), F32),
        compiler_params=pltpu.CompilerParams(
            dimension_semantics=("arbitrary",), vmem_limit_bytes=VMEM_LIMIT),
        name="out_mlp",
    )(x, mix_h, mix_m, w_out, ln2, w_up, w_down, ln_f)


def _pad_lanes(v):
    return jnp.pad(v, ((0, 0), (0, LANE - v.shape[1])))


def kernel(x_prompt, x_sample, state_hgrn, state_ssm, state_conv, hg_lb_logits, ln1, w_in, hg_norm, conv_w,
           conv_b, dt_bias, a_log, d_skip, m_norm, w_out, ln2, w_up, w_down, ln_f):
    l = 0
    bp, seq, _ = x_prompt.shape
    bs, dseq, _ = x_sample.shape
    assert DEPTH == 1 and dseq == SEG

    w_all = _pack_in_proj(jnp.transpose(w_in[l]))
    lbl = hg_lb_logits.astype(F32)
    ln1_r = ln1[l][None, :]
    hgn_r = hg_norm[l].reshape(1, HG_WIDTH)
    cw = conv_w[l]
    cb_r = conv_b[l][None, :]
    dtb_r = _pad_lanes(dt_bias[l][None, :])
    alog_r = _pad_lanes(a_log[l][None, :])
    dsk_r = jnp.repeat(d_skip[l], M_HEADDIM)[None, :]
    mn_r = m_norm[l][None, :]
    mlp_w = (w_out[l].astype(BF16), ln2[l][None, :], w_up[l].astype(BF16), w_down[l].astype(BF16), ln_f[None, :])

    mixh_p, mixm_p, hgs_p, ssm_p, conv_p = _prompt_mixer(
        x_prompt, lbl, ln1_r, w_all, hgn_r, cw, cb_r, dtb_r, alog_r, dsk_r, mn_r)
    y_p = _out_mlp(x_prompt.reshape(bp * seq, D_MODEL), mixh_p.reshape(bp * seq, HG_WIDTH),
                   mixm_p.reshape(bp * seq, M_WIDTH), *mlp_w)

    xs2 = x_sample.reshape(bs * SEG, D_MODEL)
    proj_s = _sample_in_proj(xs2, ln1_r, w_all)
    mixh_s, hgs_s = _sample_hgrn(proj_s, state_hgrn[l], lbl, hgn_r)
    conv0 = jnp.transpose(state_conv[l], (1, 0, 2))
    mixm_s, ssm_s, conv8 = _sample_ssd(proj_s, conv0, state_ssm[l].reshape(bs, M_WIDTH, M_DSTATE),
                                       cw, cb_r, dtb_r, alog_r, dsk_r, mn_r)
    y_s = _out_mlp(xs2, mixh_s, mixm_s, *mlp_w)
    conv_s = jnp.transpose(conv8, (1, 0, 2))

    return (y_p.reshape(bp, seq, D_MODEL), y_s.reshape(bs, SEG, D_MODEL),
            hgs_p[None], hgs_s[None],
            ssm_p.reshape(1, bp, M_HEADS, M_HEADDIM, M_DSTATE), ssm_s.reshape(1, bs, M_HEADS, M_HEADDIM, M_DSTATE),
            conv_p[None], conv_s[None])
```

```python
import functools

import jax
import jax.numpy as jnp
from jax import lax
from jax.experimental import pallas as pl
from jax.experimental.pallas import tpu as pltpu

F32 = jnp.float32
BF16 = jnp.bfloat16

D_MODEL = 1024
DEPTH = 1
HG_HEADS = 8
HG_DK = 128
HG_DV = 128
HG_WIDTH = HG_HEADS * HG_DV
M_WIDTH = 1024
M_HEADDIM = 64
M_HEADS = M_WIDTH // M_HEADDIM
M_DSTATE = 128
M_GROUPS = 2
M_GROUP_WIDTH = M_WIDTH // M_GROUPS
M_CONV = 4
M_CONV_DIM = M_WIDTH + 2 * M_GROUPS * M_DSTATE
D_FF = 4 * D_MODEL
NORM_EPS = 1e-5

LANE = 128
SUBLANE = 8

OFF_Q, OFF_F, OFF_I, OFF_G = 0, 1024, 2048, 3072
HG_PROJ_WIDTH = 4096
OFF_XBC = HG_PROJ_WIDTH
OFF_DT = OFF_XBC + M_CONV_DIM
XD_WIDTH = 2048
OFF_Z = OFF_XBC + XD_WIDTH
W_IN_COLS = OFF_Z + M_WIDTH
PACK_TILE = 1024

VMEM_LIMIT = 56 * 1024 * 1024

NEG_BIG = -1e30


def _dot(a, b):
    return jnp.dot(a, b, preferred_element_type=F32)


def _dot_nt(a, b):
    return lax.dot_general(a, b, (((1,), (1,)), ((), ())), preferred_element_type=F32)


def _dot_tn(a, b):
    return lax.dot_general(a, b, (((0,), (0,)), ((), ())), preferred_element_type=F32)


def _silu(x):
    hx = 0.5 * x
    return hx + hx * jnp.tanh(hx)


def _forget_gate(fr, lb):
    half = 0.5 * (1.0 - lb)
    return (lb + half) + half * jnp.tanh(0.5 * fr)


def _softplus(x):
    return jnp.maximum(x, 0.0) + jnp.log(1.0 + jnp.exp(-jnp.abs(x)))


def _rms(x, gain):
    ms = jnp.mean(x * x, axis=-1, keepdims=True)
    return x * lax.rsqrt(ms + NORM_EPS) * gain


def _lower_bound(lbl, layer):
    rows = [lbl[i:i + 1, :] for i in range(DEPTH + 1)]
    m = functools.reduce(jnp.maximum, rows)
    es = [jnp.exp(r - m) for r in rows]
    return sum(es[:layer + 1]) / sum(es)


def _cumsum_rows(x, seg):
    t = lax.broadcasted_iota(jnp.int32, x.shape, 0) & (seg - 1)
    s = 1
    while s < seg:
        x = x + jnp.where(t >= s, pltpu.roll(x, s, 0), 0.0)
        s *= 2
    return x


def _split3(x):
    h = x.astype(BF16)
    r = x - h.astype(F32)
    m = r.astype(BF16)
    l = (r - m.astype(F32)).astype(BF16)
    return h, m, l


def _lane_bcast(x, lane):
    return jnp.broadcast_to(x[:, lane:lane + 1], x.shape)


def _pack_in_proj_kernel(wt_ref, o_ref, *, tn, n_hg, n_xd, src_cols):
    j = pl.program_id(0)
    col0 = _pack_src_tile(j, n_hg, n_xd) * tn
    col = col0 + lax.broadcasted_iota(jnp.int32, wt_ref.shape, 0)
    o_ref[...] = jnp.where(col < src_cols, wt_ref[...], 0.0).T.astype(BF16)


def _pack_src_tile(j, n_hg, n_xd):
    n_z = M_WIDTH // PACK_TILE
    return jnp.where(j < n_hg, j, jnp.where(j < n_hg + n_xd, j + n_z, j - n_xd))


def _pack_in_proj(wt):
    tn = PACK_TILE
    n_hg, n_xd = HG_PROJ_WIDTH // tn, XD_WIDTH // tn
    kern = functools.partial(_pack_in_proj_kernel, tn=tn, n_hg=n_hg, n_xd=n_xd, src_cols=wt.shape[0])
    return pl.pallas_call(
        kern,
        grid=(W_IN_COLS // tn,),
        in_specs=[pl.BlockSpec((tn, D_MODEL), lambda j: (_pack_src_tile(j, n_hg, n_xd), 0))],
        out_specs=pl.BlockSpec((D_MODEL, tn), lambda j: (0, j)),
        out_shape=jax.ShapeDtypeStruct((D_MODEL, W_IN_COLS), BF16),
        compiler_params=pltpu.CompilerParams(
            dimension_semantics=("arbitrary",), vmem_limit_bytes=VMEM_LIMIT),
        name="pack_in_proj",
    )(wt)


def _in_proj(x, ln1_ref, w_ref, bufs, tb):
    h_scr, hg_scr, z_scr, xbc_scr, dt_scr = bufs
    h_scr[...] = _rms(x, ln1_ref[...]).astype(BF16)
    for n0 in range(0, HG_PROJ_WIDTH, 1024):
        hg_scr[:, n0:n0 + 1024] = _dot(h_scr[...], w_ref[:, n0:n0 + 1024])
    xbc_scr[SUBLANE:SUBLANE + tb, :] = _dot(h_scr[...], w_ref[:, OFF_XBC:OFF_DT])
    dt_scr[...] = _dot(h_scr[...], w_ref[:, OFF_DT:OFF_DT + LANE])
    z_scr[...] = _dot(h_scr[...], w_ref[:, OFF_Z:OFF_Z + M_WIDTH])


def _recurrences(bufs, prm, mixh_ref, mixm_ref, work, state, tb, ch, cs):
    _, hg_scr, z_scr, xbc_scr, dt_scr = bufs
    lbl_ref, hgn_ref, cw_ref, cb_ref, dtb_ref, alog_ref, dsk_ref, mn_ref = prm
    xs_scr, bc_scr, y_scr = work
    sth_scr, stm_scr, hist_scr = state

    xbc_scr[0:SUBLANE, :] = hist_scr[...]

    lb = _lower_bound(lbl_ref[...], 0)
    tril_h = (lax.broadcasted_iota(jnp.int32, (ch, ch), 0) >= lax.broadcasted_iota(jnp.int32, (ch, ch), 1))
    for j in range(tb // ch):
        r = slice(j * ch, (j + 1) * ch)
        for hd in range(HG_HEADS):
            cl = slice(hd * LANE, (hd + 1) * LANE)
            q = hg_scr[r, OFF_Q + hd * LANE:OFF_Q + (hd + 1) * LANE]
            fr = hg_scr[r, OFF_F + hd * LANE:OFF_F + (hd + 1) * LANE]
            v = hg_scr[r, OFF_I + hd * LANE:OFF_I + (hd + 1) * LANE]
            g = hg_scr[r, OFF_G + hd * LANE:OFF_G + (hd + 1) * LANE]
            lbh = lb[:, cl]
            f = _forget_gate(fr, lbh)
            a = jnp.log(f)
            k = 1.0 - f
            cum = _cumsum_rows(a, ch)
            mid = cum[ch // 2 - 1:ch // 2, :]
            last = cum[ch - 1:ch, :]
            qd_mid = q * jnp.exp(cum - mid)
            kd_mid = k * jnp.exp(mid - cum)
            qd = (qd_mid * jnp.exp(mid)).astype(BF16)
            k_end = (kd_mid * jnp.exp(last - mid)).astype(BF16)
            dec = jnp.exp(last)
            vb = v.astype(BF16)
            att = _dot_nt(qd_mid.astype(BF16), kd_mid.astype(BF16))
            att = jnp.where(tril_h, att, 0.0).astype(BF16)
            st = sth_scr[hd]
            o = _dot(att, vb) + _dot_nt(qd, st.astype(BF16))
            sth_scr[hd] = dec * st + _dot_tn(vb, k_end)
            on = _rms(o, hgn_ref[:, cl]) * _silu(g)
            mixh_ref[0, r, cl] = on.astype(BF16)

    for cbk in range(M_CONV_DIM // LANE):
        cl = slice(cbk * LANE, (cbk + 1) * LANE)
        acc = cb_ref[:, cl] + cw_ref[M_CONV - 1:M_CONV, cl] * xbc_scr[SUBLANE:SUBLANE + tb, cl]
        for d in range(1, M_CONV):
            acc = acc + cw_ref[M_CONV - 1 - d:M_CONV - d, cl] * xbc_scr[SUBLANE - d:SUBLANE - d + tb, cl]
        act = _silu(acc)
        if cbk < M_WIDTH // LANE:
            xs_scr[:, cl] = act
        else:
            bc_scr[:, (cbk * LANE - M_WIDTH):(cbk * LANE - M_WIDTH) + LANE] = act.astype(BF16)
    hist_scr[...] = xbc_scr[tb:tb + SUBLANE, :]

    a_row = -jnp.exp(alog_ref[...])
    tril_m = (lax.broadcasted_iota(jnp.int32, (cs, cs), 0) >= lax.broadcasted_iota(jnp.int32, (cs, cs), 1))
    lane_lo = lax.broadcasted_iota(jnp.int32, (cs, LANE), 1) < M_HEADDIM
    for s in range(tb // cs):
        r = slice(s * cs, (s + 1) * cs)
        dtv = _softplus(dt_scr[r, :] + dtb_ref[...])
        cum = _cumsum_rows(dtv * a_row, cs)
        cum_t = cum.T
        cbs = {}
        for p in range(M_HEADS // 2):
            grp = p // (M_HEADS // 2 // M_GROUPS)
            cl = slice(p * LANE, (p + 1) * LANE)
            bm = bc_scr[r, grp * M_DSTATE:(grp + 1) * M_DSTATE]
            cm = bc_scr[r, (M_GROUPS + grp) * M_DSTATE:(M_GROUPS + grp + 1) * M_DSTATE]
            if grp not in cbs:
                cbs[grp] = _dot_nt(cm, bm)
            cb = cbs[grp]
            cum_h = [_lane_bcast(cum, hh) for hh in (2 * p, 2 * p + 1)]
            cum_x = jnp.where(lane_lo, cum_h[0], cum_h[1])
            dt_x = jnp.where(lane_lo, _lane_bcast(dtv, 2 * p), _lane_bcast(dtv, 2 * p + 1))
            last_x = cum_x[cs - 1:cs, :]
            xs = xs_scr[r, cl]
            xdt = xs * dt_x
            ms = []
            for i, hh in enumerate((2 * p, 2 * p + 1)):
                seg = cum_h[i] - cum_t[hh:hh + 1, :]
                ms.append((cb * jnp.exp(jnp.where(tril_m, seg, NEG_BIG))).astype(BF16))
            x_lo = jnp.where(lane_lo, xdt, 0.0).astype(BF16)
            x_hi = jnp.where(lane_lo, 0.0, xdt).astype(BF16)
            y = _dot(jnp.concatenate(ms, axis=1), jnp.concatenate([x_lo, x_hi], axis=0))
            st = stm_scr[:, cl]
            y = y + _dot(cm, st.astype(BF16)) * jnp.exp(cum_x)
            y = y + dsk_ref[:, cl] * xs
            to_end = (xdt * jnp.exp(last_x - cum_x)).astype(BF16)
            stm_scr[:, cl] = jnp.exp(last_x) * st + _dot_tn(bm, to_end)
            y_scr[r, cl] = y * _silu(z_scr[r, cl])
        for grp in range(M_GROUPS):
            gl = slice(grp * M_GROUP_WIDTH, (grp + 1) * M_GROUP_WIDTH)
            mixm_ref[0, r, gl] = _rms(y_scr[r, gl], mn_ref[:, gl]).astype(BF16)


def _prompt_mixer_kernel(x_ref, lbl_ref, ln1_ref, w_ref, hgn_ref, cw_ref, cb_ref, dtb_ref, alog_ref, dsk_ref, mn_ref,
                         mixh_ref, mixm_ref, hgs_ref, ssm_ref, conv_ref,
                         h_scr, hg_scr, z_scr, xbc_scr, dt_scr, xs_scr, bc_scr, y_scr, sth_scr, stm_scr, hist_scr,
                         *, tb, ch, cs):
    c = pl.program_id(1)
    last_c = pl.num_programs(1) - 1

    @pl.when(c == 0)
    def _():
        sth_scr[...] = jnp.zeros(sth_scr.shape, F32)
        stm_scr[...] = jnp.zeros(stm_scr.shape, F32)
        hist_scr[...] = jnp.zeros(hist_scr.shape, F32)

    bufs = (h_scr, hg_scr, z_scr, xbc_scr, dt_scr)
    _in_proj(x_ref[0], ln1_ref, w_ref, bufs, tb)
    prm = (lbl_ref, hgn_ref, cw_ref, cb_ref, dtb_ref, alog_ref, dsk_ref, mn_ref)
    _recurrences(bufs, prm, mixh_ref, mixm_ref, (xs_scr, bc_scr, y_scr), (sth_scr, stm_scr, hist_scr), tb, ch, cs)

    @pl.when(c == last_c)
    def _():
        for hd in range(HG_HEADS):
            hgs_ref[0, hd] = sth_scr[hd].T
        for p in range(M_WIDTH // LANE):
            ssm_ref[0, p * LANE:(p + 1) * LANE, :] = stm_scr[:, p * LANE:(p + 1) * LANE].T
        conv_ref[0] = hist_scr[SUBLANE - (M_CONV - 1):SUBLANE, :]


def _const_spec(shape):
    nd = len(shape)
    return pl.BlockSpec(shape, lambda *_: (0,) * nd, pipeline_mode=pl.Buffered(1))


def _prompt_mixer(x, lbl, ln1, w_all, hgn, cw, cb, dtb, alog, dsk, mn, *, tb=256, ch=32, cs=128):
    bsz, seq, _ = x.shape
    nc = seq // tb
    kern = functools.partial(_prompt_mixer_kernel, tb=tb, ch=ch, cs=cs)
    params = (lbl, ln1, w_all, hgn, cw, cb, dtb, alog, dsk, mn)
    return pl.pallas_call(
        kern,
        grid=(bsz, nc),
        in_specs=[pl.BlockSpec((1, tb, D_MODEL), lambda b, c: (b, c, 0))] + [_const_spec(p.shape) for p in params],
        out_specs=[
            pl.BlockSpec((1, tb, HG_WIDTH), lambda b, c: (b, c, 0)),
            pl.BlockSpec((1, tb, M_WIDTH), lambda b, c: (b, c, 0)),
            pl.BlockSpec((1, HG_HEADS, HG_DK, HG_DV), lambda b, c: (b, 0, 0, 0)),
            pl.BlockSpec((1, M_WIDTH, M_DSTATE), lambda b, c: (b, 0, 0)),
            pl.BlockSpec((1, M_CONV - 1, M_CONV_DIM), lambda b, c: (b, 0, 0)),
        ],
        out_shape=[
            jax.ShapeDtypeStruct((bsz, seq, HG_WIDTH), BF16),
            jax.ShapeDtypeStruct((bsz, seq, M_WIDTH), BF16),
            jax.ShapeDtypeStruct((bsz, HG_HEADS, HG_DK, HG_DV), F32),
            jax.ShapeDtypeStruct((bsz, M_WIDTH, M_DSTATE), F32),
            jax.ShapeDtypeStruct((bsz, M_CONV - 1, M_CONV_DIM), F32),
        ],
        scratch_shapes=[
            pltpu.VMEM((tb, D_MODEL), BF16),
            pltpu.VMEM((tb, HG_PROJ_WIDTH), F32),
            pltpu.VMEM((tb, M_WIDTH), F32),
            pltpu.VMEM((tb + SUBLANE, M_CONV_DIM), F32),
            pltpu.VMEM((tb, LANE), F32),
            pltpu.VMEM((tb, M_WIDTH), F32),
            pltpu.VMEM((tb, 2 * M_GROUPS * M_DSTATE), BF16),
            pltpu.VMEM((tb, M_WIDTH), F32),
            pltpu.VMEM((HG_HEADS, HG_DV, HG_DK), F32),
            pltpu.VMEM((M_DSTATE, M_WIDTH), F32),
            pltpu.VMEM((SUBLANE, M_CONV_DIM), F32),
        ],
        compiler_params=pltpu.CompilerParams(
            dimension_semantics=("arbitrary", "arbitrary"), vmem_limit_bytes=VMEM_LIMIT),
        name="prompt_mixer",
    )(x, *params)


def _sample_in_proj_kernel(x_ref, ln1_ref, w_ref, o_ref, h_scr):
    @pl.when(pl.program_id(0) == 0)
    def _():
        h_scr[...] = _rms(x_ref[...], ln1_ref[...]).astype(BF16)

    o_ref[...] = _dot(h_scr[...], w_ref[...])


def _sample_in_proj(x, ln1, w_all, *, tn=1024):
    rows = x.shape[0]
    return pl.pallas_call(
        _sample_in_proj_kernel,
        grid=(W_IN_COLS // tn,),
        in_specs=[_const_spec((rows, D_MODEL)), _const_spec(ln1.shape),
                  pl.BlockSpec((D_MODEL, tn), lambda n: (0, n))],
        out_specs=pl.BlockSpec((rows, tn), lambda n: (0, n)),
        out_shape=jax.ShapeDtypeStruct((rows, W_IN_COLS), F32),
        scratch_shapes=[pltpu.VMEM((rows, D_MODEL), BF16)],
        compiler_params=pltpu.CompilerParams(
            dimension_semantics=("arbitrary",), vmem_limit_bytes=VMEM_LIMIT),
        name="sample_in_proj",
    )(x, ln1, w_all)


SEG = 8


def _block_causal_mask(rows):
    ri = lax.broadcasted_iota(jnp.int32, (rows, rows), 0)
    ci = lax.broadcasted_iota(jnp.int32, (rows, rows), 1)
    return ((ri & -SEG) == (ci & -SEG)) & (ci <= ri)


def _seg_last(x):
    rows, width = x.shape
    x3 = x.reshape(rows // SEG, SEG, width)
    return jnp.broadcast_to(x3[:, SEG - 1:SEG, :], x3.shape).reshape(rows, width)


def _decay_tail(dec8):
    h, m, l = _split3(dec8)
    t = lax.broadcasted_iota(jnp.int32, dec8.shape, 0)
    return jnp.where(t == 0, h.astype(F32), jnp.where(t == 1, m.astype(F32), jnp.where(t == 2, l.astype(F32), 0.0)))


def _ones_tail():
    t = lax.broadcasted_iota(jnp.int32, (2 * SEG, LANE), 0)
    return jnp.where((t >= SEG) & (t < SEG + 3), 1.0, 0.0).astype(BF16)


def _sample_hgrn_kernel(p_ref, s0_ref, lbl_ref, hgn_ref, mixh_ref, s8_ref,
                        qd_scr, ke_scr, dec_scr, o_scr, *, nb):
    rows = nb * SEG
    lb = _lower_bound(lbl_ref[...], 0)
    mask = _block_causal_mask(rows)
    for hd in range(HG_HEADS):
        cl = slice(hd * LANE, (hd + 1) * LANE)
        q = p_ref[:, OFF_Q + hd * LANE:OFF_Q + (hd + 1) * LANE]
        fr = p_ref[:, OFF_F + hd * LANE:OFF_F + (hd + 1) * LANE]
        v = p_ref[:, OFF_I + hd * LANE:OFF_I + (hd + 1) * LANE]
        lbh = lb[:, cl]
        f = _forget_gate(fr, lbh)
        a = jnp.log(f)
        k = 1.0 - f
        cum = _cumsum_rows(a, SEG)
        last = _seg_last(cum)
        qd = (q * jnp.exp(cum)).astype(BF16)
        kd = (k * jnp.exp(-cum)).astype(BF16)
        att = jnp.where(mask, _dot_nt(qd, kd), 0.0).astype(BF16)
        o_scr[:, cl] = _dot(att, v.astype(BF16))
        qd_scr[:, cl] = qd
        ke_scr[:, cl] = k * jnp.exp(last - cum)
        dec_scr[:, cl] = jnp.exp(last)

    ones_tail = _ones_tail()
    zeros8 = jnp.zeros((SEG, LANE), F32)

    def pair_body(m, carry):
        r16 = pl.ds(pl.multiple_of(m * 2 * SEG, 2 * SEG), 2 * SEG)
        for par in range(2):
            i = 2 * m + par
            r8 = pl.ds(pl.multiple_of(i * SEG, SEG), SEG)
            for hd in range(HG_HEADS):
                cl = slice(hd * LANE, (hd + 1) * LANE)
                s0 = s0_ref[i, hd]
                oi = _dot(qd_scr[r16, cl], s0.astype(BF16))
                o_scr[r8, cl] = o_scr[r8, cl] + oi[par * SEG:(par + 1) * SEG, :]
                aug = jnp.concatenate([ke_scr[r8, cl], _decay_tail(dec_scr[r8, cl])], axis=0).astype(BF16)
                v8 = p_ref[r8, OFF_I + hd * LANE:OFF_I + (hd + 1) * LANE]
                rhs = jnp.concatenate([jnp.concatenate([v8, zeros8], axis=0).astype(BF16), ones_tail], axis=1)
                ud = _dot_tn(aug, rhs)
                s8_ref[i, hd] = ud[:, LANE:] * s0 + ud[:, :LANE]
        return carry

    lax.fori_loop(0, nb // 2, pair_body, 0, unroll=True)

    for hd in range(HG_HEADS):
        cl = slice(hd * LANE, (hd + 1) * LANE)
        g = p_ref[:, OFF_G + hd * LANE:OFF_G + (hd + 1) * LANE]
        mixh_ref[:, cl] = (_rms(o_scr[:, cl], hgn_ref[:, cl]) * _silu(g)).astype(BF16)


def _sample_hgrn(proj, s0, lbl, hgn, *, nb=16):
    nseq = s0.shape[0]
    rows = nb * SEG
    kern = functools.partial(_sample_hgrn_kernel, nb=nb)
    return pl.pallas_call(
        kern,
        grid=(nseq // nb,),
        in_specs=[pl.BlockSpec((rows, OFF_XBC), lambda i: (i, 0)),
                  pl.BlockSpec((nb, HG_HEADS, HG_DK, HG_DV), lambda i: (i, 0, 0, 0)),
                  _const_spec(lbl.shape), _const_spec(hgn.shape)],
        out_specs=[pl.BlockSpec((rows, HG_WIDTH), lambda i: (i, 0)),
                   pl.BlockSpec((nb, HG_HEADS, HG_DK, HG_DV), lambda i: (i, 0, 0, 0))],
        out_shape=[jax.ShapeDtypeStruct((nseq * SEG, HG_WIDTH), BF16),
                   jax.ShapeDtypeStruct(s0.shape, F32)],
        scratch_shapes=[pltpu.VMEM((rows, HG_WIDTH), BF16),
                        pltpu.VMEM((rows, HG_WIDTH), F32),
                        pltpu.VMEM((rows, HG_WIDTH), F32),
                        pltpu.VMEM((rows, HG_WIDTH), F32)],
        compiler_params=pltpu.CompilerParams(
            dimension_semantics=("arbitrary",), vmem_limit_bytes=VMEM_LIMIT),
        name="sample_hgrn",
    )(proj, s0, lbl, hgn)


def _sample_ssd_kernel(xd_ref, z_ref, c0_ref, s0_ref, cw_ref, cb_ref, dtb_ref, alog_ref, dsk_ref, mn_ref,
                       mixm_ref, s8_ref, c8_ref,
                       hist_scr, raw_scr, xs_scr, bm_scr, cm_scr, u_scr, cd_scr, fs_scr, y_scr, *, nb):
    rows = nb * SEG
    t_in_seq = lax.broadcasted_iota(jnp.int32, (rows, LANE), 0) & (SEG - 1)

    @pl.when(pl.program_id(0) == 0)
    def _():
        hist_scr[...] = jnp.zeros(hist_scr.shape, F32)

    for cbk in range(M_CONV_DIM // LANE):
        cl = slice(cbk * LANE, (cbk + 1) * LANE)
        x = xd_ref[:, cl]
        raw_scr[cbk] = x
        for d in range(M_CONV - 1):
            seq_rows = pl.ds(SEG - (M_CONV - 1) + d, nb, stride=SEG)
            hist_scr[cbk, seq_rows, :] = c0_ref[d, :, cl]
            c8_ref[d, :, cl] = raw_scr[cbk, seq_rows, :]
        hist = hist_scr[cbk]
        acc = cb_ref[:, cl] + cw_ref[M_CONV - 1:M_CONV, cl] * x
        for d in range(1, M_CONV):
            shifted = jnp.where(t_in_seq >= d, pltpu.roll(x, d, 0), pltpu.roll(hist, rows - SEG + d, 0))
            acc = acc + cw_ref[M_CONV - 1 - d:M_CONV - d, cl] * shifted
        act = _silu(acc)
        if cbk < M_WIDTH // LANE:
            xs_scr[:, cl] = act
        elif cbk < (M_WIDTH + M_GROUPS * M_DSTATE) // LANE:
            bm_scr[:, cbk * LANE - M_WIDTH:(cbk + 1) * LANE - M_WIDTH] = act
        else:
            o0 = cbk * LANE - M_WIDTH - M_GROUPS * M_DSTATE
            cm_scr[:, o0:o0 + LANE] = act.astype(BF16)

    a_row = -jnp.exp(alog_ref[...])
    dtv = _softplus(xd_ref[:, M_CONV_DIM:M_CONV_DIM + LANE] + dtb_ref[...])
    cum = _cumsum_rows(dtv * a_row, SEG)
    cum_t = cum.T
    mask = _block_causal_mask(rows)
    lane_lo = lax.broadcasted_iota(jnp.int32, (rows, LANE), 1) < M_HEADDIM
    cbs = {}
    for p in range(M_HEADS // 2):
        grp = p // (M_HEADS // 2 // M_GROUPS)
        cl = slice(p * LANE, (p + 1) * LANE)
        if grp not in cbs:
            bm = bm_scr[:, grp * M_DSTATE:(grp + 1) * M_DSTATE].astype(BF16)
            cbs[grp] = _dot_nt(cm_scr[:, grp * M_DSTATE:(grp + 1) * M_DSTATE], bm)
        cb = cbs[grp]
        cum_h = [_lane_bcast(cum, hh) for hh in (2 * p, 2 * p + 1)]
        cum_x = jnp.where(lane_lo, cum_h[0], cum_h[1])
        dt_x = jnp.where(lane_lo, _lane_bcast(dtv, 2 * p), _lane_bcast(dtv, 2 * p + 1))
        last_x = _seg_last(cum_x)
        xs = xs_scr[:, cl]
        xdt = xs * dt_x
        ms = []
        for i, hh in enumerate((2 * p, 2 * p + 1)):
            seg = cum_h[i] - cum_t[hh:hh + 1, :]
            ms.append((cb * jnp.exp(jnp.where(mask, seg, NEG_BIG))).astype(BF16))
        x_lo = jnp.where(lane_lo, xdt, 0.0).astype(BF16)
        x_hi = jnp.where(lane_lo, 0.0, xdt).astype(BF16)
        y = _dot(jnp.concatenate(ms, axis=1), jnp.concatenate([x_lo, x_hi], axis=0))
        y_scr[:, cl] = y + dsk_ref[:, cl] * xs
        u_scr[:, cl] = xdt * jnp.exp(last_x - cum_x)
        cd_scr[:, cl] = jnp.exp(last_x)
        fs_scr[:, cl] = jnp.exp(cum_x)

    ones_tail = _ones_tail()
    zeros8 = jnp.zeros((SEG, M_DSTATE), F32)

    def pair_body(m, carry):
        r16 = pl.ds(pl.multiple_of(m * 2 * SEG, 2 * SEG), 2 * SEG)
        for par in range(2):
            i = 2 * m + par
            r8 = pl.ds(pl.multiple_of(i * SEG, SEG), SEG)
            for grp in range(M_GROUPS):
                gl = slice(grp * M_GROUP_WIDTH, (grp + 1) * M_GROUP_WIDTH)
                nl = slice(grp * M_DSTATE, (grp + 1) * M_DSTATE)
                s0 = s0_ref[i, gl, :]
                yi = _dot_nt(cm_scr[r16, nl], s0.astype(BF16))
                y_scr[r8, gl] = y_scr[r8, gl] + yi[par * SEG:(par + 1) * SEG, :] * fs_scr[r8, gl]
                aug = jnp.concatenate([u_scr[r8, gl], _decay_tail(cd_scr[r8, gl])], axis=0).astype(BF16)
                rhs = jnp.concatenate(
                    [jnp.concatenate([bm_scr[r8, nl], zeros8], axis=0).astype(BF16), ones_tail], axis=1)
                ud = _dot_tn(aug, rhs)
                s8_ref[i, gl, :] = ud[:, M_DSTATE:] * s0 + ud[:, :M_DSTATE]
        return carry

    lax.fori_loop(0, nb // 2, pair_body, 0, unroll=True)

    for grp in range(M_GROUPS):
        gl = slice(grp * M_GROUP_WIDTH, (grp + 1) * M_GROUP_WIDTH)
        y = y_scr[:, gl] * _silu(z_ref[:, gl])
        mixm_ref[:, gl] = _rms(y, mn_ref[:, gl]).astype(BF16)


def _sample_ssd(proj, c0, s0, cw, cb, dtb, alog, dsk, mn, *, nb=16):
    nseq = s0.shape[0]
    rows = nb * SEG
    kern = functools.partial(_sample_ssd_kernel, nb=nb)
    params = (cw, cb, dtb, alog, dsk, mn)
    conv_spec = pl.BlockSpec((M_CONV - 1, nb, M_CONV_DIM), lambda i: (0, i, 0))
    return pl.pallas_call(
        kern,
        grid=(nseq // nb,),
        in_specs=[pl.BlockSpec((rows, XD_WIDTH), lambda i: (i, OFF_XBC // XD_WIDTH)),
                  pl.BlockSpec((rows, M_WIDTH), lambda i: (i, OFF_Z // M_WIDTH)),
                  conv_spec,
                  pl.BlockSpec((nb, M_WIDTH, M_DSTATE), lambda i: (i, 0, 0))]
                 + [_const_spec(p.shape) for p in params],
        out_specs=[pl.BlockSpec((rows, M_WIDTH), lambda i: (i, 0)),
                   pl.BlockSpec((nb, M_WIDTH, M_DSTATE), lambda i: (i, 0, 0)),
                   conv_spec],
        out_shape=[jax.ShapeDtypeStruct((nseq * SEG, M_WIDTH), BF16),
                   jax.ShapeDtypeStruct(s0.shape, F32),
                   jax.ShapeDtypeStruct(c0.shape, F32)],
        scratch_shapes=[pltpu.VMEM((M_CONV_DIM // LANE, rows, LANE), F32),
                        pltpu.VMEM((M_CONV_DIM // LANE, rows, LANE), F32),
                        pltpu.VMEM((rows, M_WIDTH), F32),
                        pltpu.VMEM((rows, M_GROUPS * M_DSTATE), F32),
                        pltpu.VMEM((rows, M_GROUPS * M_DSTATE), BF16),
                        pltpu.VMEM((rows, M_WIDTH), F32),
                        pltpu.VMEM((rows, M_WIDTH), F32),
                        pltpu.VMEM((rows, M_WIDTH), F32),
                        pltpu.VMEM((rows, M_WIDTH), F32)],
        compiler_params=pltpu.CompilerParams(
            dimension_semantics=("arbitrary",), vmem_limit_bytes=VMEM_LIMIT),
        name="sample_ssd",
    )(proj, proj, c0, s0, *params)


def _out_mlp_kernel(x_ref, mh_ref, mm_ref, wo_ref, ln2_ref, wu_ref, wd_ref, lnf_ref, o_ref, *, ff_tile):
    x1 = x_ref[...] + _dot(mh_ref[...], wo_ref[0:HG_WIDTH, :]) + _dot(mm_ref[...], wo_ref[HG_WIDTH:, :])
    hn = _rms(x1, ln2_ref[...]).astype(BF16)
    mlp = None
    for j in range(D_FF // ff_tile):
        u = jnp.maximum(_dot(hn, wu_ref[:, j * ff_tile:(j + 1) * ff_tile]), 0.0)
        d = _dot((u * u).astype(BF16), wd_ref[j * ff_tile:(j + 1) * ff_tile, :])
        mlp = d if mlp is None else mlp + d
    o_ref[...] = _rms(x1 + mlp, lnf_ref[...])


def _out_mlp(x, mix_h, mix_m, w_out, ln2, w_up, w_down, ln_f, *, tm=512, ff_tile=1024):
    rows = x.shape[0]
    kern = functools.partial(_out_mlp_kernel, ff_tile=ff_tile)
    row_spec = lambda w: pl.BlockSpec((tm, w), lambda i: (i, 0))
    return pl.pallas_call(
        kern,
        grid=(rows // tm,),
        in_specs=[row_spec(D_MODEL), row_spec(HG_WIDTH), row_spec(M_WIDTH),
                  _const_spec(w_out.shape), _const_spec(ln2.shape), _const_spec(w_up.shape),
                  _const_spec(w_down.shape), _const_spec(ln_f.shape)],
        out_specs=row_spec(D_MODEL),
        out_shape=jax.ShapeDtypeStruct((rows, D_MODEL), F32),
        compiler_params=pltpu.CompilerParams(
            dimension_semantics=("arbitrary",), vmem_limit_bytes=VMEM_LIMIT),
        name="out_mlp",
    )(x, mix_h, mix_m, w_out, ln2, w_up, w_down, ln_f)


def _pad_lanes(v):
    return jnp.pad(v, ((0, 0), (0, LANE - v.shape[1])))


def kernel(x_prompt, x_sample, state_hgrn, state_ssm, state_conv, hg_lb_logits, ln1, w_in, hg_norm, conv_w,
           conv_b, dt_bias, a_log, d_skip, m_norm, w_out, ln2, w_up, w_down, ln_f):
    l = 0
    bp, seq, _ = x_prompt.shape
    bs, dseq, _ = x_sample.shape
    assert DEPTH == 1 and dseq == SEG

    w_all = _pack_in_proj(jnp.transpose(w_in[l]))
    lbl = hg_lb_logits.astype(F32)
    ln1_r = ln1[l][None, :]
    hgn_r = hg_norm[l].reshape(1, HG_WIDTH)
    cw = conv_w[l]
    cb_r = conv_b[l][None, :]
    dtb_r = _pad_lanes(dt_bias[l][None, :])
    alog_r = _pad_lanes(a_log[l][None, :])
    dsk_r = jnp.repeat(d_skip[l], M_HEADDIM)[None, :]
    mn_r = m_norm[l][None, :]
    mlp_w = (w_out[l].astype(BF16), ln2[l][None, :], w_up[l].astype(BF16), w_down[l].astype(BF16), ln_f[None, :])

    mixh_p, mixm_p, hgs_p, ssm_p, conv_p = _prompt_mixer(
        x_prompt, lbl, ln1_r, w_all, hgn_r, cw, cb_r, dtb_r, alog_r, dsk_r, mn_r)
    y_p = _out_mlp(x_prompt.reshape(bp * seq, D_MODEL), mixh_p.reshape(bp * seq, HG_WIDTH),
                   mixm_p.reshape(bp * seq, M_WIDTH), *mlp_w)

    xs2 = x_sample.reshape(bs * SEG, D_MODEL)
    proj_s = _sample_in_proj(xs2, ln1_r, w_all)
    mixh_s, hgs_s = _sample_hgrn(proj_s, state_hgrn[l], lbl, hgn_r)
    conv0 = jnp.transpose(state_conv[l], (1, 0, 2))
    mixm_s, ssm_s, conv8 = _sample_ssd(proj_s, conv0, state_ssm[l].reshape(bs, M_WIDTH, M_DSTATE),
                                       cw, cb_r, dtb_r, alog_r, dsk_r, mn_r)
    y_s = _out_mlp(xs2, mixh_s, mixm_s, *mlp_w)
    conv_s = jnp.transpose(conv8, (1, 0, 2))

    return (y_p.reshape(bp, seq, D_MODEL), y_s.reshape(bs, SEG, D_MODEL),
            hgs_p[None], hgs_s[None],
            ssm_p.reshape(1, bp, M_HEADS, M_HEADDIM, M_DSTATE), ssm_s.reshape(1, bs, M_HEADS, M_HEADDIM, M_DSTATE),
            conv_p[None], conv_s[None])
```

```python
import functools

import jax
import jax.numpy as jnp
from jax import lax
from jax.experimental import pallas as pl
from jax.experimental.pallas import tpu as pltpu

F32 = jnp.float32
BF16 = jnp.bfloat16

D_MODEL = 1024
DEPTH = 1
HG_HEADS = 8
HG_DK = 128
HG_DV = 128
HG_WIDTH = HG_HEADS * HG_DV
M_WIDTH = 1024
M_HEADDIM = 64
M_HEADS = M_WIDTH // M_HEADDIM
M_DSTATE = 128
M_GROUPS = 2
M_GROUP_WIDTH = M_WIDTH // M_GROUPS
M_CONV = 4
M_CONV_DIM = M_WIDTH + 2 * M_GROUPS * M_DSTATE
D_FF = 4 * D_MODEL
NORM_EPS = 1e-5

LANE = 128
SUBLANE = 8

OFF_Q, OFF_F, OFF_I, OFF_G = 0, 1024, 2048, 3072
HG_PROJ_WIDTH = 4096
OFF_XBC = HG_PROJ_WIDTH
OFF_DT = OFF_XBC + M_CONV_DIM
XD_WIDTH = 2048
OFF_Z = OFF_XBC + XD_WIDTH
W_IN_COLS = OFF_Z + M_WIDTH
PACK_TILE = 1024

VMEM_LIMIT = 56 * 1024 * 1024

NEG_BIG = -1e30


def _dot(a, b):
    return jnp.dot(a, b, preferred_element_type=F32)


def _dot_nt(a, b):
    return lax.dot_general(a, b, (((1,), (1,)), ((), ())), preferred_element_type=F32)


def _dot_tn(a, b):
    return lax.dot_general(a, b, (((0,), (0,)), ((), ())), preferred_element_type=F32)


def _silu(x):
    hx = 0.5 * x
    return hx + hx * jnp.tanh(hx)


def _forget_gate(fr, lb):
    half = 0.5 * (1.0 - lb)
    return (lb + half) + half * jnp.tanh(0.5 * fr)


def _softplus(x):
    return jnp.maximum(x, 0.0) + jnp.log(1.0 + jnp.exp(-jnp.abs(x)))


def _rms(x, gain):
    ms = jnp.mean(x * x, axis=-1, keepdims=True)
    return x * lax.rsqrt(ms + NORM_EPS) * gain


def _lower_bound(lbl, layer):
    rows = [lbl[i:i + 1, :] for i in range(DEPTH + 1)]
    m = functools.reduce(jnp.maximum, rows)
    es = [jnp.exp(r - m) for r in rows]
    return sum(es[:layer + 1]) / sum(es)


def _cumsum_rows(x, seg):
    t = lax.broadcasted_iota(jnp.int32, x.shape, 0) & (seg - 1)
    s = 1
    while s < seg:
        x = x + jnp.where(t >= s, pltpu.roll(x, s, 0), 0.0)
        s *= 2
    return x


def _split3(x):
    h = x.astype(BF16)
    r = x - h.astype(F32)
    m = r.astype(BF16)
    l = (r - m.astype(F32)).astype(BF16)
    return h, m, l


def _lane_bcast(x, lane):
    return jnp.broadcast_to(x[:, lane:lane + 1], x.shape)


def _pack_in_proj_kernel(wt_ref, o_ref, *, tn, n_hg, n_xd, src_cols):
    j = pl.program_id(0)
    col0 = _pack_src_tile(j, n_hg, n_xd) * tn
    col = col0 + lax.broadcasted_iota(jnp.int32, wt_ref.shape, 0)
    o_ref[...] = jnp.where(col < src_cols, wt_ref[...], 0.0).T.astype(BF16)


def _pack_src_tile(j, n_hg, n_xd):
    n_z = M_WIDTH // PACK_TILE
    return jnp.where(j < n_hg, j, jnp.where(j < n_hg + n_xd, j + n_z, j - n_xd))


def _pack_in_proj(wt):
    tn = PACK_TILE
    n_hg, n_xd = HG_PROJ_WIDTH // tn, XD_WIDTH // tn
    kern = functools.partial(_pack_in_proj_kernel, tn=tn, n_hg=n_hg, n_xd=n_xd, src_cols=wt.shape[0])
    return pl.pallas_call(
        kern,
        grid=(W_IN_COLS // tn,),
        in_specs=[pl.BlockSpec((tn, D_MODEL), lambda j: (_pack_src_tile(j, n_hg, n_xd), 0))],
        out_specs=pl.BlockSpec((D_MODEL, tn), lambda j: (0, j)),
        out_shape=jax.ShapeDtypeStruct((D_MODEL, W_IN_COLS), BF16),
        compiler_params=pltpu.CompilerParams(
            dimension_semantics=("arbitrary",), vmem_limit_bytes=VMEM_LIMIT),
        name="pack_in_proj",
    )(wt)


def _in_proj(x, ln1_ref, w_ref, bufs, tb):
    h_scr, hg_scr, z_scr, xbc_scr, dt_scr = bufs
    h_scr[...] = _rms(x, ln1_ref[...]).astype(BF16)
    for n0 in range(0, HG_PROJ_WIDTH, 1024):
        hg_scr[:, n0:n0 + 1024] = _dot(h_scr[...], w_ref[:, n0:n0 + 1024])
    xbc_scr[SUBLANE:SUBLANE + tb, :] = _dot(h_scr[...], w_ref[:, OFF_XBC:OFF_DT])
    dt_scr[...] = _dot(h_scr[...], w_ref[:, OFF_DT:OFF_DT + LANE])
    z_scr[...] = _dot(h_scr[...], w_ref[:, OFF_Z:OFF_Z + M_WIDTH])


def _recurrences(bufs, prm, mixh_ref, mixm_ref, work, state, tb, ch, cs):
    _, hg_scr, z_scr, xbc_scr, dt_scr = bufs
    lbl_ref, hgn_ref, cw_ref, cb_ref, dtb_ref, alog_ref, dsk_ref, mn_ref = prm
    xs_scr, bc_scr, y_scr = work
    sth_scr, stm_scr, hist_scr = state

    xbc_scr[0:SUBLANE, :] = hist_scr[...]

    lb = _lower_bound(lbl_ref[...], 0)
    tril_h = (lax.broadcasted_iota(jnp.int32, (ch, ch), 0) >= lax.broadcasted_iota(jnp.int32, (ch, ch), 1))
    for j in range(tb // ch):
        r = slice(j * ch, (j + 1) * ch)
        for hd in range(HG_HEADS):
            cl = slice(hd * LANE, (hd + 1) * LANE)
            q = hg_scr[r, OFF_Q + hd * LANE:OFF_Q + (hd + 1) * LANE]
            fr = hg_scr[r, OFF_F + hd * LANE:OFF_F + (hd + 1) * LANE]
            v = hg_scr[r, OFF_I + hd * LANE:OFF_I + (hd + 1) * LANE]
            g = hg_scr[r, OFF_G + hd * LANE:OFF_G + (hd + 1) * LANE]
            lbh = lb[:, cl]
            f = _forget_gate(fr, lbh)
            a = jnp.log(f)
            k = 1.0 - f
            cum = _cumsum_rows(a, ch)
            hh = ch // 2
            cum_a, cum_b = cum[:hh, :], cum[hh:, :]
            q_a, q_b, k_a, k_b = q[:hh, :], q[hh:, :], k[:hh, :], k[hh:, :]
            mid_a = cum[hh // 2 - 1:hh // 2, :]
            mid_b = cum[hh + hh // 2 - 1:hh + hh // 2, :]
            edge = cum[hh - 1:hh, :]
            last = cum[ch - 1:ch, :]
            qa_mid = q_a * jnp.exp(cum_a - mid_a)
            ka_edge = k_a * jnp.exp(edge - cum_a)
            kb_mid = k_b * jnp.exp(mid_b - cum_b)
            qb_edge = q_b * jnp.exp(cum_b - edge)
            zero = jnp.zeros((hh, LANE), F32)
            q_fact = jnp.concatenate([jnp.concatenate([qa_mid, zero, zero], axis=1),
                                      jnp.concatenate([zero, q_b * jnp.exp(cum_b - mid_b), qb_edge], axis=1)], axis=0)
            k_fact = jnp.concatenate([jnp.concatenate([k_a * jnp.exp(mid_a - cum_a), zero, ka_edge], axis=1),
                                      jnp.concatenate([zero, kb_mid, zero], axis=1)], axis=0)
            qd = jnp.concatenate([qa_mid * jnp.exp(mid_a), qb_edge * jnp.exp(edge)], axis=0).astype(BF16)
            k_end = jnp.concatenate([ka_edge * jnp.exp(last - edge), kb_mid * jnp.exp(last - mid_b)],
                                    axis=0).astype(BF16)
            dec = jnp.exp(last)
            vb = v.astype(BF16)
            att = _dot_nt(q_fact.astype(BF16), k_fact.astype(BF16))
            att = jnp.where(tril_h, att, 0.0).astype(BF16)
            st = sth_scr[hd]
            o = _dot(att, vb) + _dot_nt(qd, st.astype(BF16))
            sth_scr[hd] = dec * st + _dot_tn(vb, k_end)
            on = _rms(o, hgn_ref[:, cl]) * _silu(g)
            mixh_ref[0, r, cl] = on.astype(BF16)

    for cbk in range(M_CONV_DIM // LANE):
        cl = slice(cbk * LANE, (cbk + 1) * LANE)
        acc = cb_ref[:, cl] + cw_ref[M_CONV - 1:M_CONV, cl] * xbc_scr[SUBLANE:SUBLANE + tb, cl]
        for d in range(1, M_CONV):
            acc = acc + cw_ref[M_CONV - 1 - d:M_CONV - d, cl] * xbc_scr[SUBLANE - d:SUBLANE - d + tb, cl]
        act = _silu(acc)
        if cbk < M_WIDTH // LANE:
            xs_scr[:, cl] = act
        else:
            bc_scr[:, (cbk * LANE - M_WIDTH):(cbk * LANE - M_WIDTH) + LANE] = act.astype(BF16)
    hist_scr[...] = xbc_scr[tb:tb + SUBLANE, :]

    a_row = -jnp.exp(alog_ref[...])
    tril_m = (lax.broadcasted_iota(jnp.int32, (cs, cs), 0) >= lax.broadcasted_iota(jnp.int32, (cs, cs), 1))
    lane_lo = lax.broadcasted_iota(jnp.int32, (cs, LANE), 1) < M_HEADDIM
    for s in range(tb // cs):
        r = slice(s * cs, (s + 1) * cs)
        dtv = _softplus(dt_scr[r, :] + dtb_ref[...])
        cum = _cumsum_rows(dtv * a_row, cs)
        cum_t = cum.T
        cbs = {}
        for p in range(M_HEADS // 2):
            grp = p // (M_HEADS // 2 // M_GROUPS)
            cl = slice(p * LANE, (p + 1) * LANE)
            bm = bc_scr[r, grp * M_DSTATE:(grp + 1) * M_DSTATE]
            cm = bc_scr[r, (M_GROUPS + grp) * M_DSTATE:(M_GROUPS + grp + 1) * M_DSTATE]
            if grp not in cbs:
                cbs[grp] = _dot_nt(cm, bm)
            cb = cbs[grp]
            cum_h = [_lane_bcast(cum, hh) for hh in (2 * p, 2 * p + 1)]
            cum_x = jnp.where(lane_lo, cum_h[0], cum_h[1])
            dt_x = jnp.where(lane_lo, _lane_bcast(dtv, 2 * p), _lane_bcast(dtv, 2 * p + 1))
            last_x = cum_x[cs - 1:cs, :]
            xs = xs_scr[r, cl]
            xdt = xs * dt_x
            ms = []
            for i, hh in enumerate((2 * p, 2 * p + 1)):
                seg = cum_h[i] - cum_t[hh:hh + 1, :]
                ms.append((cb * jnp.exp(jnp.where(tril_m, seg, NEG_BIG))).astype(BF16))
            x_lo = jnp.where(lane_lo, xdt, 0.0).astype(BF16)
            x_hi = jnp.where(lane_lo, 0.0, xdt).astype(BF16)
            y = _dot(jnp.concatenate(ms, axis=1), jnp.concatenate([x_lo, x_hi], axis=0))
            st = stm_scr[:, cl]
            y = y + _dot(cm, st.astype(BF16)) * jnp.exp(cum_x)
            y = y + dsk_ref[:, cl] * xs
            to_end = (xdt * jnp.exp(last_x - cum_x)).astype(BF16)
            stm_scr[:, cl] = jnp.exp(last_x) * st + _dot_tn(bm, to_end)
            y_scr[r, cl] = y * _silu(z_scr[r, cl])
        for grp in range(M_GROUPS):
            gl = slice(grp * M_GROUP_WIDTH, (grp + 1) * M_GROUP_WIDTH)
            mixm_ref[0, r, gl] = _rms(y_scr[r, gl], mn_ref[:, gl]).astype(BF16)


def _prompt_mixer_kernel(x_ref, lbl_ref, ln1_ref, w_ref, hgn_ref, cw_ref, cb_ref, dtb_ref, alog_ref, dsk_ref, mn_ref,
                         mixh_ref, mixm_ref, hgs_ref, ssm_ref, conv_ref,
                         h_scr, hg_scr, z_scr, xbc_scr, dt_scr, xs_scr, bc_scr, y_scr, sth_scr, stm_scr, hist_scr,
                         *, tb, ch, cs):
    c = pl.program_id(1)
    last_c = pl.num_programs(1) - 1

    @pl.when(c == 0)
    def _():
        sth_scr[...] = jnp.zeros(sth_scr.shape, F32)
        stm_scr[...] = jnp.zeros(stm_scr.shape, F32)
        hist_scr[...] = jnp.zeros(hist_scr.shape, F32)

    bufs = (h_scr, hg_scr, z_scr, xbc_scr, dt_scr)
    _in_proj(x_ref[0], ln1_ref, w_ref, bufs, tb)
    prm = (lbl_ref, hgn_ref, cw_ref, cb_ref, dtb_ref, alog_ref, dsk_ref, mn_ref)
    _recurrences(bufs, prm, mixh_ref, mixm_ref, (xs_scr, bc_scr, y_scr), (sth_scr, stm_scr, hist_scr), tb, ch, cs)

    @pl.when(c == last_c)
    def _():
        for hd in range(HG_HEADS):
            hgs_ref[0, hd] = sth_scr[hd].T
        for p in range(M_WIDTH // LANE):
            ssm_ref[0, p * LANE:(p + 1) * LANE, :] = stm_scr[:, p * LANE:(p + 1) * LANE].T
        conv_ref[0] = hist_scr[SUBLANE - (M_CONV - 1):SUBLANE, :]


def _const_spec(shape):
    nd = len(shape)
    return pl.BlockSpec(shape, lambda *_: (0,) * nd, pipeline_mode=pl.Buffered(1))


def _prompt_mixer(x, lbl, ln1, w_all, hgn, cw, cb, dtb, alog, dsk, mn, *, tb=512, ch=64, cs=128):
    bsz, seq, _ = x.shape
    nc = seq // tb
    kern = functools.partial(_prompt_mixer_kernel, tb=tb, ch=ch, cs=cs)
    params = (lbl, ln1, w_all, hgn, cw, cb, dtb, alog, dsk, mn)
    return pl.pallas_call(
        kern,
        grid=(bsz, nc),
        in_specs=[pl.BlockSpec((1, tb, D_MODEL), lambda b, c: (b, c, 0))] + [_const_spec(p.shape) for p in params],
        out_specs=[
            pl.BlockSpec((1, tb, HG_WIDTH), lambda b, c: (b, c, 0)),
            pl.BlockSpec((1, tb, M_WIDTH), lambda b, c: (b, c, 0)),
            pl.BlockSpec((1, HG_HEADS, HG_DK, HG_DV), lambda b, c: (b, 0, 0, 0)),
            pl.BlockSpec((1, M_WIDTH, M_DSTATE), lambda b, c: (b, 0, 0)),
            pl.BlockSpec((1, M_CONV - 1, M_CONV_DIM), lambda b, c: (b, 0, 0)),
        ],
        out_shape=[
            jax.ShapeDtypeStruct((bsz, seq, HG_WIDTH), BF16),
            jax.ShapeDtypeStruct((bsz, seq, M_WIDTH), BF16),
            jax.ShapeDtypeStruct((bsz, HG_HEADS, HG_DK, HG_DV), F32),
            jax.ShapeDtypeStruct((bsz, M_WIDTH, M_DSTATE), F32),
            jax.ShapeDtypeStruct((bsz, M_CONV - 1, M_CONV_DIM), F32),
        ],
        scratch_shapes=[
            pltpu.VMEM((tb, D_MODEL), BF16),
            pltpu.VMEM((tb, HG_PROJ_WIDTH), F32),
            pltpu.VMEM((tb, M_WIDTH), F32),
            pltpu.VMEM((tb + SUBLANE, M_CONV_DIM), F32),
            pltpu.VMEM((tb, LANE), F32),
            pltpu.VMEM((tb, M_WIDTH), F32),
            pltpu.VMEM((tb, 2 * M_GROUPS * M_DSTATE), BF16),
            pltpu.VMEM((tb, M_WIDTH), F32),
            pltpu.VMEM((HG_HEADS, HG_DV, HG_DK), F32),
            pltpu.VMEM((M_DSTATE, M_WIDTH), F32),
            pltpu.VMEM((SUBLANE, M_CONV_DIM), F32),
        ],
        compiler_params=pltpu.CompilerParams(
            dimension_semantics=("arbitrary", "arbitrary"), vmem_limit_bytes=VMEM_LIMIT),
        name="prompt_mixer",
    )(x, *params)


def _sample_in_proj_kernel(x_ref, ln1_ref, w_ref, o_ref, h_scr):
    @pl.when(pl.program_id(0) == 0)
    def _():
        h_scr[...] = _rms(x_ref[...], ln1_ref[...]).astype(BF16)

    o_ref[...] = _dot(h_scr[...], w_ref[...])


def _sample_in_proj(x, ln1, w_all, *, tn=1024):
    rows = x.shape[0]
    return pl.pallas_call(
        _sample_in_proj_kernel,
        grid=(W_IN_COLS // tn,),
        in_specs=[_const_spec((rows, D_MODEL)), _const_spec(ln1.shape),
                  pl.BlockSpec((D_MODEL, tn), lambda n: (0, n))],
        out_specs=pl.BlockSpec((rows, tn), lambda n: (0, n)),
        out_shape=jax.ShapeDtypeStruct((rows, W_IN_COLS), F32),
        scratch_shapes=[pltpu.VMEM((rows, D_MODEL), BF16)],
        compiler_params=pltpu.CompilerParams(
            dimension_semantics=("arbitrary",), vmem_limit_bytes=VMEM_LIMIT),
        name="sample_in_proj",
    )(x, ln1, w_all)


SEG = 8


def _block_causal_mask(rows):
    ri = lax.broadcasted_iota(jnp.int32, (rows, rows), 0)
    ci = lax.broadcasted_iota(jnp.int32, (rows, rows), 1)
    return ((ri & -SEG) == (ci & -SEG)) & (ci <= ri)


def _seg_last(x):
    rows, width = x.shape
    x3 = x.reshape(rows // SEG, SEG, width)
    return jnp.broadcast_to(x3[:, SEG - 1:SEG, :], x3.shape).reshape(rows, width)


def _decay_tail(dec8):
    h, m, l = _split3(dec8)
    t = lax.broadcasted_iota(jnp.int32, dec8.shape, 0)
    return jnp.where(t == 0, h.astype(F32), jnp.where(t == 1, m.astype(F32), jnp.where(t == 2, l.astype(F32), 0.0)))


def _ones_tail():
    t = lax.broadcasted_iota(jnp.int32, (2 * SEG, LANE), 0)
    return jnp.where((t >= SEG) & (t < SEG + 3), 1.0, 0.0).astype(BF16)


def _sample_hgrn_kernel(p_ref, s0_ref, lbl_ref, hgn_ref, mixh_ref, s8_ref,
                        qd_scr, ke_scr, dec_scr, o_scr, *, nb):
    rows = nb * SEG
    lb = _lower_bound(lbl_ref[...], 0)
    mask = _block_causal_mask(rows)
    for hd in range(HG_HEADS):
        cl = slice(hd * LANE, (hd + 1) * LANE)
        q = p_ref[:, OFF_Q + hd * LANE:OFF_Q + (hd + 1) * LANE]
        fr = p_ref[:, OFF_F + hd * LANE:OFF_F + (hd + 1) * LANE]
        v = p_ref[:, OFF_I + hd * LANE:OFF_I + (hd + 1) * LANE]
        lbh = lb[:, cl]
        f = _forget_gate(fr, lbh)
        a = jnp.log(f)
        k = 1.0 - f
        cum = _cumsum_rows(a, SEG)
        last = _seg_last(cum)
        qd = (q * jnp.exp(cum)).astype(BF16)
        kd = (k * jnp.exp(-cum)).astype(BF16)
        att = jnp.where(mask, _dot_nt(qd, kd), 0.0).astype(BF16)
        o_scr[:, cl] = _dot(att, v.astype(BF16))
        qd_scr[:, cl] = qd
        ke_scr[:, cl] = k * jnp.exp(last - cum)
        dec_scr[:, cl] = jnp.exp(last)

    ones_tail = _ones_tail()
    zeros8 = jnp.zeros((SEG, LANE), F32)

    def pair_body(m, carry):
        r16 = pl.ds(pl.multiple_of(m * 2 * SEG, 2 * SEG), 2 * SEG)
        for par in range(2):
            i = 2 * m + par
            r8 = pl.ds(pl.multiple_of(i * SEG, SEG), SEG)
            for hd in range(HG_HEADS):
                cl = slice(hd * LANE, (hd + 1) * LANE)
                s0 = s0_ref[i, hd]
                oi = _dot(qd_scr[r16, cl], s0.astype(BF16))
                o_scr[r8, cl] = o_scr[r8, cl] + oi[par * SEG:(par + 1) * SEG, :]
                aug = jnp.concatenate([ke_scr[r8, cl], _decay_tail(dec_scr[r8, cl])], axis=0).astype(BF16)
                v8 = p_ref[r8, OFF_I + hd * LANE:OFF_I + (hd + 1) * LANE]
                rhs = jnp.concatenate([jnp.concatenate([v8, zeros8], axis=0).astype(BF16), ones_tail], axis=1)
                ud = _dot_tn(aug, rhs)
                s8_ref[i, hd] = ud[:, LANE:] * s0 + ud[:, :LANE]
        return carry

    lax.fori_loop(0, nb // 2, pair_body, 0, unroll=True)

    for hd in range(HG_HEADS):
        cl = slice(hd * LANE, (hd + 1) * LANE)
        g = p_ref[:, OFF_G + hd * LANE:OFF_G + (hd + 1) * LANE]
        mixh_ref[:, cl] = (_rms(o_scr[:, cl], hgn_ref[:, cl]) * _silu(g)).astype(BF16)


def _sample_hgrn(proj, s0, lbl, hgn, *, nb=16):
    nseq = s0.shape[0]
    rows = nb * SEG
    kern = functools.partial(_sample_hgrn_kernel, nb=nb)
    return pl.pallas_call(
        kern,
        grid=(nseq // nb,),
        in_specs=[pl.BlockSpec((rows, OFF_XBC), lambda i: (i, 0)),
                  pl.BlockSpec((nb, HG_HEADS, HG_DK, HG_DV), lambda i: (i, 0, 0, 0)),
                  _const_spec(lbl.shape), _const_spec(hgn.shape)],
        out_specs=[pl.BlockSpec((rows, HG_WIDTH), lambda i: (i, 0)),
                   pl.BlockSpec((nb, HG_HEADS, HG_DK, HG_DV), lambda i: (i, 0, 0, 0))],
        out_shape=[jax.ShapeDtypeStruct((nseq * SEG, HG_WIDTH), BF16),
                   jax.ShapeDtypeStruct(s0.shape, F32)],
        scratch_shapes=[pltpu.VMEM((rows, HG_WIDTH), BF16),
                        pltpu.VMEM((rows, HG_WIDTH), F32),
                        pltpu.VMEM((rows, HG_WIDTH), F32),
                        pltpu.VMEM((rows, HG_WIDTH), F32)],
        compiler_params=pltpu.CompilerParams(
            dimension_semantics=("arbitrary",), vmem_limit_bytes=VMEM_LIMIT),
        name="sample_hgrn",
    )(proj, s0, lbl, hgn)


def _sample_ssd_kernel(xd_ref, z_ref, c0_ref, s0_ref, cw_ref, cb_ref, dtb_ref, alog_ref, dsk_ref, mn_ref,
                       mixm_ref, s8_ref, c8_ref,
                       hist_scr, raw_scr, xs_scr, bm_scr, cm_scr, u_scr, cd_scr, fs_scr, y_scr, *, nb):
    rows = nb * SEG
    t_in_seq = lax.broadcasted_iota(jnp.int32, (rows, LANE), 0) & (SEG - 1)

    @pl.when(pl.program_id(0) == 0)
    def _():
        hist_scr[...] = jnp.zeros(hist_scr.shape, F32)

    for cbk in range(M_CONV_DIM // LANE):
        cl = slice(cbk * LANE, (cbk + 1) * LANE)
        x = xd_ref[:, cl]
        raw_scr[cbk] = x
        for d in range(M_CONV - 1):
            seq_rows = pl.ds(SEG - (M_CONV - 1) + d, nb, stride=SEG)
            hist_scr[cbk, seq_rows, :] = c0_ref[d, :, cl]
            c8_ref[d, :, cl] = raw_scr[cbk, seq_rows, :]
        hist = hist_scr[cbk]
        acc = cb_ref[:, cl] + cw_ref[M_CONV - 1:M_CONV, cl] * x
        for d in range(1, M_CONV):
            shifted = jnp.where(t_in_seq >= d, pltpu.roll(x, d, 0), pltpu.roll(hist, rows - SEG + d, 0))
            acc = acc + cw_ref[M_CONV - 1 - d:M_CONV - d, cl] * shifted
        act = _silu(acc)
        if cbk < M_WIDTH // LANE:
            xs_scr[:, cl] = act
        elif cbk < (M_WIDTH + M_GROUPS * M_DSTATE) // LANE:
            bm_scr[:, cbk * LANE - M_WIDTH:(cbk + 1) * LANE - M_WIDTH] = act
        else:
            o0 = cbk * LANE - M_WIDTH - M_GROUPS * M_DSTATE
            cm_scr[:, o0:o0 + LANE] = act.astype(BF16)

    a_row = -jnp.exp(alog_ref[...])
    dtv = _softplus(xd_ref[:, M_CONV_DIM:M_CONV_DIM + LANE] + dtb_ref[...])
    cum = _cumsum_rows(dtv * a_row, SEG)
    cum_t = cum.T
    mask = _block_causal_mask(rows)
    lane_lo = lax.broadcasted_iota(jnp.int32, (rows, LANE), 1) < M_HEADDIM
    cbs = {}
    for p in range(M_HEADS // 2):
        grp = p // (M_HEADS // 2 // M_GROUPS)
        cl = slice(p * LANE, (p + 1) * LANE)
        if grp not in cbs:
            bm = bm_scr[:, grp * M_DSTATE:(grp + 1) * M_DSTATE].astype(BF16)
            cbs[grp] = _dot_nt(cm_scr[:, grp * M_DSTATE:(grp + 1) * M_DSTATE], bm)
        cb = cbs[grp]
        cum_h = [_lane_bcast(cum, hh) for hh in (2 * p, 2 * p + 1)]
        cum_x = jnp.where(lane_lo, cum_h[0], cum_h[1])
        dt_x = jnp.where(lane_lo, _lane_bcast(dtv, 2 * p), _lane_bcast(dtv, 2 * p + 1))
        last_x = _seg_last(cum_x)
        xs = xs_scr[:, cl]
        xdt = xs * dt_x
        ms = []
        for i, hh in enumerate((2 * p, 2 * p + 1)):
            seg = cum_h[i] - cum_t[hh:hh + 1, :]
            ms.append((cb * jnp.exp(jnp.where(mask, seg, NEG_BIG))).astype(BF16))
        x_lo = jnp.where(lane_lo, xdt, 0.0).astype(BF16)
        x_hi = jnp.where(lane_lo, 0.0, xdt).astype(BF16)
        y = _dot(jnp.concatenate(ms, axis=1), jnp.concatenate([x_lo, x_hi], axis=0))
        y_scr[:, cl] = y + dsk_ref[:, cl] * xs
        u_scr[:, cl] = xdt * jnp.exp(last_x - cum_x)
        cd_scr[:, cl] = jnp.exp(last_x)
        fs_scr[:, cl] = jnp.exp(cum_x)

    ones_tail = _ones_tail()
    zeros8 = jnp.zeros((SEG, M_DSTATE), F32)

    def pair_body(m, carry):
        r16 = pl.ds(pl.multiple_of(m * 2 * SEG, 2 * SEG), 2 * SEG)
        for par in range(2):
            i = 2 * m + par
            r8 = pl.ds(pl.multiple_of(i * SEG, SEG), SEG)
            for grp in range(M_GROUPS):
                gl = slice(grp * M_GROUP_WIDTH, (grp + 1) * M_GROUP_WIDTH)
                nl = slice(grp * M_DSTATE, (grp + 1) * M_DSTATE)
                s0 = s0_ref[i, gl, :]
                yi = _dot_nt(cm_scr[r16, nl], s0.astype(BF16))
                y_scr[r8, gl] = y_scr[r8, gl] + yi[par * SEG:(par + 1) * SEG, :] * fs_scr[r8, gl]
                aug = jnp.concatenate([u_scr[r8, gl], _decay_tail(cd_scr[r8, gl])], axis=0).astype(BF16)
                rhs = jnp.concatenate(
                    [jnp.concatenate([bm_scr[r8, nl], zeros8], axis=0).astype(BF16), ones_tail], axis=1)
                ud = _dot_tn(aug, rhs)
                s8_ref[i, gl, :] = ud[:, M_DSTATE:] * s0 + ud[:, :M_DSTATE]
        return carry

    lax.fori_loop(0, nb // 2, pair_body, 0, unroll=True)

    for grp in range(M_GROUPS):
        gl = slice(grp * M_GROUP_WIDTH, (grp + 1) * M_GROUP_WIDTH)
        y = y_scr[:, gl] * _silu(z_ref[:, gl])
        mixm_ref[:, gl] = _rms(y, mn_ref[:, gl]).astype(BF16)


def _sample_ssd(proj, c0, s0, cw, cb, dtb, alog, dsk, mn, *, nb=16):
    nseq = s0.shape[0]
    rows = nb * SEG
    kern = functools.partial(_sample_ssd_kernel, nb=nb)
    params = (cw, cb, dtb, alog, dsk, mn)
    conv_spec = pl.BlockSpec((M_CONV - 1, nb, M_CONV_DIM), lambda i: (0, i, 0))
    return pl.pallas_call(
        kern,
        grid=(nseq // nb,),
        in_specs=[pl.BlockSpec((rows, XD_WIDTH), lambda i: (i, OFF_XBC // XD_WIDTH)),
                  pl.BlockSpec((rows, M_WIDTH), lambda i: (i, OFF_Z // M_WIDTH)),
                  conv_spec,
                  pl.BlockSpec((nb, M_WIDTH, M_DSTATE), lambda i: (i, 0, 0))]
                 + [_const_spec(p.shape) for p in params],
        out_specs=[pl.BlockSpec((rows, M_WIDTH), lambda i: (i, 0)),
                   pl.BlockSpec((nb, M_WIDTH, M_DSTATE), lambda i: (i, 0, 0)),
                   conv_spec],
        out_shape=[jax.ShapeDtypeStruct((nseq * SEG, M_WIDTH), BF16),
                   jax.ShapeDtypeStruct(s0.shape, F32),
                   jax.ShapeDtypeStruct(c0.shape, F32)],
        scratch_shapes=[pltpu.VMEM((M_CONV_DIM // LANE, rows, LANE), F32),
                        pltpu.VMEM((M_CONV_DIM // LANE, rows, LANE), F32),
                        pltpu.VMEM((rows, M_WIDTH), F32),
                        pltpu.VMEM((rows, M_GROUPS * M_DSTATE), F32),
                        pltpu.VMEM((rows, M_GROUPS * M_DSTATE), BF16),
                        pltpu.VMEM((rows, M_WIDTH), F32),
                        pltpu.VMEM((rows, M_WIDTH), F32),
                        pltpu.VMEM((rows, M_WIDTH), F32),
                        pltpu.VMEM((rows, M_WIDTH), F32)],
        compiler_params=pltpu.CompilerParams(
            dimension_semantics=("arbitrary",), vmem_limit_bytes=VMEM_LIMIT),
        name="sample_ssd",
    )(proj, proj, c0, s0, *params)


def _out_mlp_kernel(x_ref, mh_ref, mm_ref, wo_ref, ln2_ref, wu_ref, wd_ref, lnf_ref, o_ref, *, ff_tile):
    x1 = x_ref[...] + _dot(mh_ref[...], wo_ref[0:HG_WIDTH, :]) + _dot(mm_ref[...], wo_ref[HG_WIDTH:, :])
    hn = _rms(x1, ln2_ref[...]).astype(BF16)
    mlp = None
    for j in range(D_FF // ff_tile):
        u = jnp.maximum(_dot(hn, wu_ref[:, j * ff_tile:(j + 1) * ff_tile]), 0.0)
        d = _dot((u * u).astype(BF16), wd_ref[j * ff_tile:(j + 1) * ff_tile, :])
        mlp = d if mlp is None else mlp + d
    o_ref[...] = _rms(x1 + mlp, lnf_ref[...])


def _out_mlp(x, mix_h, mix_m, w_out, ln2, w_up, w_down, ln_f, *, tm=512, ff_tile=1024):
    rows = x.shape[0]
    kern = functools.partial(_out_mlp_kernel, ff_tile=ff_tile)
    row_spec = lambda w: pl.BlockSpec((tm, w), lambda i: (i, 0))
    return pl.pallas_call(
        kern,
        grid=(rows // tm,),
        in_specs=[row_spec(D_MODEL), row_spec(HG_WIDTH), row_spec(M_WIDTH),
                  _const_spec(w_out.shape), _const_spec(ln2.shape), _const_spec(w_up.shape),
                  _const_spec(w_down.shape), _const_spec(ln_f.shape)],
        out_specs=row_spec(D_MODEL),
        out_shape=jax.ShapeDtypeStruct((rows, D_MODEL), F32),
        compiler_params=pltpu.CompilerParams(
            dimension_semantics=("arbitrary",), vmem_limit_bytes=VMEM_LIMIT),
        name="out_mlp",
    )(x, mix_h, mix_m, w_out, ln2, w_up, w_down, ln_f)


def _pad_lanes(v):
    return jnp.pad(v, ((0, 0), (0, LANE - v.shape[1])))


def kernel(x_prompt, x_sample, state_hgrn, state_ssm, state_conv, hg_lb_logits, ln1, w_in, hg_norm, conv_w,
           conv_b, dt_bias, a_log, d_skip, m_norm, w_out, ln2, w_up, w_down, ln_f):
    l = 0
    bp, seq, _ = x_prompt.shape
    bs, dseq, _ = x_sample.shape
    assert DEPTH == 1 and dseq == SEG

    w_all = _pack_in_proj(jnp.transpose(w_in[l]))
    lbl = hg_lb_logits.astype(F32)
    ln1_r = ln1[l][None, :]
    hgn_r = hg_norm[l].reshape(1, HG_WIDTH)
    cw = conv_w[l]
    cb_r = conv_b[l][None, :]
    dtb_r = _pad_lanes(dt_bias[l][None, :])
    alog_r = _pad_lanes(a_log[l][None, :])
    dsk_r = jnp.repeat(d_skip[l], M_HEADDIM)[None, :]
    mn_r = m_norm[l][None, :]
    mlp_w = (w_out[l].astype(BF16), ln2[l][None, :], w_up[l].astype(BF16), w_down[l].astype(BF16), ln_f[None, :])

    mixh_p, mixm_p, hgs_p, ssm_p, conv_p = _prompt_mixer(
        x_prompt, lbl, ln1_r, w_all, hgn_r, cw, cb_r, dtb_r, alog_r, dsk_r, mn_r)
    y_p = _out_mlp(x_prompt.reshape(bp * seq, D_MODEL), mixh_p.reshape(bp * seq, HG_WIDTH),
                   mixm_p.reshape(bp * seq, M_WIDTH), *mlp_w)

    xs2 = x_sample.reshape(bs * SEG, D_MODEL)
    proj_s = _sample_in_proj(xs2, ln1_r, w_all)
    mixh_s, hgs_s = _sample_hgrn(proj_s, state_hgrn[l], lbl, hgn_r)
    conv0 = jnp.transpose(state_conv[l], (1, 0, 2))
    mixm_s, ssm_s, conv8 = _sample_ssd(proj_s, conv0, state_ssm[l].reshape(bs, M_WIDTH, M_DSTATE),
                                       cw, cb_r, dtb_r, alog_r, dsk_r, mn_r)
    y_s = _out_mlp(xs2, mixh_s, mixm_s, *mlp_w)
    conv_s = jnp.transpose(conv8, (1, 0, 2))

    return (y_p.reshape(bp, seq, D_MODEL), y_s.reshape(bs, SEG, D_MODEL),
            hgs_p[None], hgs_s[None],
            ssm_p.reshape(1, bp, M_HEADS, M_HEADDIM, M_DSTATE), ssm_s.reshape(1, bs, M_HEADS, M_HEADDIM, M_DSTATE),
            conv_p[None], conv_s[None])
```
